```python
import math
import jax, jax.numpy as jnp
from jax import lax
import numpy as np

D_MODEL = 2048
BATCH = 1
SEQ = 8192
DEPTH = 4

GRID_W = 64
QBLK = 128
HEAD_DIM = 64
BRANCH_W = 512
N_BRANCH = 4
ROPE_THETA = 10000.0
EPS = 1e-6
NEG = -1e30

A_HEADS = 8
A_NOPE = 64
A_ROPE = 32
A_V = 64
A_QK = A_NOPE + A_ROPE
Q_LORA = 384
KV_LORA = 128

B_HEADS = 8
B_KV_HEADS = 2
B_GROUP = B_HEADS // B_KV_HEADS

C_HEADS = 8
C_CONFIGS = ((128, 1), (512, 4), (2048, 16))
C_GROUPS = len(C_CONFIGS)

D_HEADS = 4
D_V = 2 * HEAD_DIM

NUM_BUCKETS = 32
T5_MAX_DISTANCE = 1024
BIAS_HEADS = C_GROUPS * C_HEADS + D_HEADS

IN_SPLITS = (
    Q_LORA, KV_LORA, A_ROPE,
    B_HEADS * HEAD_DIM, B_KV_HEADS * HEAD_DIM, B_KV_HEADS * HEAD_DIM,
    C_GROUPS * C_HEADS * HEAD_DIM, C_GROUPS * C_HEADS * HEAD_DIM, C_GROUPS * C_HEADS * HEAD_DIM,
    D_HEADS * 2 * HEAD_DIM, D_HEADS * 2 * HEAD_DIM, D_HEADS * D_V,
    N_BRANCH * BRANCH_W,
    N_BRANCH * D_MODEL,
)
N_IN = sum(IN_SPLITS)

kernel_name = "hybrid_gated_mla_gqa_dilated_diff_encoder"


def rmsnorm(x, w):
    xf = x.astype(jnp.float32)
    y = xf * lax.rsqrt(jnp.mean(xf * xf, axis=-1, keepdims=True) + EPS)
    return (y * w.astype(jnp.float32)).astype(x.dtype)


def rope_cos_sin(pos, dim):
    inv = ROPE_THETA ** (-jnp.arange(0, dim, 2, dtype=jnp.float32) / dim)
    ang = pos.astype(jnp.float32)[:, None] * inv[None, :]
    return jnp.cos(ang), jnp.sin(ang)


def apply_rope(x, cos, sin):
    half = x.shape[-1] // 2
    xf = x.astype(jnp.float32)
    x1, x2 = xf[..., :half], xf[..., half:]
    c, s = cos[:, None, :], sin[:, None, :]
    return jnp.concatenate([x1 * c - x2 * s, x1 * s + x2 * c], axis=-1).astype(x.dtype)


def rel_bucket(rel):
    nb = NUM_BUCKETS // 2
    max_exact = nb // 2
    side = jnp.where(rel > 0, nb, 0)
    n = jnp.abs(rel)
    nf = jnp.maximum(n, 1).astype(jnp.float32)
    large = max_exact + (jnp.log(nf / max_exact) / math.log(T5_MAX_DISTANCE / max_exact)
                         * (nb - max_exact)).astype(jnp.int32)
    large = jnp.minimum(large, nb - 1)
    return side + jnp.where(n < max_exact, n, large)


def sweep(fn, n_tokens):
    starts = jnp.arange(n_tokens // QBLK, dtype=jnp.int32) * QBLK
    return lax.map(fn, starts)


def unblock(y):
    nb, b, q = y.shape[:3]
    return jnp.moveaxis(y, 0, 1).reshape((b, nb * q) + y.shape[3:])


def block_q(a, t0):
    return lax.dynamic_slice_in_dim(a, t0, QBLK, axis=1)


def mla_mixer(c_q, c_kv, k_r, q_norm_w, kv_norm_w, w_uq, w_ukv, qk_norm, cos, sin):
    b, s, _ = c_q.shape
    q = (rmsnorm(c_q, q_norm_w) @ w_uq).reshape(b, s, A_HEADS, A_QK)
    kv = (rmsnorm(c_kv, kv_norm_w) @ w_ukv).reshape(b, s, A_HEADS, A_NOPE + A_V)
    k_nope, v = kv[..., :A_NOPE], kv[..., A_NOPE:]
    k = jnp.concatenate([k_nope, jnp.broadcast_to(k_r[:, :, None, :], (b, s, A_HEADS, A_ROPE))], axis=-1)
    q = rmsnorm(q, qk_norm[0])
    k = rmsnorm(k, qk_norm[1])
    q = jnp.concatenate([q[..., :A_NOPE], apply_rope(q[..., A_NOPE:], cos, sin)], axis=-1)
    k = jnp.concatenate([k[..., :A_NOPE], apply_rope(k[..., A_NOPE:], cos, sin)], axis=-1)
    scale = A_QK ** -0.5

    def blk(t0):
        qb = block_q(q, t0)
        logits = jnp.einsum('bqhd,bshd->bhqs', qb, k).astype(jnp.float32) * scale
        p = jax.nn.softmax(logits, axis=-1).astype(v.dtype)
        return jnp.einsum('bhqs,bshd->bqhd', p, v)

    return unblock(sweep(blk, s)).reshape(b, s, A_HEADS * A_V)


def gqa_axial_mixer(q, k, v, qk_norm, row_cs, col_cs):
    b, s, _ = q.shape
    half = HEAD_DIM // 2
    q = rmsnorm(q.reshape(b, s, B_HEADS, HEAD_DIM), qk_norm[0])
    k = rmsnorm(k.reshape(b, s, B_KV_HEADS, HEAD_DIM), qk_norm[1])
    v = v.reshape(b, s, B_KV_HEADS, HEAD_DIM)

    def axial(a):
        return jnp.concatenate([apply_rope(a[..., :half], *row_cs), apply_rope(a[..., half:], *col_cs)], axis=-1)

    q = axial(q).reshape(b, s, B_KV_HEADS, B_GROUP, HEAD_DIM)
    k = axial(k)
    scale = HEAD_DIM ** -0.5

    def blk(t0):
        qb = block_q(q, t0)
        logits = jnp.einsum('bqkgd,bskd->bkgqs', qb, k).astype(jnp.float32) * scale
        p = jax.nn.softmax(logits, axis=-1).astype(v.dtype)
        return jnp.einsum('bkgqs,bskd->bqkgd', p, v)

    return unblock(sweep(blk, s)).reshape(b, s, B_HEADS * HEAD_DIM)


def dilated_mixer(q, k, v, qk_norm, bias_table):
    b, s, _ = q.shape
    shp = (b, s, C_GROUPS, C_HEADS, HEAD_DIM)
    q = rmsnorm(q.reshape(shp), qk_norm[0][:, None, :])
    k = rmsnorm(k.reshape(shp), qk_norm[1][:, None, :])
    v = v.reshape(shp)
    scale = HEAD_DIM ** -0.5
    outs, lses = [], []
    for g, (window, dil) in enumerate(C_CONFIGS):
        n_side = window // (2 * dil)
        offs = dil * jnp.arange(-n_side, n_side + 1, dtype=jnp.int32)
        bias = bias_table[rel_bucket(offs)][:, g * C_HEADS:(g + 1) * C_HEADS].T
        qg, kg, vg = q[:, :, g], k[:, :, g], v[:, :, g]

        def blk(t0):
            qb = block_q(qg, t0)
            idx = t0 + jnp.arange(QBLK, dtype=jnp.int32)[:, None] + offs[None, :]
            valid = (idx >= 0) & (idx < s)
            idxc = jnp.clip(idx, 0, s - 1)
            kb = kg[:, idxc]
            vb = vg[:, idxc]
            logits = jnp.einsum('bqhd,bqjhd->bhqj', qb, kb).astype(jnp.float32) * scale
            logits = logits + bias[None, :, None, :].astype(jnp.float32)
            logits = jnp.where(valid[None, None], logits, NEG)
            lse = jax.nn.logsumexp(logits, axis=-1)
            p = jnp.exp(logits - lse[..., None]).astype(vb.dtype)
            o = jnp.einsum('bhqj,bqjhd->bqhd', p, vb)
            return o, jnp.swapaxes(lse, 1, 2)

        o, lse = sweep(blk, s)
        outs.append(unblock(o))
        lses.append(unblock(lse))
    wts = jax.nn.softmax(jnp.stack(lses, axis=0), axis=0)
    out = jnp.einsum('gbsh,gbshd->bshd', wts.astype(v.dtype), jnp.stack(outs, axis=0))
    return out.reshape(b, s, C_HEADS * HEAD_DIM)


def diff_mixer(q, k, v, qk_norm, lam_vecs, subnorm_w, bias_table, lambda_init):
    b, s, _ = q.shape
    q = rmsnorm(q.reshape(b, s, D_HEADS, 2, HEAD_DIM), qk_norm[0])
    k = rmsnorm(k.reshape(b, s, D_HEADS, 2, HEAD_DIM), qk_norm[1])
    v = v.reshape(b, s, D_HEADS, D_V)
    lv = lam_vecs.astype(jnp.float32)
    lam = jnp.exp(jnp.sum(lv[0] * lv[1])) - jnp.exp(jnp.sum(lv[2] * lv[3])) + lambda_init
    scale = HEAD_DIM ** -0.5
    kpos = jnp.arange(s, dtype=jnp.int32)

    def blk(t0):
        qb = block_q(q, t0)
        qpos = t0 + jnp.arange(QBLK, dtype=jnp.int32)
        bias = jnp.transpose(bias_table[rel_bucket(kpos[None, :] - qpos[:, None])], (2, 0, 1))
        logits = jnp.einsum('bqhmd,bshmd->bhmqs', qb, k).astype(jnp.float32) * scale
        logits = logits + bias[None, :, None].astype(jnp.float32)
        p = jax.nn.softmax(logits, axis=-1)
        attn = (p[:, :, 0] - lam * p[:, :, 1]).astype(v.dtype)
        return jnp.einsum('bhqs,bshe->bqhe', attn, v)

    o = unblock(sweep(blk, s))
    o = rmsnorm(o, subnorm_w) * (1.0 - lambda_init)
    return o.reshape(b, s, D_HEADS * D_V)


def setup_inputs(seed: int = 0) -> dict:
    key = jax.random.key(seed)
    ks = jax.random.split(key, 16)
    f32 = jnp.float32

    def nrm(k, shape, scale):
        return jax.random.normal(k, shape, f32) * scale

    def gain(k, shape):
        return 1.0 + nrm(k, shape, 0.02)

    return {
        "x": nrm(ks[0], (BATCH, SEQ, D_MODEL), 1.0),
        "norm_w": gain(ks[1], (DEPTH, D_MODEL)),
        "w_in": nrm(ks[2], (DEPTH, D_MODEL, N_IN), D_MODEL ** -0.5),
        "mla_q_norm": gain(ks[3], (DEPTH, Q_LORA)),
        "mla_kv_norm": gain(ks[4], (DEPTH, KV_LORA)),
        "mla_w_uq": nrm(ks[5], (DEPTH, Q_LORA, A_HEADS * A_QK), Q_LORA ** -0.5),
        "mla_w_ukv": nrm(ks[6], (DEPTH, KV_LORA, A_HEADS * (A_NOPE + A_V)), KV_LORA ** -0.5),
        "mla_qk_norm": gain(ks[7], (DEPTH, 2, A_QK)),
        "gqa_qk_norm": gain(ks[8], (DEPTH, 2, HEAD_DIM)),
        "dil_qk_norm": gain(ks[9], (DEPTH, 2, C_GROUPS, HEAD_DIM)),
        "diff_qk_norm": gain(ks[10], (DEPTH, 2, HEAD_DIM)),
        "diff_lambda": nrm(ks[11], (DEPTH, 4, HEAD_DIM), 0.1),
        "diff_subnorm": gain(ks[12], (DEPTH, D_V)),
        "rel_bias": nrm(ks[13], (NUM_BUCKETS, BIAS_HEADS), 0.2),
        "w_branch": nrm(ks[14], (DEPTH, N_BRANCH, BRANCH_W, D_MODEL), BRANCH_W ** -0.5),
        "w_out": nrm(ks[15], (DEPTH, D_MODEL, D_MODEL), 0.5 * D_MODEL ** -0.5),
    }


def reference(x, norm_w, w_in, mla_q_norm, mla_kv_norm, mla_w_uq, mla_w_ukv, mla_qk_norm,
              gqa_qk_norm, dil_qk_norm, diff_qk_norm, diff_lambda, diff_subnorm, rel_bias,
              w_branch, w_out):
    b, s, _ = x.shape
    rows = s // GRID_W
    pos = jnp.arange(s, dtype=jnp.int32)
    row_pos = jnp.repeat(jnp.arange(rows, dtype=jnp.int32), GRID_W)
    col_pos = jnp.tile(jnp.arange(GRID_W, dtype=jnp.int32), rows)
    a_cs = rope_cos_sin(pos, A_ROPE)
    row_cs = rope_cos_sin(row_pos, HEAD_DIM // 2)
    col_cs = rope_cos_sin(col_pos, HEAD_DIM // 2)
    split_pts = [int(p) for p in np.cumsum(IN_SPLITS)[:-1]]
    bias_c = rel_bias[:, :C_GROUPS * C_HEADS]
    bias_d = rel_bias[:, C_GROUPS * C_HEADS:]

    for l in range(DEPTH):
        h = rmsnorm(x, norm_w[l])
        proj = h @ w_in[l]
        (a_cq, a_ckv, a_kr, b_q, b_k, b_v, c_q, c_k, c_v,
         d_q, d_k, d_v, silu_g, merge_g) = jnp.split(proj, split_pts, axis=-1)

        y_a = mla_mixer(a_cq, a_ckv, a_kr, mla_q_norm[l], mla_kv_norm[l], mla_w_uq[l],
                        mla_w_ukv[l], mla_qk_norm[l], *a_cs)
        y_b = gqa_axial_mixer(b_q, b_k, b_v, gqa_qk_norm[l], row_cs, col_cs)
        y_c = dilated_mixer(c_q, c_k, c_v, dil_qk_norm[l], bias_c)
        lambda_init = 0.8 - 0.6 * math.exp(-0.3 * l)
        y_d = diff_mixer(d_q, d_k, d_v, diff_qk_norm[l], diff_lambda[l], diff_subnorm[l],
                         bias_d, lambda_init)

        y = jnp.stack([y_a, y_b, y_c, y_d], axis=2)
        y = y * jax.nn.silu(silu_g.reshape(b, s, N_BRANCH, BRANCH_W))
        z = jnp.einsum('bsnc,ncd->bsnd', y, w_branch[l])
        gates = jax.nn.sigmoid(merge_g.reshape(b, s, N_BRANCH, D_MODEL))
        mixed = jnp.sum(gates * z, axis=2)
        x = x + mixed @ w_out[l]
    return x
```

```python
import functools
import math

import numpy as np
import jax
import jax.numpy as jnp
from jax import lax
from jax.experimental import pallas as pl
from jax.experimental.pallas import tpu as pltpu

D_MODEL = 2048
GRID_W = 64
HEAD_DIM = 64
BRANCH_W = 512
N_BRANCH = 4
ROPE_THETA = 10000.0
EPS = 1e-6
NEG = -1e30

A_HEADS = 8
A_NOPE = 64
A_ROPE = 32
A_V = 64
A_QK = A_NOPE + A_ROPE
Q_LORA = 384
KV_LORA = 128

B_HEADS = 8
B_KV_HEADS = 2

C_HEADS = 8
C_CONFIGS = ((128, 1), (512, 4), (2048, 16))
C_GROUPS = len(C_CONFIGS)

D_HEADS = 4
D_V = 2 * HEAD_DIM

NUM_BUCKETS = 32
T5_MAX_DISTANCE = 1024

LANES = 128

OFF_SILU = 0
OFF_CQ, OFF_CK, OFF_CV = 2048, 3584, 5120
OFF_DQ, OFF_DK, OFF_DV = 6656, 7168, 7680
OFF_BQ, OFF_BKV = 8192, 8704
OFF_A = 8960
A_BLOCK = 640
N_MAIN = 9600

SRC_A, SRC_B, SRC_C, SRC_D, SRC_SILU, SRC_MERGE = 0, 544, 1312, 5920, 7456, 9504

VMEM_LIMIT = 56 * 1024 * 1024

C_QT = 128
C_PAD = (128, 256, 1024)
C_WIN = tuple(C_QT + 2 * p for p in C_PAD)

D_TQ = 512
D_TK = 512
D_BAND = 1152


def _cparams(sem):
    return pltpu.CompilerParams(dimension_semantics=sem, vmem_limit_bytes=VMEM_LIMIT)


def _rmsnorm_kernel(x_ref, w_ref, o_ref):
    x = x_ref[...]
    ms = jnp.mean(x * x, axis=-1, keepdims=True)
    o_ref[...] = (x * lax.rsqrt(ms + EPS) * w_ref[...]).astype(o_ref.dtype)


def _rmsnorm(x, w, tm=512):
    s, d = x.shape
    return pl.pallas_call(
        _rmsnorm_kernel,
        grid=(s // tm,),
        in_specs=[pl.BlockSpec((tm, d), lambda i: (i, 0)),
                  pl.BlockSpec((1, d), lambda i: (0, 0))],
        out_specs=pl.BlockSpec((tm, d), lambda i: (i, 0)),
        out_shape=jax.ShapeDtypeStruct((s, d), jnp.bfloat16),
        compiler_params=_cparams(("parallel",)),
        name="rmsnorm",
    )(x, w)


def _matmul_kernel(a_ref, b_ref, o_ref):
    o_ref[...] = jnp.dot(a_ref[...], b_ref[...], preferred_element_type=jnp.float32)


def _in_proj(h, w_all, layer, tm=1024, tn=1920):
    s, d = h.shape
    n = w_all.shape[-1]
    return pl.pallas_call(
        _matmul_kernel,
        grid=(s // tm, n // tn),
        in_specs=[pl.BlockSpec((tm, d), lambda i, j: (i, 0)),
                  pl.BlockSpec((None, d, tn), lambda i, j: (layer, 0, j))],
        out_specs=pl.BlockSpec((tm, tn), lambda i, j: (i, j)),
        out_shape=jax.ShapeDtypeStruct((s, n), jnp.float32),
        compiler_params=_cparams(("parallel", "arbitrary")),
        name="in_proj",
    )(h, w_all)


def _lane_lo(shape):
    return lax.broadcasted_iota(jnp.int32, shape, len(shape) - 1) < HEAD_DIM


def _rope(x, c, sa, sb):
    return x * c + pltpu.roll(x, LANES - 16, 1) * sa + pltpu.roll(x, 16, 1) * sb


def _halfnorm(x, gain, lo):
    sq = x * x
    s_lo = jnp.sum(jnp.where(lo, sq, 0.0), axis=-1, keepdims=True)
    s_hi = jnp.sum(jnp.where(lo, 0.0, sq), axis=-1, keepdims=True)
    r = jnp.where(lo, lax.rsqrt(s_lo * (1.0 / HEAD_DIM) + EPS),
                  lax.rsqrt(s_hi * (1.0 / HEAD_DIM) + EPS))
    return x * r * gain


def _prep_a_kernel(p_ref, qn_ref, kvn_ref, wuq_ref, wukv_ref, gq_ref, gk_ref,
                   c_ref, sa_ref, sb_ref, q_out, kt_out, v_out):
    p = p_ref[...]
    cq = p[:, :Q_LORA]
    cq = cq * lax.rsqrt(jnp.mean(cq * cq, axis=-1, keepdims=True) + EPS) * qn_ref[...]
    q = jnp.dot(cq.astype(jnp.bfloat16), wuq_ref[...], preferred_element_type=jnp.float32)
    ckv = p[:, Q_LORA:Q_LORA + KV_LORA]
    ckv = ckv * lax.rsqrt(jnp.mean(ckv * ckv, axis=-1, keepdims=True) + EPS) * kvn_ref[...]
    kvu = jnp.dot(ckv.astype(jnp.bfloat16), wukv_ref[...], preferred_element_type=jnp.float32)
    v_out[...] = kvu.astype(v_out.dtype)
    kr = pltpu.roll(p[:, Q_LORA + KV_LORA:], HEAD_DIM, 1)
    lo = _lane_lo(kr.shape)
    c, sa, sb = c_ref[...], sa_ref[...], sb_ref[...]
    scale = A_QK ** -0.5
    for h in range(A_HEADS):
        qh = q[:, h * LANES:(h + 1) * LANES]
        ss = jnp.sum(qh * qh, axis=-1, keepdims=True) * (1.0 / A_QK)
        qh = qh * lax.rsqrt(ss + EPS) * gq_ref[...]
        q_out[h] = (_rope(qh, c, sa, sb) * scale).astype(q_out.dtype)
        kh = jnp.where(lo, kvu[:, h * LANES:(h + 1) * LANES], kr)
        ss = jnp.sum(kh * kh, axis=-1, keepdims=True) * (1.0 / A_QK)
        kh = kh * lax.rsqrt(ss + EPS) * gk_ref[...]
        kt_out[h] = _rope(kh, c, sa, sb).T.astype(kt_out.dtype)


def _prep_a(proj, qn, kvn, wuq, wukv, gq, gk, tabs, tm=256):
    s = proj.shape[0]
    row = lambda w: pl.BlockSpec((1, w), lambda i: (0, 0))
    tab = pl.BlockSpec((tm, LANES), lambda i: (i, 0))
    return pl.pallas_call(
        _prep_a_kernel,
        grid=(s // tm,),
        in_specs=[pl.BlockSpec((tm, A_BLOCK), lambda i: (i, OFF_A // A_BLOCK)),
                  row(Q_LORA), row(KV_LORA),
                  pl.BlockSpec(wuq.shape, lambda i: (0, 0)),
                  pl.BlockSpec(wukv.shape, lambda i: (0, 0)),
                  row(LANES), row(LANES), tab, tab, tab],
        out_specs=[pl.BlockSpec((A_HEADS, tm, LANES), lambda i: (0, i, 0)),
                   pl.BlockSpec((A_HEADS, LANES, tm), lambda i: (0, 0, i)),
                   pl.BlockSpec((tm, A_HEADS * LANES), lambda i: (i, 0))],
        out_shape=[jax.ShapeDtypeStruct((A_HEADS, s, LANES), jnp.bfloat16),
                   jax.ShapeDtypeStruct((A_HEADS, LANES, s), jnp.bfloat16),
                   jax.ShapeDtypeStruct((s, A_HEADS * LANES), jnp.bfloat16)],
        compiler_params=_cparams(("parallel",)),
        name="prep_a",
    )(proj, qn, kvn, wuq, wukv, gq, gk, *tabs)


def _prep_b_kernel(q_ref, kv_ref, gq_ref, gk_ref, c_ref, sa_ref, sb_ref,
                   q_out, kt_out, v_out):
    c, sa, sb = c_ref[...], sa_ref[...], sb_ref[...]
    lo = _lane_lo(c.shape)
    scale = HEAD_DIM ** -0.5
    q = q_ref[...]
    for b in range(B_HEADS // 2):
        x = _rope(_halfnorm(q[:, b * LANES:(b + 1) * LANES], gq_ref[...], lo), c, sa, sb) * scale
        xr = pltpu.roll(x, HEAD_DIM, 1)
        g = b // 2
        even, odd = (x, xr) if g == 0 else (xr, x)
        keep = lo if g == 0 else jnp.logical_not(lo)
        q_out[g, 2 * (b % 2)] = jnp.where(keep, even, 0.0).astype(q_out.dtype)
        q_out[g, 2 * (b % 2) + 1] = jnp.where(keep, odd, 0.0).astype(q_out.dtype)
    kv = kv_ref[...]
    k = _rope(_halfnorm(kv[:, :LANES], gk_ref[...], lo), c, sa, sb)
    kt_out[...] = k.T.astype(kt_out.dtype)
    v = kv[:, LANES:]
    vr = pltpu.roll(v, HEAD_DIM, 1)
    v_out[0] = jnp.where(lo, v, vr).astype(v_out.dtype)
    v_out[1] = jnp.where(lo, vr, v).astype(v_out.dtype)


def _prep_b(proj, gq, gk, tabs, tm=256):
    s = proj.shape[0]
    row = pl.BlockSpec((1, LANES), lambda i: (0, 0))
    tab = pl.BlockSpec((tm, LANES), lambda i: (i, 0))
    return pl.pallas_call(
        _prep_b_kernel,
        grid=(s // tm,),
        in_specs=[pl.BlockSpec((tm, 512), lambda i: (i, OFF_BQ // 512)),
                  pl.BlockSpec((tm, 256), lambda i: (i, OFF_BKV // 256)),
                  row, row, tab, tab, tab],
        out_specs=[pl.BlockSpec((2, 4, tm, LANES), lambda i: (0, 0, i, 0)),
                   pl.BlockSpec((LANES, tm), lambda i: (0, i)),
                   pl.BlockSpec((2, tm, LANES), lambda i: (0, i, 0))],
        out_shape=[jax.ShapeDtypeStruct((2, 4, s, LANES), jnp.bfloat16),
                   jax.ShapeDtypeStruct((LANES, s), jnp.bfloat16),
                   jax.ShapeDtypeStruct((2, s, LANES), jnp.bfloat16)],
        compiler_params=_cparams(("parallel",)),
        name="prep_b",
    )(proj, proj, gq, gk, *tabs)


def _prep_d_kernel(q_ref, k_ref, v_ref, gq_ref, gk_ref, q_out, kt_out, v_out):
    q, k = q_ref[...], k_ref[...]
    lo = _lane_lo((q.shape[0], LANES))
    scale = HEAD_DIM ** -0.5
    for h in range(D_HEADS):
        x = _halfnorm(q[:, h * LANES:(h + 1) * LANES], gq_ref[...], lo) * scale
        q_out[h, 0] = jnp.where(lo, x, 0.0).astype(q_out.dtype)
        q_out[h, 1] = jnp.where(lo, 0.0, x).astype(q_out.dtype)
        y = _halfnorm(k[:, h * LANES:(h + 1) * LANES], gk_ref[...], lo)
        kt_out[h] = y.T.astype(kt_out.dtype)
    v_out[...] = v_ref[...].astype(v_out.dtype)


def _prep_d(proj, gq, gk, tm=256):
    s = proj.shape[0]
    row = pl.BlockSpec((1, LANES), lambda i: (0, 0))
    blk = lambda off: pl.BlockSpec((tm, 512), lambda i: (i, off // 512))
    return pl.pallas_call(
        _prep_d_kernel,
        grid=(s // tm,),
        in_specs=[blk(OFF_DQ), blk(OFF_DK), blk(OFF_DV), row, row],
        out_specs=[pl.BlockSpec((D_HEADS, 2, tm, LANES), lambda i: (0, 0, i, 0)),
                   pl.BlockSpec((D_HEADS, LANES, tm), lambda i: (0, 0, i)),
                   pl.BlockSpec((tm, 512), lambda i: (i, 0))],
        out_shape=[jax.ShapeDtypeStruct((D_HEADS, 2, s, LANES), jnp.bfloat16),
                   jax.ShapeDtypeStruct((D_HEADS, LANES, s), jnp.bfloat16),
                   jax.ShapeDtypeStruct((s, 512), jnp.bfloat16)],
        compiler_params=_cparams(("parallel",)),
        name="prep_d",
    )(proj, proj, proj, gq, gk)


def _prep_c_kernel(q_ref, k_ref, v_ref, gq_ref, gk_ref, q_out, kt_out, v_out, *, npad, nt):
    i = pl.program_id(0)
    lo = _lane_lo(q_ref.shape)
    scale = HEAD_DIM ** -0.5
    q_out[...] = (_halfnorm(q_ref[...], gq_ref[...], lo) * scale).astype(q_out.dtype)
    is_pad = jnp.logical_or(i < npad, i >= npad + nt)

    @pl.when(is_pad)
    def _():
        kt_out[...] = jnp.zeros_like(kt_out)
        v_out[...] = jnp.zeros_like(v_out)

    @pl.when(jnp.logical_not(is_pad))
    def _():
        kt_out[...] = _halfnorm(k_ref[...], gk_ref[...], lo).T.astype(kt_out.dtype)
        v_out[...] = v_ref[...].astype(v_out.dtype)


def _prep_c(proj, gq, gk, g, tm=128):
    s = proj.shape[0]
    pad = C_PAD[g]
    npad, nt = pad // tm, s // tm
    nb = C_HEADS // 2
    src = lambda i: jnp.clip(i - npad, 0, nt - 1)
    blk = lambda off: pl.BlockSpec((tm, LANES), lambda i, b: (src(i), off // LANES + g * nb + b))
    row = pl.BlockSpec((1, LANES), lambda i, b: (0, 0))
    return pl.pallas_call(
        functools.partial(_prep_c_kernel, npad=npad, nt=nt),
        grid=(nt + 2 * npad, nb),
        in_specs=[blk(OFF_CQ), blk(OFF_CK), blk(OFF_CV), row, row],
        out_specs=[pl.BlockSpec((tm, LANES), lambda i, b: (src(i), b)),
                   pl.BlockSpec((LANES, tm), lambda i, b: (b, i)),
                   pl.BlockSpec((tm, LANES), lambda i, b: (i, b))],
        out_shape=[jax.ShapeDtypeStruct((s, nb * LANES), jnp.bfloat16),
                   jax.ShapeDtypeStruct((nb * LANES, s + 2 * pad), jnp.bfloat16),
                   jax.ShapeDtypeStruct((s + 2 * pad, nb * LANES), jnp.bfloat16)],
        compiler_params=_cparams(("arbitrary", "arbitrary")),
        name=f"prep_c{g}",
    )(proj, proj, proj, gq, gk)


def _online_update(s, v, m_ref, l_ref, acc_ref, r0, rows):
    tk = s.shape[1]
    sl = pl.ds(r0, rows)
    m_prev = m_ref[sl, :]
    m_cur = jnp.max(s, axis=1, keepdims=True)
    m_next = jnp.maximum(m_prev, m_cur)
    p = jnp.exp(s - pltpu.repeat(m_next, tk // LANES, 1))
    alpha = jnp.exp(m_prev - m_next)
    l_ref[sl, :] = alpha * l_ref[sl, :] + jnp.sum(p, axis=1, keepdims=True)
    m_ref[sl, :] = m_next
    pv = jnp.dot(p.astype(jnp.bfloat16), v, preferred_element_type=jnp.float32)
    acc_ref[sl, :] = alpha * acc_ref[sl, :] + pv


def _init_stats(m_ref, l_ref, acc_ref):
    m_ref[...] = jnp.full_like(m_ref, -jnp.inf)
    l_ref[...] = jnp.zeros_like(l_ref)
    acc_ref[...] = jnp.zeros_like(acc_ref)


def _attn_a_kernel(q_ref, kt_ref, v_ref, o_ref, m_ref, l_ref, acc_ref, *, tq, tk, nk):
    _init_stats(m_ref, l_ref, acc_ref)

    def body(c, carry):
        k0 = pl.multiple_of(c * tk, tk)
        for hh in range(2):
            s = jnp.dot(q_ref[hh], kt_ref[hh, :, pl.ds(k0, tk)],
                        preferred_element_type=jnp.float32)
            v = v_ref[pl.ds(k0, tk), hh * LANES:(hh + 1) * LANES]
            _online_update(s, v, m_ref, l_ref, acc_ref, hh * tq, tq)
        return carry

    lax.fori_loop(0, nk, body, 0)
    o = acc_ref[...] / l_ref[...]
    lo = _lane_lo((tq, LANES))
    o_ref[...] = jnp.where(lo, pltpu.roll(o[:tq], HEAD_DIM, 1), o[tq:])


def _attn_a(q, kt, v, tq=512, tk=512):
    s = q.shape[1]
    return pl.pallas_call(
        functools.partial(_attn_a_kernel, tq=tq, tk=tk, nk=s // tk),
        grid=(A_HEADS // 2, s // tq),
        in_specs=[pl.BlockSpec((2, tq, LANES), lambda p, i: (p, i, 0)),
                  pl.BlockSpec((2, LANES, s), lambda p, i: (p, 0, 0)),
                  pl.BlockSpec((s, 2 * LANES), lambda p, i: (0, p))],
        out_specs=pl.BlockSpec((tq, LANES), lambda p, i: (i, p)),
        out_shape=jax.ShapeDtypeStruct((s, BRANCH_W), jnp.float32),
        scratch_shapes=[pltpu.VMEM((2 * tq, LANES), jnp.float32)] * 3,
        compiler_params=_cparams(("parallel", "arbitrary")),
        name="attn_a",
    )(q, kt, v)


def _attn_b_kernel(q_ref, kt_ref, v_ref, o_ref, m_ref, l_ref, acc_ref, *, tq, tk, nk):
    _init_stats(m_ref, l_ref, acc_ref)
    q = q_ref[...].reshape(4 * tq, LANES)

    def body(c, carry):
        k0 = pl.multiple_of(c * tk, tk)
        s = jnp.dot(q, kt_ref[:, pl.ds(k0, tk)], preferred_element_type=jnp.float32)
        _online_update(s, v_ref[pl.ds(k0, tk), :], m_ref, l_ref, acc_ref, 0, 4 * tq)
        return carry

    lax.fori_loop(0, nk, body, 0)
    o = acc_ref[...] / l_ref[...]
    lo = _lane_lo((tq, LANES))
    o_ref[:, :LANES] = jnp.where(lo, o[:tq], o[tq:2 * tq])
    o_ref[:, LANES:] = jnp.where(lo, o[2 * tq:3 * tq], o[3 * tq:])


def _attn_b(q, kt, v, tq=256, tk=512):
    s = q.shape[2]
    return pl.pallas_call(
        functools.partial(_attn_b_kernel, tq=tq, tk=tk, nk=s // tk),
        grid=(2, s // tq),
        in_specs=[pl.BlockSpec((None, 4, tq, LANES), lambda g, i: (g, 0, i, 0)),
                  pl.BlockSpec((LANES, s), lambda g, i: (0, 0)),
                  pl.BlockSpec((None, s, LANES), lambda g, i: (g, 0, 0))],
        out_specs=pl.BlockSpec((tq, 2 * LANES), lambda g, i: (i, g)),
        out_shape=jax.ShapeDtypeStruct((s, BRANCH_W), jnp.float32),
        scratch_shapes=[pltpu.VMEM((4 * tq, LANES), jnp.float32)] * 3,
        compiler_params=_cparams(("parallel", "arbitrary")),
        name="attn_b",
    )(q, kt, v)


def _attn_d_kernel(q_ref, kt_ref, v_ref, band_ref, lam_ref, sub_ref, o_ref,
                   m_ref, l_ref, acc_ref, *, tq, tk, nk, lambda_init):
    _init_stats(m_ref, l_ref, acc_ref)
    q = q_ref[...].reshape(2 * tq, LANES)
    q0 = pl.program_id(1) * tq
    hi = tq + 2 * D_BAND - tk

    def body(c, carry):
        k0 = pl.multiple_of(c * tk, tk)
        s = jnp.dot(q, kt_ref[:, pl.ds(k0, tk)], preferred_element_type=jnp.float32)
        b0 = pl.multiple_of(jnp.clip(k0 - q0 + D_BAND, 0, hi), LANES)
        bias = band_ref[:, pl.ds(b0, tk)]
        s = (s.reshape(2, tq, tk) + bias[None]).reshape(2 * tq, tk)
        _online_update(s, v_ref[pl.ds(k0, tk), :], m_ref, l_ref, acc_ref, 0, 2 * tq)
        return carry

    lax.fori_loop(0, nk, body, 0)
    lv = lam_ref[...]
    lam = (jnp.exp(jnp.sum(lv[0:1] * lv[1:2], axis=-1, keepdims=True))
           - jnp.exp(jnp.sum(lv[2:3] * lv[3:4], axis=-1, keepdims=True)) + lambda_init)
    o = acc_ref[...] / l_ref[...]
    o = o[:tq] - lam * o[tq:]
    o = o * lax.rsqrt(jnp.mean(o * o, axis=-1, keepdims=True) + EPS) * sub_ref[...]
    o_ref[...] = o * (1.0 - lambda_init)


def _attn_d(q, kt, v, band, lam_vecs, subw, lambda_init, tq=D_TQ, tk=D_TK):
    s = q.shape[2]
    return pl.pallas_call(
        functools.partial(_attn_d_kernel, tq=tq, tk=tk, nk=s // tk, lambda_init=lambda_init),
        grid=(D_HEADS, s // tq),
        in_specs=[pl.BlockSpec((None, 2, tq, LANES), lambda h, i: (h, 0, i, 0)),
                  pl.BlockSpec((None, LANES, s), lambda h, i: (h, 0, 0)),
                  pl.BlockSpec((s, LANES), lambda h, i: (0, h)),
                  pl.BlockSpec((None, tq, tq + 2 * D_BAND), lambda h, i: (h, 0, 0)),
                  pl.BlockSpec((4, HEAD_DIM), lambda h, i: (0, 0)),
                  pl.BlockSpec((1, LANES), lambda h, i: (0, 0))],
        out_specs=pl.BlockSpec((tq, LANES), lambda h, i: (i, h)),
        out_shape=jax.ShapeDtypeStruct((s, BRANCH_W), jnp.float32),
        scratch_shapes=[pltpu.VMEM((2 * tq, LANES), jnp.float32)] * 3,
        compiler_params=_cparams(("parallel", "arbitrary")),
        name="attn_d",
    )(q, kt, v, band, lam_vecs, subw)


def _attn_c_kernel(q0_ref, q1_ref, q2_ref, kt0_ref, kt1_ref, kt2_ref, v0_ref, v1_ref, v2_ref,
                   b0_ref, b1_ref, b2_ref, o_ref, *, seq):
    t0 = pl.multiple_of(pl.program_id(1) * C_QT, C_QT)
    lo = _lane_lo((C_QT, LANES))
    q_refs = (q0_ref, q1_ref, q2_ref)
    kt_refs = (kt0_ref, kt1_ref, kt2_ref)
    v_refs = (v0_ref, v1_ref, v2_ref)
    b_refs = (b0_ref, b1_ref, b2_ref)
    outs = []
    for hh in range(2):
        keep = lo if hh == 0 else jnp.logical_not(lo)
        logits = []
        for g in range(C_GROUPS):
            w = C_WIN[g]
            q = jnp.where(keep, q_refs[g][...], 0.0).astype(jnp.bfloat16)
            s = jnp.dot(q, kt_refs[g][:, pl.ds(t0, w)], preferred_element_type=jnp.float32)
            kidx = t0 - C_PAD[g] + lax.broadcasted_iota(jnp.int32, (1, w), 1)
            pen = jnp.where(jnp.logical_and(kidx >= 0, kidx < seq), 0.0, NEG)
            logits.append(s + b_refs[g][hh] + pen)
        m = functools.reduce(jnp.maximum, [jnp.max(s, axis=1, keepdims=True) for s in logits])
        l = 0.0
        acc = 0.0
        for g in range(C_GROUPS):
            p = jnp.exp(logits[g] - m)
            l = l + jnp.sum(p, axis=1, keepdims=True)
            acc = acc + jnp.dot(p.astype(jnp.bfloat16), v_refs[g][pl.ds(t0, C_WIN[g]), :],
                                preferred_element_type=jnp.float32)
        outs.append(acc / l)
    o_ref[...] = jnp.where(lo, outs[0], outs[1])


def _attn_c(qs, kts, vs, bands, seq):
    nb = C_HEADS // 2
    qspec = pl.BlockSpec((C_QT, LANES), lambda b, i: (i, b))
    in_specs = [qspec] * 3
    in_specs += [pl.BlockSpec((LANES, seq + 2 * C_PAD[g]), lambda b, i: (b, 0)) for g in range(3)]
    in_specs += [pl.BlockSpec((seq + 2 * C_PAD[g], LANES), lambda b, i: (0, b)) for g in range(3)]
    in_specs += [pl.BlockSpec((2, C_QT, C_WIN[g]), lambda b, i: (b, 0, 0)) for g in range(3)]
    return pl.pallas_call(
        functools.partial(_attn_c_kernel, seq=seq),
        grid=(nb, seq // C_QT),
        in_specs=in_specs,
        out_specs=pl.BlockSpec((C_QT, LANES), lambda b, i: (i, b)),
        out_shape=jax.ShapeDtypeStruct((seq, BRANCH_W), jnp.float32),
        compiler_params=_cparams(("parallel", "arbitrary")),
        name="attn_c",
    )(*qs, *kts, *vs, *bands)


def _merge_kernel(x_ref, h_ref, sg_ref, ya_ref, yb_ref, yc_ref, yd_ref,
                  wm0_ref, wm1_ref, wm2_ref, wm3_ref, wb_ref, wo_ref, o_ref, yg_ref):
    c = pl.program_id(1)

    @pl.when(c == 0)
    def _():
        sg = sg_ref[...]
        for n, y_ref in enumerate((ya_ref, yb_ref, yc_ref, yd_ref)):
            g = sg[:, n * BRANCH_W:(n + 1) * BRANCH_W]
            yg_ref[n] = (y_ref[...] * (g * jax.nn.sigmoid(g))).astype(yg_ref.dtype)
        o_ref[...] = x_ref[...]

    h = h_ref[...]
    mixed = 0.0
    for n, wm_ref in enumerate((wm0_ref, wm1_ref, wm2_ref, wm3_ref)):
        gate = jax.nn.sigmoid(jnp.dot(h, wm_ref[...], preferred_element_type=jnp.float32))
        z = jnp.dot(yg_ref[n], wb_ref[n], preferred_element_type=jnp.float32)
        mixed = mixed + gate * z
    o_ref[...] += jnp.dot(mixed.astype(jnp.bfloat16), wo_ref[...],
                          preferred_element_type=jnp.float32)


def _merge(x, h, proj, ys, wmerge_all, wbranch_all, wout_all, layer, tm=512, tn=256):
    s, d = x.shape
    nc = d // tn
    row = lambda w: pl.BlockSpec((tm, w), lambda i, c: (i, 0))
    wm = lambda n: pl.BlockSpec((None, d, tn), lambda i, c: (layer, 0, n * nc + c))
    return pl.pallas_call(
        _merge_kernel,
        grid=(s // tm, nc),
        in_specs=[row(d), row(d),
                  pl.BlockSpec((tm, N_BRANCH * BRANCH_W), lambda i, c: (i, OFF_SILU)),
                  row(BRANCH_W), row(BRANCH_W), row(BRANCH_W), row(BRANCH_W),
                  wm(0), wm(1), wm(2), wm(3),
                  pl.BlockSpec((None, N_BRANCH, BRANCH_W, tn), lambda i, c: (layer, 0, 0, c)),
                  pl.BlockSpec((None, tn, d), lambda i, c: (layer, c, 0))],
        out_specs=row(d),
        out_shape=jax.ShapeDtypeStruct((s, d), jnp.float32),
        scratch_shapes=[pltpu.VMEM((N_BRANCH, tm, BRANCH_W), jnp.bfloat16)],
        compiler_params=_cparams(("parallel", "arbitrary")),
        name="merge",
    )(x, h, proj, *ys, wmerge_all, wmerge_all, wmerge_all, wmerge_all, wbranch_all, wout_all)


def _rope_cos_sin(pos, dim):
    inv = ROPE_THETA ** (-jnp.arange(0, dim, 2, dtype=jnp.float32) / dim)
    ang = pos.astype(jnp.float32)[:, None] * inv[None, :]
    return jnp.cos(ang), jnp.sin(ang)


def _tables_a(s):
    cos, sin = _rope_cos_sin(jnp.arange(s, dtype=jnp.int32), A_ROPE)
    z16, z32 = jnp.zeros((s, 16), jnp.float32), jnp.zeros((s, 32), jnp.float32)
    one = jnp.ones((s, A_NOPE), jnp.float32)
    zero = jnp.zeros((s, A_NOPE), jnp.float32)
    c = jnp.concatenate([one, cos, cos, z32], axis=1)
    sa = jnp.concatenate([zero, -sin, z16, z32], axis=1)
    sb = jnp.concatenate([zero, z16, sin, z32], axis=1)
    return c, sa, sb


def _tables_b(s):
    rows = s // GRID_W
    row_pos = jnp.repeat(jnp.arange(rows, dtype=jnp.int32), GRID_W)
    col_pos = jnp.tile(jnp.arange(GRID_W, dtype=jnp.int32), rows)
    cr, sr = _rope_cos_sin(row_pos, HEAD_DIM // 2)
    cc, sc = _rope_cos_sin(col_pos, HEAD_DIM // 2)
    z = jnp.zeros_like(sr)
    c = jnp.concatenate([cr, cr, cc, cc] * 2, axis=1)
    sa = jnp.concatenate([-sr, z, -sc, z] * 2, axis=1)
    sb = jnp.concatenate([z, sr, z, sc] * 2, axis=1)
    return c, sa, sb


def _rel_bucket(rel):
    nb = NUM_BUCKETS // 2
    max_exact = nb // 2
    side = jnp.where(rel > 0, nb, 0)
    n = jnp.abs(rel)
    nf = jnp.maximum(n, 1).astype(jnp.float32)
    large = max_exact + (jnp.log(nf / max_exact) / math.log(T5_MAX_DISTANCE / max_exact)
                         * (nb - max_exact)).astype(jnp.int32)
    large = jnp.minimum(large, nb - 1)
    return side + jnp.where(n < max_exact, n, large)


def _bands_c(rel_bias):
    bands = []
    for g, (window, dil) in enumerate(C_CONFIGS):
        rel = (jnp.arange(C_WIN[g], dtype=jnp.int32)[None, :]
               - jnp.arange(C_QT, dtype=jnp.int32)[:, None] - C_PAD[g])
        ok = jnp.logical_and(jnp.abs(rel) <= window // 2, rel % dil == 0)
        tab = rel_bias[:, g * C_HEADS:(g + 1) * C_HEADS][_rel_bucket(rel)]
        bands.append(jnp.where(ok[None], jnp.transpose(tab, (2, 0, 1)), NEG).astype(jnp.float32))
    return bands


def _band_d(rel_bias):
    rel = (jnp.arange(D_TQ + 2 * D_BAND, dtype=jnp.int32)[None, :]
           - jnp.arange(D_TQ, dtype=jnp.int32)[:, None] - D_BAND)
    tab = rel_bias[:, C_GROUPS * C_HEADS:][_rel_bucket(rel)]
    return jnp.transpose(tab, (2, 0, 1)).astype(jnp.float32)


def _pad_lanes(v, width=LANES):
    return jnp.pad(v, ((0, width - v.shape[0]),))[None, :]


def kernel(x, norm_w, w_in, mla_q_norm, mla_kv_norm, mla_w_uq, mla_w_ukv, mla_qk_norm,
           gqa_qk_norm, dil_qk_norm, diff_qk_norm, diff_lambda, diff_subnorm, rel_bias,
           w_branch, w_out):
    b, s, d = x.shape
    assert b == 1 and d == D_MODEL
    depth = norm_w.shape[0]
    bf = jnp.bfloat16

    w_main = jnp.concatenate(
        [w_in[:, :, SRC_SILU:SRC_MERGE], w_in[:, :, SRC_C:SRC_D], w_in[:, :, SRC_D:SRC_SILU],
         w_in[:, :, SRC_B:SRC_C], w_in[:, :, SRC_A:SRC_B],
         jnp.zeros((depth, d, A_BLOCK - SRC_B), w_in.dtype)], axis=2).astype(bf)
    w_merge = w_in[:, :, SRC_MERGE:].astype(bf)
    w_branch_b = w_branch.astype(bf)
    w_out_b = w_out.astype(bf)
    wuq = mla_w_uq.reshape(depth, Q_LORA, A_HEADS, A_QK)
    wuq = jnp.pad(wuq, ((0, 0), (0, 0), (0, 0), (0, LANES - A_QK)))
    wuq = wuq.reshape(depth, Q_LORA, A_HEADS * LANES).astype(bf)
    wukv = mla_w_ukv.astype(bf)

    tabs_a = _tables_a(s)
    tabs_b = _tables_b(s)
    bands_c = _bands_c(rel_bias)
    band_d = _band_d(rel_bias)

    xs = x[0]
    for l in range(depth):
        h = _rmsnorm(xs, norm_w[l][None, :])
        proj = _in_proj(h, w_main, l)

        qa, kta, va = _prep_a(proj, mla_q_norm[l][None, :], mla_kv_norm[l][None, :],
                              wuq[l], wukv[l], _pad_lanes(mla_qk_norm[l, 0]),
                              _pad_lanes(mla_qk_norm[l, 1]), tabs_a)
        y_a = _attn_a(qa, kta, va)

        g2 = lambda v: jnp.tile(v, 2)[None, :]
        qb, ktb, vb = _prep_b(proj, g2(gqa_qk_norm[l, 0]), g2(gqa_qk_norm[l, 1]), tabs_b)
        y_b = _attn_b(qb, ktb, vb)

        qs, kts, vs = [], [], []
        for g in range(C_GROUPS):
            qg, ktg, vg = _prep_c(proj, g2(dil_qk_norm[l, 0, g]), g2(dil_qk_norm[l, 1, g]), g)
            qs.append(qg); kts.append(ktg); vs.append(vg)
        y_c = _attn_c(qs, kts, vs, bands_c, s)

        qd, ktd, vd = _prep_d(proj, g2(diff_qk_norm[l, 0]), g2(diff_qk_norm[l, 1]))
        lambda_init = 0.8 - 0.6 * math.exp(-0.3 * l)
        y_d = _attn_d(qd, ktd, vd, band_d, diff_lambda[l], diff_subnorm[l][None, :], lambda_init)

        xs = _merge(xs, h, proj, (y_a, y_b, y_c, y_d), w_merge, w_branch_b, w_out_b, l)
    return xs[None]
```

```python
import functools
import math

import numpy as np
import jax
import jax.numpy as jnp
from jax import lax
from jax.experimental import pallas as pl
from jax.experimental.pallas import tpu as pltpu

D_MODEL = 2048
GRID_W = 64
HEAD_DIM = 64
BRANCH_W = 512
N_BRANCH = 4
ROPE_THETA = 10000.0
EPS = 1e-6
NEG = -1e30

A_HEADS = 8
A_NOPE = 64
A_ROPE = 32
A_V = 64
A_QK = A_NOPE + A_ROPE
Q_LORA = 384
KV_LORA = 128

B_HEADS = 8
B_KV_HEADS = 2

C_HEADS = 8
C_CONFIGS = ((128, 1), (512, 4), (2048, 16))
C_GROUPS = len(C_CONFIGS)

D_HEADS = 4
D_V = 2 * HEAD_DIM

NUM_BUCKETS = 32
T5_MAX_DISTANCE = 1024

LANES = 128

OFF_SILU = 0
OFF_CQ, OFF_CK, OFF_CV = 2048, 3584, 5120
OFF_DQ, OFF_DK, OFF_DV = 6656, 7168, 7680
OFF_BQ, OFF_BKV = 8192, 8704
OFF_A = 8960
A_BLOCK = 640
N_MAIN = 9600

SRC_A, SRC_B, SRC_C, SRC_D, SRC_SILU, SRC_MERGE = 0, 544, 1312, 5920, 7456, 9504

VMEM_LIMIT = 56 * 1024 * 1024

LOG2E = math.log2(math.e)

C_QT = 128
C_HALF = 64
C_DIL_MAX = max(d for _, d in C_CONFIGS)
C_BAND_W = 5 * C_QT

D_TQ = 512
D_TK = 1024
D_BAND = 1664


def _bucket_thresholds():
    nb = NUM_BUCKETS // 2
    max_exact = nb // 2
    n = np.arange(max_exact, 4 * T5_MAX_DISTANCE, dtype=np.float32)
    large = max_exact + (np.log(n / np.float32(max_exact))
                         / np.float32(math.log(T5_MAX_DISTANCE / max_exact))
                         * np.float32(nb - max_exact)).astype(np.int32)
    large = np.minimum(large, nb - 1)
    return tuple(int(n[np.argmax(large >= max_exact + k)]) for k in range(1, nb - max_exact))


BUCKET_STEPS = _bucket_thresholds()
assert D_BAND - D_TK + 1 >= BUCKET_STEPS[-1]


def _cparams(sem):
    return pltpu.CompilerParams(dimension_semantics=sem, vmem_limit_bytes=VMEM_LIMIT)


def _rmsnorm_kernel(x_ref, w_ref, o_ref):
    x = x_ref[...]
    ms = jnp.mean(x * x, axis=-1, keepdims=True)
    o_ref[...] = (x * lax.rsqrt(ms + EPS) * w_ref[...]).astype(o_ref.dtype)


def _rmsnorm(x, w, tm=512):
    s, d = x.shape
    return pl.pallas_call(
        _rmsnorm_kernel,
        grid=(s // tm,),
        in_specs=[pl.BlockSpec((tm, d), lambda i: (i, 0)),
                  pl.BlockSpec((1, d), lambda i: (0, 0))],
        out_specs=pl.BlockSpec((tm, d), lambda i: (i, 0)),
        out_shape=jax.ShapeDtypeStruct((s, d), jnp.bfloat16),
        compiler_params=_cparams(("parallel",)),
        name="rmsnorm",
    )(x, w)


def _matmul_kernel(a_ref, b_ref, o_ref):
    o_ref[...] = jnp.dot(a_ref[...], b_ref[...], preferred_element_type=jnp.float32)


def _in_proj(h, w_all, layer, tm=1024, tn=1920):
    s, d = h.shape
    n = w_all.shape[-1]
    return pl.pallas_call(
        _matmul_kernel,
        grid=(s // tm, n // tn),
        in_specs=[pl.BlockSpec((tm, d), lambda i, j: (i, 0)),
                  pl.BlockSpec((None, d, tn), lambda i, j: (layer, 0, j))],
        out_specs=pl.BlockSpec((tm, tn), lambda i, j: (i, j)),
        out_shape=jax.ShapeDtypeStruct((s, n), jnp.float32),
        compiler_params=_cparams(("parallel", "arbitrary")),
        name="in_proj",
    )(h, w_all)


def _band_kernel(tab_ref, o_ref, *, pad, dil, half, col0):
    h = pl.program_id(0)
    rows, width = o_ref.shape
    col = lax.broadcasted_iota(jnp.int32, (rows, width), 1) + pl.program_id(1) * width
    row = lax.broadcasted_iota(jnp.int32, (rows, width), 0)
    rel_u = col - row - pad
    rel = rel_u * dil
    n = jnp.abs(rel)
    nb = NUM_BUCKETS // 2
    max_exact = nb // 2
    large = jnp.full((rows, width), max_exact, jnp.int32)
    for t in BUCKET_STEPS:
        large = large + jnp.where(n >= t, 1, 0)
    bucket = jnp.where(rel > 0, nb, 0) + jnp.where(n < max_exact, n, large)
    val = jnp.zeros((rows, width), jnp.float32)
    for b in range(NUM_BUCKETS):
        val = jnp.where(bucket == b, tab_ref[b, col0 + h], val)
    val = val * LOG2E
    if half is not None:
        val = jnp.where(jnp.abs(rel_u) <= half, val, NEG)
    o_ref[...] = val


def _band(rel_bias, heads, rows, width, wblk, *, pad, dil, half, col0, name):
    return pl.pallas_call(
        functools.partial(_band_kernel, pad=pad, dil=dil, half=half, col0=col0),
        grid=(heads, width // wblk),
        in_specs=[pl.BlockSpec(memory_space=pltpu.SMEM)],
        out_specs=pl.BlockSpec((None, rows, wblk), lambda h, j: (h, 0, j)),
        out_shape=jax.ShapeDtypeStruct((heads, rows, width), jnp.float32),
        compiler_params=_cparams(("parallel", "arbitrary")),
        name=name,
    )(rel_bias)


def _lane_lo(shape):
    return (lax.broadcasted_iota(jnp.int32, shape, len(shape) - 1) % LANES) < HEAD_DIM


def _rope(x, c, sa, sb):
    return x * c + pltpu.roll(x, LANES - 16, 1) * sa + pltpu.roll(x, 16, 1) * sb


def _halfnorm(x, gain, lo):
    sq = x * x
    s_lo = jnp.sum(jnp.where(lo, sq, 0.0), axis=-1, keepdims=True)
    s_hi = jnp.sum(jnp.where(lo, 0.0, sq), axis=-1, keepdims=True)
    r = jnp.where(lo, lax.rsqrt(s_lo * (1.0 / HEAD_DIM) + EPS),
                  lax.rsqrt(s_hi * (1.0 / HEAD_DIM) + EPS))
    return x * r * gain


def _prep_a_kernel(p_ref, qn_ref, kvn_ref, wuq_ref, wukv_ref, gq_ref, gk_ref,
                   c_ref, sa_ref, sb_ref, q_out, kt_out, v_out):
    p = p_ref[...]
    cq = p[:, :Q_LORA]
    cq = cq * lax.rsqrt(jnp.mean(cq * cq, axis=-1, keepdims=True) + EPS) * qn_ref[...]
    q = jnp.dot(cq.astype(jnp.bfloat16), wuq_ref[...], preferred_element_type=jnp.float32)
    ckv = p[:, Q_LORA:Q_LORA + KV_LORA]
    ckv = ckv * lax.rsqrt(jnp.mean(ckv * ckv, axis=-1, keepdims=True) + EPS) * kvn_ref[...]
    kvu = jnp.dot(ckv.astype(jnp.bfloat16), wukv_ref[...], preferred_element_type=jnp.float32)
    v_out[...] = jnp.where(_lane_lo(kvu.shape), 1.0, kvu).astype(v_out.dtype)
    kr = pltpu.roll(p[:, Q_LORA + KV_LORA:], HEAD_DIM, 1)
    lo = _lane_lo(kr.shape)
    c, sa, sb = c_ref[...], sa_ref[...], sb_ref[...]
    scale = A_QK ** -0.5 * LOG2E
    for h in range(A_HEADS):
        qh = q[:, h * LANES:(h + 1) * LANES]
        ss = jnp.sum(qh * qh, axis=-1, keepdims=True) * (1.0 / A_QK)
        qh = qh * lax.rsqrt(ss + EPS) * gq_ref[...]
        q_out[h] = (_rope(qh, c, sa, sb) * scale).astype(q_out.dtype)
        kh = jnp.where(lo, kvu[:, h * LANES:(h + 1) * LANES], kr)
        ss = jnp.sum(kh * kh, axis=-1, keepdims=True) * (1.0 / A_QK)
        kh = kh * lax.rsqrt(ss + EPS) * gk_ref[...]
        kt_out[h] = _rope(kh, c, sa, sb).T.astype(kt_out.dtype)


def _prep_a(proj, qn, kvn, wuq, wukv, gq, gk, tabs, tm=256):
    s = proj.shape[0]
    row = lambda w: pl.BlockSpec((1, w), lambda i: (0, 0))
    tab = pl.BlockSpec((tm, LANES), lambda i: (i, 0))
    return pl.pallas_call(
        _prep_a_kernel,
        grid=(s // tm,),
        in_specs=[pl.BlockSpec((tm, A_BLOCK), lambda i: (i, OFF_A // A_BLOCK)),
                  row(Q_LORA), row(KV_LORA),
                  pl.BlockSpec(wuq.shape, lambda i: (0, 0)),
                  pl.BlockSpec(wukv.shape, lambda i: (0, 0)),
                  row(LANES), row(LANES), tab, tab, tab],
        out_specs=[pl.BlockSpec((A_HEADS, tm, LANES), lambda i: (0, i, 0)),
                   pl.BlockSpec((A_HEADS, LANES, tm), lambda i: (0, 0, i)),
                   pl.BlockSpec((tm, A_HEADS * LANES), lambda i: (i, 0))],
        out_shape=[jax.ShapeDtypeStruct((A_HEADS, s, LANES), jnp.bfloat16),
                   jax.ShapeDtypeStruct((A_HEADS, LANES, s), jnp.bfloat16),
                   jax.ShapeDtypeStruct((s, A_HEADS * LANES), jnp.bfloat16)],
        compiler_params=_cparams(("parallel",)),
        name="prep_a",
    )(proj, qn, kvn, wuq, wukv, gq, gk, *tabs)


def _prep_b_kernel(q_ref, kv_ref, gq_ref, gk_ref, c_ref, sa_ref, sb_ref,
                   q_out, kt_out, v_out):
    c, sa, sb = c_ref[...], sa_ref[...], sb_ref[...]
    lo = _lane_lo(c.shape)
    scale = HEAD_DIM ** -0.5 * LOG2E
    q = q_ref[...]
    for b in range(B_HEADS // 2):
        x = _rope(_halfnorm(q[:, b * LANES:(b + 1) * LANES], gq_ref[...], lo), c, sa, sb) * scale
        xr = pltpu.roll(x, HEAD_DIM, 1)
        g = b // 2
        even, odd = (x, xr) if g == 0 else (xr, x)
        keep = lo if g == 0 else jnp.logical_not(lo)
        q_out[g, 2 * (b % 2)] = jnp.where(keep, even, 0.0).astype(q_out.dtype)
        q_out[g, 2 * (b % 2) + 1] = jnp.where(keep, odd, 0.0).astype(q_out.dtype)
    kv = kv_ref[...]
    k = _rope(_halfnorm(kv[:, :LANES], gk_ref[...], lo), c, sa, sb)
    kt_out[...] = k.T.astype(kt_out.dtype)
    v = kv[:, LANES:]
    v_out[0] = jnp.where(lo, v, 1.0).astype(v_out.dtype)
    v_out[1] = jnp.where(lo, pltpu.roll(v, HEAD_DIM, 1), 1.0).astype(v_out.dtype)


def _prep_b(proj, gq, gk, tabs, tm=256):
    s = proj.shape[0]
    row = pl.BlockSpec((1, LANES), lambda i: (0, 0))
    tab = pl.BlockSpec((tm, LANES), lambda i: (i, 0))
    return pl.pallas_call(
        _prep_b_kernel,
        grid=(s // tm,),
        in_specs=[pl.BlockSpec((tm, 512), lambda i: (i, OFF_BQ // 512)),
                  pl.BlockSpec((tm, 256), lambda i: (i, OFF_BKV // 256)),
                  row, row, tab, tab, tab],
        out_specs=[pl.BlockSpec((2, 4, tm, LANES), lambda i: (0, 0, i, 0)),
                   pl.BlockSpec((LANES, tm), lambda i: (0, i)),
                   pl.BlockSpec((2, tm, LANES), lambda i: (0, i, 0))],
        out_shape=[jax.ShapeDtypeStruct((2, 4, s, LANES), jnp.bfloat16),
                   jax.ShapeDtypeStruct((LANES, s), jnp.bfloat16),
                   jax.ShapeDtypeStruct((2, s, LANES), jnp.bfloat16)],
        compiler_params=_cparams(("parallel",)),
        name="prep_b",
    )(proj, proj, gq, gk, *tabs)


def _prep_d_kernel(q_ref, k_ref, v_ref, gq_ref, gk_ref, q_out, kt_out, v_out):
    q, k = q_ref[...], k_ref[...]
    lo = _lane_lo((q.shape[0], LANES))
    scale = HEAD_DIM ** -0.5 * LOG2E
    for h in range(D_HEADS):
        x = _halfnorm(q[:, h * LANES:(h + 1) * LANES], gq_ref[...], lo) * scale
        q_out[h, 0] = jnp.where(lo, x, 0.0).astype(q_out.dtype)
        q_out[h, 1] = jnp.where(lo, 0.0, x).astype(q_out.dtype)
        y = _halfnorm(k[:, h * LANES:(h + 1) * LANES], gk_ref[...], lo)
        kt_out[h] = y.T.astype(kt_out.dtype)
    v_out[...] = v_ref[...].astype(v_out.dtype)


def _prep_d(proj, gq, gk, tm=256):
    s = proj.shape[0]
    row = pl.BlockSpec((1, LANES), lambda i: (0, 0))
    blk = lambda off: pl.BlockSpec((tm, 512), lambda i: (i, off // 512))
    return pl.pallas_call(
        _prep_d_kernel,
        grid=(s // tm,),
        in_specs=[blk(OFF_DQ), blk(OFF_DK), blk(OFF_DV), row, row],
        out_specs=[pl.BlockSpec((D_HEADS, 2, tm, LANES), lambda i: (0, 0, i, 0)),
                   pl.BlockSpec((D_HEADS, LANES, tm), lambda i: (0, 0, i)),
                   pl.BlockSpec((tm, 512), lambda i: (i, 0))],
        out_shape=[jax.ShapeDtypeStruct((D_HEADS, 2, s, LANES), jnp.bfloat16),
                   jax.ShapeDtypeStruct((D_HEADS, LANES, s), jnp.bfloat16),
                   jax.ShapeDtypeStruct((s, 512), jnp.bfloat16)],
        compiler_params=_cparams(("parallel",)),
        name="prep_d",
    )(proj, proj, proj, gq, gk)


def _prep_c_kernel(q_ref, k_ref, v_ref, gq_ref, gk_ref, q_out, kt_out, v_out, *, dil, ut):
    lo = _lane_lo((ut, LANES))
    scale = HEAD_DIM ** -0.5 * LOG2E
    for r in range(dil):
        rows = pl.ds(r, ut, stride=dil) if dil > 1 else pl.ds(0, ut)
        q_out[r] = (_halfnorm(q_ref[rows, :], gq_ref[...], lo) * scale).astype(q_out.dtype)
        kt_out[r] = _halfnorm(k_ref[rows, :], gk_ref[...], lo).T.astype(kt_out.dtype)
        v_out[r] = v_ref[rows, :].astype(v_out.dtype)


def _prep_c(proj, gq, gk, g, ut):
    s = proj.shape[0]
    dil = C_CONFIGS[g][1]
    seg = s // dil
    nb = C_HEADS // 2
    tn = ut * dil
    blk = lambda off: pl.BlockSpec((tn, LANES), lambda i, b: (i, off // LANES + g * nb + b))
    row = pl.BlockSpec((1, LANES), lambda i, b: (0, 0))
    return pl.pallas_call(
        functools.partial(_prep_c_kernel, dil=dil, ut=ut),
        grid=(s // tn, nb),
        in_specs=[blk(OFF_CQ), blk(OFF_CK), blk(OFF_CV), row, row],
        out_specs=[pl.BlockSpec((dil, ut, LANES), lambda i, b: (0, i, b)),
                   pl.BlockSpec((dil, LANES, ut), lambda i, b: (0, b, i)),
                   pl.BlockSpec((dil, ut, LANES), lambda i, b: (0, i, b))],
        out_shape=[jax.ShapeDtypeStruct((dil, seg, nb * LANES), jnp.bfloat16),
                   jax.ShapeDtypeStruct((dil, nb * LANES, seg), jnp.bfloat16),
                   jax.ShapeDtypeStruct((dil, seg, nb * LANES), jnp.bfloat16)],
        compiler_params=_cparams(("parallel", "parallel")),
        name=f"prep_c{g}",
    )(proj, proj, proj, gq, gk)


def _online_update(s, v, m_ref, acc_ref, r0, rows, l_ref=None):
    tk = s.shape[1]
    sl = pl.ds(r0, rows)
    m_prev = m_ref[sl, :]
    m_cur = jnp.max(s, axis=1, keepdims=True)
    m_next = jnp.maximum(m_prev, m_cur)
    p = jnp.exp2(s - pltpu.repeat(m_next, tk // LANES, 1))
    alpha = jnp.exp2(m_prev - m_next)
    if l_ref is not None:
        l_ref[sl, :] = alpha * l_ref[sl, :] + jnp.sum(p, axis=1, keepdims=True)
    m_ref[sl, :] = m_next
    pv = jnp.dot(p.astype(jnp.bfloat16), v, preferred_element_type=jnp.float32)
    acc_ref[sl, :] = alpha * acc_ref[sl, :] + pv


def _init_stats(m_ref, acc_ref, l_ref=None):
    m_ref[...] = jnp.full_like(m_ref, -jnp.inf)
    acc_ref[...] = jnp.zeros_like(acc_ref)
    if l_ref is not None:
        l_ref[...] = jnp.zeros_like(l_ref)


def _attn_a_kernel(q_ref, kt_ref, v_ref, o_ref, m_ref, acc_ref, *, tq, tk, nk):
    _init_stats(m_ref, acc_ref)

    def body(c, carry):
        k0 = pl.multiple_of(c * tk, tk)
        for hh in range(2):
            s = jnp.dot(q_ref[hh], kt_ref[hh, :, pl.ds(k0, tk)],
                        preferred_element_type=jnp.float32)
            v = v_ref[pl.ds(k0, tk), hh * LANES:(hh + 1) * LANES]
            _online_update(s, v, m_ref, acc_ref, hh * tq, tq)
        return carry

    lax.fori_loop(0, nk, body, 0)
    acc = acc_ref[...]
    o = acc / pltpu.roll(acc, HEAD_DIM, 1)
    o_ref[...] = jnp.where(_lane_lo((tq, LANES)), pltpu.roll(o[:tq], HEAD_DIM, 1), o[tq:])


def _attn_a(q, kt, v, tq=512, tk=1024):
    s = q.shape[1]
    tk = min(tk, s)
    return pl.pallas_call(
        functools.partial(_attn_a_kernel, tq=tq, tk=tk, nk=s // tk),
        grid=(A_HEADS // 2, s // tq),
        in_specs=[pl.BlockSpec((2, tq, LANES), lambda p, i: (p, i, 0)),
                  pl.BlockSpec((2, LANES, s), lambda p, i: (p, 0, 0)),
                  pl.BlockSpec((s, 2 * LANES), lambda p, i: (0, p))],
        out_specs=pl.BlockSpec((tq, LANES), lambda p, i: (i, p)),
        out_shape=jax.ShapeDtypeStruct((s, BRANCH_W), jnp.float32),
        scratch_shapes=[pltpu.VMEM((2 * tq, LANES), jnp.float32)] * 2,
        compiler_params=_cparams(("parallel", "arbitrary")),
        name="attn_a",
    )(q, kt, v)


def _attn_b_kernel(q_ref, kt_ref, v_ref, o_ref, m_ref, acc_ref, *, tq, tk, nk):
    _init_stats(m_ref, acc_ref)
    q = q_ref[...].reshape(4 * tq, LANES)

    def body(c, carry):
        k0 = pl.multiple_of(c * tk, tk)
        s = jnp.dot(q, kt_ref[:, pl.ds(k0, tk)], preferred_element_type=jnp.float32)
        _online_update(s, v_ref[pl.ds(k0, tk), :], m_ref, acc_ref, 0, 4 * tq)
        return carry

    lax.fori_loop(0, nk, body, 0)
    acc = acc_ref[...]
    o = acc / pltpu.roll(acc, HEAD_DIM, 1)
    lo = _lane_lo((tq, LANES))
    o_ref[:, :LANES] = jnp.where(lo, o[:tq], pltpu.roll(o[tq:2 * tq], HEAD_DIM, 1))
    o_ref[:, LANES:] = jnp.where(lo, o[2 * tq:3 * tq], pltpu.roll(o[3 * tq:], HEAD_DIM, 1))


def _attn_b(q, kt, v, tq=256, tk=1024):
    s = q.shape[2]
    tk = min(tk, s)
    return pl.pallas_call(
        functools.partial(_attn_b_kernel, tq=tq, tk=tk, nk=s // tk),
        grid=(2, s // tq),
        in_specs=[pl.BlockSpec((None, 4, tq, LANES), lambda g, i: (g, 0, i, 0)),
                  pl.BlockSpec((LANES, s), lambda g, i: (0, 0)),
                  pl.BlockSpec((None, s, LANES), lambda g, i: (g, 0, 0))],
        out_specs=pl.BlockSpec((tq, 2 * LANES), lambda g, i: (i, g)),
        out_shape=jax.ShapeDtypeStruct((s, BRANCH_W), jnp.float32),
        scratch_shapes=[pltpu.VMEM((4 * tq, LANES), jnp.float32)] * 2,
        compiler_params=_cparams(("parallel", "arbitrary")),
        name="attn_b",
    )(q, kt, v)


def _attn_d_kernel(q_ref, kt_ref, v_ref, band_ref, lam_ref, sub_ref, o_ref,
                   m_ref, acc_ref, l_ref, *, tq, tk, nk, lambda_init):
    _init_stats(m_ref, acc_ref, l_ref)
    q = q_ref[...].reshape(2 * tq, LANES)
    q0 = pl.program_id(1) * tq
    hi = tq + 2 * D_BAND - tk

    def body(c, carry):
        k0 = pl.multiple_of(c * tk, tk)
        s = jnp.dot(q, kt_ref[:, pl.ds(k0, tk)], preferred_element_type=jnp.float32)
        b0 = pl.multiple_of(jnp.clip(k0 - q0 + D_BAND, 0, hi), LANES)
        bias = band_ref[:, pl.ds(b0, tk)]
        s = (s.reshape(2, tq, tk) + bias[None]).reshape(2 * tq, tk)
        _online_update(s, v_ref[pl.ds(k0, tk), :], m_ref, acc_ref, 0, 2 * tq, l_ref)
        return carry

    lax.fori_loop(0, nk, body, 0)
    lv = lam_ref[...]
    lam = (jnp.exp(jnp.sum(lv[0:1] * lv[1:2], axis=-1, keepdims=True))
           - jnp.exp(jnp.sum(lv[2:3] * lv[3:4], axis=-1, keepdims=True)) + lambda_init)
    o = acc_ref[...] / l_ref[...]
    o = o[:tq] - lam * o[tq:]
    o = o * lax.rsqrt(jnp.mean(o * o, axis=-1, keepdims=True) + EPS) * sub_ref[...]
    o_ref[...] = o * (1.0 - lambda_init)


def _attn_d(q, kt, v, band, lam_vecs, subw, lambda_init, tq=D_TQ, tk=D_TK):
    s = q.shape[2]
    return pl.pallas_call(
        functools.partial(_attn_d_kernel, tq=tq, tk=tk, nk=s // tk, lambda_init=lambda_init),
        grid=(D_HEADS, s // tq),
        in_specs=[pl.BlockSpec((None, 2, tq, LANES), lambda h, i: (h, 0, i, 0)),
                  pl.BlockSpec((None, LANES, s), lambda h, i: (h, 0, 0)),
                  pl.BlockSpec((s, LANES), lambda h, i: (0, h)),
                  pl.BlockSpec((None, tq, tq + 2 * D_BAND), lambda h, i: (h, 0, 0)),
                  pl.BlockSpec((4, HEAD_DIM), lambda h, i: (0, 0)),
                  pl.BlockSpec((1, LANES), lambda h, i: (0, 0))],
        out_specs=pl.BlockSpec((tq, LANES), lambda h, i: (i, h)),
        out_shape=jax.ShapeDtypeStruct((s, BRANCH_W), jnp.float32),
        scratch_shapes=[pltpu.VMEM((2 * tq, LANES), jnp.float32)] * 3,
        compiler_params=_cparams(("parallel", "arbitrary")),
        name="attn_d",
    )(q, kt, v, band, lam_vecs, subw)


def _attn_c_kernel(q_ref, kt_ref, v_ref, band_ref, o_ref, lse_ref, *, nt, seg, win):
    j = pl.program_id(2)
    lo = _lane_lo((C_QT, LANES))

    def body(t, carry):
        r0 = pl.multiple_of(t * C_QT, C_QT)
        u0 = j * (nt * C_QT) + r0
        ws = pl.multiple_of(jnp.clip(u0 - C_QT, 0, seg - win), C_QT)
        x = pl.multiple_of(ws - u0 + 2 * C_QT, C_QT)
        q = q_ref[pl.ds(r0, C_QT), :]
        zero = jnp.zeros_like(q)
        q2 = jnp.concatenate([jnp.where(lo, q, zero), jnp.where(lo, zero, q)], axis=0)
        s = jnp.dot(q2, kt_ref[:, pl.ds(ws, win)], preferred_element_type=jnp.float32)
        s = s + jnp.concatenate([band_ref[0, :, pl.ds(x, win)], band_ref[1, :, pl.ds(x, win)]], axis=0)
        m = jnp.max(s, axis=1, keepdims=True)
        p = jnp.exp2(s - m)
        l = jnp.sum(p, axis=1, keepdims=True)
        pv = jnp.dot(p.astype(jnp.bfloat16), v_ref[pl.ds(ws, win), :],
                     preferred_element_type=jnp.float32)
        o = pv / l
        lse = m + jnp.log2(l)
        o_ref[pl.ds(r0, C_QT), :] = jnp.where(lo, o[:C_QT], o[C_QT:])
        lse_ref[pl.ds(r0, C_QT), :] = jnp.where(lo, lse[:C_QT], lse[C_QT:])
        return carry

    lax.fori_loop(0, nt, body, 0)


def _attn_c(q, kt, v, band, nt_max=8):
    dil, seg, w = q.shape
    nt = min(nt_max, seg // C_QT)
    win = min(3 * C_QT, seg)
    tqb = nt * C_QT
    out = jax.ShapeDtypeStruct((dil, seg, w), jnp.float32)
    ospec = pl.BlockSpec((None, tqb, LANES), lambda r, b, j: (r, j, b))
    return pl.pallas_call(
        functools.partial(_attn_c_kernel, nt=nt, seg=seg, win=win),
        grid=(dil, C_HEADS // 2, seg // tqb),
        in_specs=[pl.BlockSpec((None, tqb, LANES), lambda r, b, j: (r, j, b)),
                  pl.BlockSpec((None, LANES, seg), lambda r, b, j: (r, b, 0)),
                  pl.BlockSpec((None, seg, LANES), lambda r, b, j: (r, 0, b)),
                  pl.BlockSpec((2, C_QT, C_BAND_W), lambda r, b, j: (b, 0, 0))],
        out_specs=[ospec, ospec],
        out_shape=[out, out],
        compiler_params=_cparams(("parallel", "parallel", "arbitrary")),
        name=f"attn_c_d{dil}",
    )(q, kt, v, band)


def _combine_c_kernel(*refs):
    o_refs, l_refs, out_ref = refs[0:2 * C_GROUPS:2], refs[1:2 * C_GROUPS:2], refs[-1]
    n = out_ref.shape[0] // C_DIL_MAX
    for r in range(C_DIL_MAX):
        outs, lses = [], []
        for (_, dil), o_ref, l_ref in zip(C_CONFIGS, o_refs, l_refs):
            rows = pl.ds(r // dil, n, stride=C_DIL_MAX // dil)
            cls = pl.ds(r % dil, 1)
            outs.append(o_ref[cls, rows, :][0])
            lses.append(l_ref[cls, rows, :][0])
        m = functools.reduce(jnp.maximum, lses)
        es = [jnp.exp2(l - m) for l in lses]
        num = sum(e * o for e, o in zip(es, outs))
        out_ref[pl.ds(r, n, stride=C_DIL_MAX), :] = num / sum(es)


def _combine_c(os, lses, s, tn=2048):
    nb = C_HEADS // 2
    args, in_specs = [], []
    for (_, dil), o, l in zip(C_CONFIGS, os, lses):
        spec = pl.BlockSpec((dil, tn // dil, LANES), lambda i, b: (0, i, b))
        args += [o, l]
        in_specs += [spec, spec]
    return pl.pallas_call(
        _combine_c_kernel,
        grid=(s // tn, nb),
        in_specs=in_specs,
        out_specs=pl.BlockSpec((tn, LANES), lambda i, b: (i, b)),
        out_shape=jax.ShapeDtypeStruct((s, nb * LANES), jnp.float32),
        compiler_params=_cparams(("parallel", "parallel")),
        name="combine_c",
    )(*args)


def _merge_kernel(x_ref, h_ref, sg_ref, ya_ref, yb_ref, yc_ref, yd_ref,
                  wm0_ref, wm1_ref, wm2_ref, wm3_ref, wb_ref, wo_ref, o_ref, yg_ref):
    c = pl.program_id(1)

    @pl.when(c == 0)
    def _():
        sg = sg_ref[...]
        for n, y_ref in enumerate((ya_ref, yb_ref, yc_ref, yd_ref)):
            g = sg[:, n * BRANCH_W:(n + 1) * BRANCH_W]
            yg_ref[n] = (y_ref[...] * (g * jax.nn.sigmoid(g))).astype(yg_ref.dtype)
        o_ref[...] = x_ref[...]

    h = h_ref[...]
    mixed = 0.0
    for n, wm_ref in enumerate((wm0_ref, wm1_ref, wm2_ref, wm3_ref)):
        gate = jax.nn.sigmoid(jnp.dot(h, wm_ref[...], preferred_element_type=jnp.float32))
        z = jnp.dot(yg_ref[n], wb_ref[n], preferred_element_type=jnp.float32)
        mixed = mixed + gate * z
    o_ref[...] += jnp.dot(mixed.astype(jnp.bfloat16), wo_ref[...],
                          preferred_element_type=jnp.float32)


def _merge(x, h, proj, ys, wmerge_all, wbranch_all, wout_all, layer, tm=512, tn=256):
    s, d = x.shape
    nc = d // tn
    row = lambda w: pl.BlockSpec((tm, w), lambda i, c: (i, 0))
    wm = lambda n: pl.BlockSpec((None, d, tn), lambda i, c: (layer, 0, n * nc + c))
    return pl.pallas_call(
        _merge_kernel,
        grid=(s // tm, nc),
        in_specs=[row(d), row(d),
                  pl.BlockSpec((tm, N_BRANCH * BRANCH_W), lambda i, c: (i, OFF_SILU)),
                  row(BRANCH_W), row(BRANCH_W), row(BRANCH_W), row(BRANCH_W),
                  wm(0), wm(1), wm(2), wm(3),
                  pl.BlockSpec((None, N_BRANCH, BRANCH_W, tn), lambda i, c: (layer, 0, 0, c)),
                  pl.BlockSpec((None, tn, d), lambda i, c: (layer, c, 0))],
        out_specs=row(d),
        out_shape=jax.ShapeDtypeStruct((s, d), jnp.float32),
        scratch_shapes=[pltpu.VMEM((N_BRANCH, tm, BRANCH_W), jnp.bfloat16)],
        compiler_params=_cparams(("parallel", "arbitrary")),
        name="merge",
    )(x, h, proj, *ys, wmerge_all, wmerge_all, wmerge_all, wmerge_all, wbranch_all, wout_all)


def _rope_cos_sin(pos, dim):
    inv = ROPE_THETA ** (-jnp.arange(0, dim, 2, dtype=jnp.float32) / dim)
    ang = pos.astype(jnp.float32)[:, None] * inv[None, :]
    return jnp.cos(ang), jnp.sin(ang)


def _tables_a(s):
    cos, sin = _rope_cos_sin(jnp.arange(s, dtype=jnp.int32), A_ROPE)
    z16, z32 = jnp.zeros((s, 16), jnp.float32), jnp.zeros((s, 32), jnp.float32)
    one = jnp.ones((s, A_NOPE), jnp.float32)
    zero = jnp.zeros((s, A_NOPE), jnp.float32)
    c = jnp.concatenate([one, cos, cos, z32], axis=1)
    sa = jnp.concatenate([zero, -sin, z16, z32], axis=1)
    sb = jnp.concatenate([zero, z16, sin, z32], axis=1)
    return c, sa, sb


def _tables_b(s):
    rows = s // GRID_W
    row_pos = jnp.repeat(jnp.arange(rows, dtype=jnp.int32), GRID_W)
    col_pos = jnp.tile(jnp.arange(GRID_W, dtype=jnp.int32), rows)
    cr, sr = _rope_cos_sin(row_pos, HEAD_DIM // 2)
    cc, sc = _rope_cos_sin(col_pos, HEAD_DIM // 2)
    z = jnp.zeros_like(sr)
    c = jnp.concatenate([cr, cr, cc, cc] * 2, axis=1)
    sa = jnp.concatenate([-sr, z, -sc, z] * 2, axis=1)
    sb = jnp.concatenate([z, sr, z, sc] * 2, axis=1)
    return c, sa, sb


def _pad_lanes(v, width=LANES):
    return jnp.pad(v, ((0, width - v.shape[0]),))[None, :]


def kernel(x, norm_w, w_in, mla_q_norm, mla_kv_norm, mla_w_uq, mla_w_ukv, mla_qk_norm,
           gqa_qk_norm, dil_qk_norm, diff_qk_norm, diff_lambda, diff_subnorm, rel_bias,
           w_branch, w_out):
    b, s, d = x.shape
    assert b == 1 and d == D_MODEL
    depth = norm_w.shape[0]
    bf = jnp.bfloat16

    w_main = jnp.concatenate(
        [w_in[:, :, SRC_SILU:SRC_MERGE], w_in[:, :, SRC_C:SRC_D], w_in[:, :, SRC_D:SRC_SILU],
         w_in[:, :, SRC_B:SRC_C], w_in[:, :, SRC_A:SRC_B],
         jnp.zeros((depth, d, A_BLOCK - SRC_B), w_in.dtype)], axis=2).astype(bf)
    w_merge = w_in[:, :, SRC_MERGE:].astype(bf)
    w_branch_b = w_branch.astype(bf)
    w_out_b = w_out.astype(bf)
    wuq = mla_w_uq.reshape(depth, Q_LORA, A_HEADS, A_QK)
    wuq = jnp.pad(wuq, ((0, 0), (0, 0), (0, 0), (0, LANES - A_QK)))
    wuq = wuq.reshape(depth, Q_LORA, A_HEADS * LANES).astype(bf)
    wukv = mla_w_ukv.astype(bf)

    tabs_a = _tables_a(s)
    tabs_b = _tables_b(s)
    bands_c = [_band(rel_bias, C_HEADS, C_QT, C_BAND_W, C_BAND_W, pad=2 * C_QT, dil=dil,
                     half=C_HALF, col0=g * C_HEADS, name=f"band_c{g}")
               for g, (_, dil) in enumerate(C_CONFIGS)]
    band_w = D_TQ + 2 * D_BAND
    band_d = _band(rel_bias, D_HEADS, D_TQ, band_w, band_w // 3, pad=D_BAND, dil=1, half=None,
                   col0=C_GROUPS * C_HEADS, name="band_d")
    c_ut = [max(C_QT, 512 // dil) for _, dil in C_CONFIGS]

    xs = x[0]
    for l in range(depth):
        h = _rmsnorm(xs, norm_w[l][None, :])
        proj = _in_proj(h, w_main, l)

        qa, kta, va = _prep_a(proj, mla_q_norm[l][None, :], mla_kv_norm[l][None, :],
                              wuq[l], wukv[l], _pad_lanes(mla_qk_norm[l, 0]),
                              _pad_lanes(mla_qk_norm[l, 1]), tabs_a)
        y_a = _attn_a(qa, kta, va)

        g2 = lambda v: jnp.tile(v, 2)[None, :]
        qb, ktb, vb = _prep_b(proj, g2(gqa_qk_norm[l, 0]), g2(gqa_qk_norm[l, 1]), tabs_b)
        y_b = _attn_b(qb, ktb, vb)

        os, lses = [], []
        for g in range(C_GROUPS):
            qg, ktg, vg = _prep_c(proj, g2(dil_qk_norm[l, 0, g]), g2(dil_qk_norm[l, 1, g]),
                                  g, c_ut[g])
            og, lg = _attn_c(qg, ktg, vg, bands_c[g])
            os.append(og)
            lses.append(lg)
        y_c = _combine_c(os, lses, s)

        qd, ktd, vd = _prep_d(proj, g2(diff_qk_norm[l, 0]), g2(diff_qk_norm[l, 1]))
        lambda_init = 0.8 - 0.6 * math.exp(-0.3 * l)
        y_d = _attn_d(qd, ktd, vd, band_d, diff_lambda[l], diff_subnorm[l][None, :], lambda_init)

        xs = _merge(xs, h, proj, (y_a, y_b, y_c, y_d), w_merge, w_branch_b, w_out_b, l)
    return xs[None]
```

```python
import functools
import math

import numpy as np
import jax
import jax.numpy as jnp
from jax import lax
from jax.experimental import pallas as pl
from jax.experimental.pallas import tpu as pltpu

D_MODEL = 2048
GRID_W = 64
HEAD_DIM = 64
BRANCH_W = 512
N_BRANCH = 4
ROPE_THETA = 10000.0
EPS = 1e-6
NEG = -1e30

A_HEADS = 8
A_NOPE = 64
A_ROPE = 32
A_V = 64
A_QK = A_NOPE + A_ROPE
Q_LORA = 384
KV_LORA = 128

B_HEADS = 8
B_KV_HEADS = 2

C_HEADS = 8
C_CONFIGS = ((128, 1), (512, 4), (2048, 16))
C_GROUPS = len(C_CONFIGS)

D_HEADS = 4
D_V = 2 * HEAD_DIM

NUM_BUCKETS = 32
T5_MAX_DISTANCE = 1024

LANES = 128

OFF_SILU = 0
OFF_CQ, OFF_CK, OFF_CV = 2048, 3584, 5120
OFF_DQ, OFF_DK, OFF_DV = 6656, 7168, 7680
OFF_BQ, OFF_BKV = 8192, 8704
OFF_A = 8960
A_BLOCK = 640
N_MAIN = 9600

SRC_A, SRC_B, SRC_C, SRC_D, SRC_SILU, SRC_MERGE = 0, 544, 1312, 5920, 7456, 9504

VMEM_LIMIT = 56 * 1024 * 1024

LOG2E = math.log2(math.e)

C_QT = 128
C_HALF = 64
C_DIL_MAX = max(d for _, d in C_CONFIGS)
C_BAND_W = 5 * C_QT

D_TQ = 512
D_TK = 512
D_BAND = 1152


def _bucket_thresholds():
    nb = NUM_BUCKETS // 2
    max_exact = nb // 2
    n = np.arange(max_exact, 4 * T5_MAX_DISTANCE, dtype=np.float32)
    large = max_exact + (np.log(n / np.float32(max_exact))
                         / np.float32(math.log(T5_MAX_DISTANCE / max_exact))
                         * np.float32(nb - max_exact)).astype(np.int32)
    large = np.minimum(large, nb - 1)
    return tuple(int(n[np.argmax(large >= max_exact + k)]) for k in range(1, nb - max_exact))


BUCKET_STEPS = _bucket_thresholds()
assert D_BAND - D_TK + 1 >= BUCKET_STEPS[-1]


def _cparams(sem):
    return pltpu.CompilerParams(dimension_semantics=sem, vmem_limit_bytes=VMEM_LIMIT)


def _rmsnorm_kernel(x_ref, w_ref, o_ref):
    x = x_ref[...]
    ms = jnp.mean(x * x, axis=-1, keepdims=True)
    o_ref[...] = (x * lax.rsqrt(ms + EPS) * w_ref[...]).astype(o_ref.dtype)


def _rmsnorm(x, w, tm=512):
    s, d = x.shape
    return pl.pallas_call(
        _rmsnorm_kernel,
        grid=(s // tm,),
        in_specs=[pl.BlockSpec((tm, d), lambda i: (i, 0)),
                  pl.BlockSpec((1, d), lambda i: (0, 0))],
        out_specs=pl.BlockSpec((tm, d), lambda i: (i, 0)),
        out_shape=jax.ShapeDtypeStruct((s, d), jnp.bfloat16),
        compiler_params=_cparams(("parallel",)),
        name="rmsnorm",
    )(x, w)


def _matmul_kernel(a_ref, b_ref, o_ref):
    o_ref[...] = jnp.dot(a_ref[...], b_ref[...], preferred_element_type=jnp.float32)


def _in_proj(h, w_all, layer, tm=1024, tn=1920):
    s, d = h.shape
    n = w_all.shape[-1]
    return pl.pallas_call(
        _matmul_kernel,
        grid=(s // tm, n // tn),
        in_specs=[pl.BlockSpec((tm, d), lambda i, j: (i, 0)),
                  pl.BlockSpec((None, d, tn), lambda i, j: (layer, 0, j))],
        out_specs=pl.BlockSpec((tm, tn), lambda i, j: (i, j)),
        out_shape=jax.ShapeDtypeStruct((s, n), jnp.float32),
        compiler_params=_cparams(("parallel", "arbitrary")),
        name="in_proj",
    )(h, w_all)


def _band_kernel(tab_ref, o_ref, *, pad, dil, half, col0, key_axis):
    h = pl.program_id(0)
    rows, width = o_ref.shape
    col = lax.broadcasted_iota(jnp.int32, (rows, width), 1)
    row = lax.broadcasted_iota(jnp.int32, (rows, width), 0) + pl.program_id(1) * rows
    rel_u = (col - row if key_axis == 1 else row - col) - pad
    rel = rel_u * dil
    n = jnp.abs(rel)
    nb = NUM_BUCKETS // 2
    max_exact = nb // 2
    large = jnp.full((rows, width), max_exact, jnp.int32)
    for t in BUCKET_STEPS:
        large = large + jnp.where(n >= t, 1, 0)
    bucket = jnp.where(rel > 0, nb, 0) + jnp.where(n < max_exact, n, large)
    val = jnp.zeros((rows, width), jnp.float32)
    for b in range(NUM_BUCKETS):
        val = jnp.where(bucket == b, tab_ref[b, col0 + h], val)
    val = val * LOG2E
    if half is not None:
        val = jnp.where(jnp.abs(rel_u) <= half, val, NEG)
    o_ref[...] = val


def _band(rel_bias, heads, rows, width, rblk, *, pad, dil, half, col0, key_axis, name):
    return pl.pallas_call(
        functools.partial(_band_kernel, pad=pad, dil=dil, half=half, col0=col0,
                          key_axis=key_axis),
        grid=(heads, rows // rblk),
        in_specs=[pl.BlockSpec(memory_space=pltpu.SMEM)],
        out_specs=pl.BlockSpec((None, rblk, width), lambda h, j: (h, j, 0)),
        out_shape=jax.ShapeDtypeStruct((heads, rows, width), jnp.float32),
        compiler_params=_cparams(("parallel", "arbitrary")),
        name=name,
    )(rel_bias)


def _lane_lo(shape):
    return (lax.broadcasted_iota(jnp.int32, shape, len(shape) - 1) % LANES) < HEAD_DIM


def _rope(x, c, sa, sb):
    return x * c + pltpu.roll(x, LANES - 16, 1) * sa + pltpu.roll(x, 16, 1) * sb


def _halfnorm(x, gain, lo):
    sq = x * x
    s_lo = jnp.sum(jnp.where(lo, sq, 0.0), axis=-1, keepdims=True)
    s_hi = jnp.sum(jnp.where(lo, 0.0, sq), axis=-1, keepdims=True)
    r = jnp.where(lo, lax.rsqrt(s_lo * (1.0 / HEAD_DIM) + EPS),
                  lax.rsqrt(s_hi * (1.0 / HEAD_DIM) + EPS))
    return x * r * gain


ONES_ROWS = 16


def _prep_a_kernel(p_ref, qn_ref, kvn_ref, wuq_ref, wukv_ref, gq_ref, gk_ref,
                   c_ref, sa_ref, sb_ref, qt_out, k_out, vt_out):
    p = p_ref[...]
    cq = p[:, :Q_LORA]
    cq = cq * lax.rsqrt(jnp.mean(cq * cq, axis=-1, keepdims=True) + EPS) * qn_ref[...]
    q = jnp.dot(cq.astype(jnp.bfloat16), wuq_ref[...], preferred_element_type=jnp.float32)
    ckv = p[:, Q_LORA:Q_LORA + KV_LORA]
    ckv = ckv * lax.rsqrt(jnp.mean(ckv * ckv, axis=-1, keepdims=True) + EPS) * kvn_ref[...]
    kvu = jnp.dot(ckv.astype(jnp.bfloat16), wukv_ref[...], preferred_element_type=jnp.float32)
    kr = pltpu.roll(p[:, Q_LORA + KV_LORA:], HEAD_DIM, 1)
    lo = _lane_lo(kr.shape)
    c, sa, sb = c_ref[...], sa_ref[...], sb_ref[...]
    scale = A_QK ** -0.5 * LOG2E
    ones = jnp.ones((ONES_ROWS, p.shape[0]), vt_out.dtype)
    for h in range(A_HEADS):
        qh = q[:, h * LANES:(h + 1) * LANES]
        ss = jnp.sum(qh * qh, axis=-1, keepdims=True) * (1.0 / A_QK)
        qh = qh * lax.rsqrt(ss + EPS) * gq_ref[...]
        qt_out[h] = (_rope(qh, c, sa, sb) * scale).T.astype(qt_out.dtype)
        kvh = kvu[:, h * LANES:(h + 1) * LANES]
        kh = jnp.where(lo, kvh, kr)
        ss = jnp.sum(kh * kh, axis=-1, keepdims=True) * (1.0 / A_QK)
        kh = kh * lax.rsqrt(ss + EPS) * gk_ref[...]
        k_out[h] = _rope(kh, c, sa, sb).astype(k_out.dtype)
        vt_out[h, :A_V, :] = kvh.T[A_NOPE:].astype(vt_out.dtype)
        vt_out[h, A_V:, :] = ones


def _prep_a(proj, qn, kvn, wuq, wukv, gq, gk, tabs, tm=256):
    s = proj.shape[0]
    row = lambda w: pl.BlockSpec((1, w), lambda i: (0, 0))
    tab = pl.BlockSpec((tm, LANES), lambda i: (i, 0))
    return pl.pallas_call(
        _prep_a_kernel,
        grid=(s // tm,),
        in_specs=[pl.BlockSpec((tm, A_BLOCK), lambda i: (i, OFF_A // A_BLOCK)),
                  row(Q_LORA), row(KV_LORA),
                  pl.BlockSpec(wuq.shape, lambda i: (0, 0)),
                  pl.BlockSpec(wukv.shape, lambda i: (0, 0)),
                  row(LANES), row(LANES), tab, tab, tab],
        out_specs=[pl.BlockSpec((A_HEADS, LANES, tm), lambda i: (0, 0, i)),
                   pl.BlockSpec((A_HEADS, tm, LANES), lambda i: (0, i, 0)),
                   pl.BlockSpec((A_HEADS, A_V + ONES_ROWS, tm), lambda i: (0, 0, i))],
        out_shape=[jax.ShapeDtypeStruct((A_HEADS, LANES, s), jnp.bfloat16),
                   jax.ShapeDtypeStruct((A_HEADS, s, LANES), jnp.bfloat16),
                   jax.ShapeDtypeStruct((A_HEADS, A_V + ONES_ROWS, s), jnp.bfloat16)],
        compiler_params=_cparams(("parallel",)),
        name="prep_a",
    )(proj, qn, kvn, wuq, wukv, gq, gk, *tabs)


B_TQ = 256


def _prep_b_kernel(q_ref, kv_ref, gq_ref, gk_ref, c_ref, sa_ref, sb_ref,
                   qt_out, k_out, vt_out):
    c, sa, sb = c_ref[...], sa_ref[...], sb_ref[...]
    lo = _lane_lo(c.shape)
    tm = c.shape[0]
    scale = HEAD_DIM ** -0.5 * LOG2E
    q = q_ref[...]
    for b in range(B_HEADS // 2):
        x = _rope(_halfnorm(q[:, b * LANES:(b + 1) * LANES], gq_ref[...], lo), c, sa, sb) * scale
        xr = pltpu.roll(x, HEAD_DIM, 1)
        g = b // 2
        even, odd = (x, xr) if g == 0 else (xr, x)
        keep = lo if g == 0 else jnp.logical_not(lo)
        for j, xh in enumerate((even, odd)):
            hh = 2 * (b % 2) + j
            qt_out[g, :, hh * tm:(hh + 1) * tm] = jnp.where(keep, xh, 0.0).T.astype(qt_out.dtype)
    kv = kv_ref[...]
    k_out[...] = _rope(_halfnorm(kv[:, :LANES], gk_ref[...], lo), c, sa, sb).astype(k_out.dtype)
    vt = kv[:, LANES:].T.astype(vt_out.dtype)
    ones = jnp.ones((ONES_ROWS, tm), vt_out.dtype)
    for g in range(B_KV_HEADS):
        vt_out[g, :HEAD_DIM, :] = vt[g * HEAD_DIM:(g + 1) * HEAD_DIM]
        vt_out[g, HEAD_DIM:, :] = ones


def _prep_b(proj, gq, gk, tabs, tm=B_TQ):
    s = proj.shape[0]
    row = pl.BlockSpec((1, LANES), lambda i: (0, 0))
    tab = pl.BlockSpec((tm, LANES), lambda i: (i, 0))
    rv = HEAD_DIM + ONES_ROWS
    return pl.pallas_call(
        _prep_b_kernel,
        grid=(s // tm,),
        in_specs=[pl.BlockSpec((tm, 512), lambda i: (i, OFF_BQ // 512)),
                  pl.BlockSpec((tm, 256), lambda i: (i, OFF_BKV // 256)),
                  row, row, tab, tab, tab],
        out_specs=[pl.BlockSpec((2, None, LANES, 4 * tm), lambda i: (0, i, 0, 0)),
                   pl.BlockSpec((tm, LANES), lambda i: (i, 0)),
                   pl.BlockSpec((2, rv, tm), lambda i: (0, 0, i))],
        out_shape=[jax.ShapeDtypeStruct((2, s // tm, LANES, 4 * tm), jnp.bfloat16),
                   jax.ShapeDtypeStruct((s, LANES), jnp.bfloat16),
                   jax.ShapeDtypeStruct((2, rv, s), jnp.bfloat16)],
        compiler_params=_cparams(("parallel",)),
        name="prep_b",
    )(proj, proj, gq, gk, *tabs)


def _prep_d_kernel(q_ref, k_ref, v_ref, gq_ref, gk_ref, qt_out, k_out, vt_out):
    q, k, v = q_ref[...], k_ref[...], v_ref[...]
    tm = q.shape[0]
    lo = _lane_lo((tm, LANES))
    scale = HEAD_DIM ** -0.5 * LOG2E
    ones = jnp.ones((ONES_ROWS, tm), vt_out.dtype)
    for h in range(D_HEADS):
        cols = slice(h * LANES, (h + 1) * LANES)
        x = _halfnorm(q[:, cols], gq_ref[...], lo) * scale
        qt_out[h, :, :tm] = jnp.where(lo, x, 0.0).T.astype(qt_out.dtype)
        qt_out[h, :, tm:] = jnp.where(lo, 0.0, x).T.astype(qt_out.dtype)
        k_out[:, cols] = _halfnorm(k[:, cols], gk_ref[...], lo).astype(k_out.dtype)
        vt_out[h, :D_V, :] = v[:, cols].T.astype(vt_out.dtype)
        vt_out[h, D_V:, :] = ones


def _prep_d(proj, gq, gk, tm=D_TQ):
    s = proj.shape[0]
    row = pl.BlockSpec((1, LANES), lambda i: (0, 0))
    blk = lambda off: pl.BlockSpec((tm, 512), lambda i: (i, off // 512))
    rv = D_V + ONES_ROWS
    return pl.pallas_call(
        _prep_d_kernel,
        grid=(s // tm,),
        in_specs=[blk(OFF_DQ), blk(OFF_DK), blk(OFF_DV), row, row],
        out_specs=[pl.BlockSpec((D_HEADS, None, LANES, 2 * tm), lambda i: (0, i, 0, 0)),
                   pl.BlockSpec((tm, 512), lambda i: (i, 0)),
                   pl.BlockSpec((D_HEADS, rv, tm), lambda i: (0, 0, i))],
        out_shape=[jax.ShapeDtypeStruct((D_HEADS, s // tm, LANES, 2 * tm), jnp.bfloat16),
                   jax.ShapeDtypeStruct((s, 512), jnp.bfloat16),
                   jax.ShapeDtypeStruct((D_HEADS, rv, s), jnp.bfloat16)],
        compiler_params=_cparams(("parallel",)),
        name="prep_d",
    )(proj, proj, proj, gq, gk)


def _prep_c_kernel(q_ref, k_ref, v_ref, gq_ref, gk_ref, q_out, kt_out, v_out, *, dil, ut):
    lo = _lane_lo((ut, LANES))
    scale = HEAD_DIM ** -0.5 * LOG2E
    for r in range(dil):
        rows = pl.ds(r, ut, stride=dil) if dil > 1 else pl.ds(0, ut)
        q_out[r] = (_halfnorm(q_ref[rows, :], gq_ref[...], lo) * scale).astype(q_out.dtype)
        kt_out[r] = _halfnorm(k_ref[rows, :], gk_ref[...], lo).T.astype(kt_out.dtype)
        v_out[r] = v_ref[rows, :].astype(v_out.dtype)


def _prep_c(proj, gq, gk, g, ut):
    s = proj.shape[0]
    dil = C_CONFIGS[g][1]
    seg = s // dil
    nb = C_HEADS // 2
    tn = ut * dil
    blk = lambda off: pl.BlockSpec((tn, LANES), lambda i, b: (i, off // LANES + g * nb + b))
    row = pl.BlockSpec((1, LANES), lambda i, b: (0, 0))
    return pl.pallas_call(
        functools.partial(_prep_c_kernel, dil=dil, ut=ut),
        grid=(s // tn, nb),
        in_specs=[blk(OFF_CQ), blk(OFF_CK), blk(OFF_CV), row, row],
        out_specs=[pl.BlockSpec((dil, ut, LANES), lambda i, b: (0, i, b)),
                   pl.BlockSpec((dil, LANES, ut), lambda i, b: (0, b, i)),
                   pl.BlockSpec((dil, ut, LANES), lambda i, b: (0, i, b))],
        out_shape=[jax.ShapeDtypeStruct((dil, seg, nb * LANES), jnp.bfloat16),
                   jax.ShapeDtypeStruct((dil, nb * LANES, seg), jnp.bfloat16),
                   jax.ShapeDtypeStruct((dil, seg, nb * LANES), jnp.bfloat16)],
        compiler_params=_cparams(("parallel", "parallel")),
        name=f"prep_c{g}",
    )(proj, proj, proj, gq, gk)


def _scores(k, qt, s_ref, slot, bias=None):
    st = jnp.dot(k, qt, preferred_element_type=jnp.float32)
    s_ref[slot] = st if bias is None else st + bias


def _consume(s_ref, slot, vt, m_ref, acc_ref):
    st = s_ref[slot]
    m_prev = m_ref[...]
    m_next = jnp.maximum(m_prev, jnp.max(st, axis=0, keepdims=True))
    p = jnp.exp2(st - m_next)
    alpha = jnp.exp2(m_prev - m_next)
    m_ref[...] = m_next
    pv = jnp.dot(vt, p.astype(jnp.bfloat16), preferred_element_type=jnp.float32)
    acc_ref[...] = alpha * acc_ref[...] + pv


FLASH_UNROLL = 4


def _flash_loop(nk, score, consume):
    u = min(FLASH_UNROLL, nk)
    assert nk % u == 0 and u % 2 == 0
    score(0, 0)

    def body(j, carry):
        for i in range(u):
            score(j * u + i + 1, (i + 1) % 2)
            consume(j * u + i, i % 2)
        return carry

    lax.fori_loop(0, nk // u - 1, body, 0)
    for c in range(nk - u, nk):
        if c + 1 < nk:
            score(c + 1, (c + 1) % 2)
        consume(c, c % 2)


def _init_stats(m_ref, acc_ref):
    m_ref[...] = jnp.full_like(m_ref, -jnp.inf)
    acc_ref[...] = jnp.zeros_like(acc_ref)


def _normalised(acc_ref, rv):
    acc = acc_ref[...]
    return acc[:rv] / acc[rv:rv + 1]


def _flash_scratch(streams, tk, n, rv):
    return ([pltpu.VMEM((2, tk, n), jnp.float32)] * streams
            + [pltpu.VMEM((1, n), jnp.float32)] * streams
            + [pltpu.VMEM((rv, n), jnp.float32)] * streams)


def _attn_a_kernel(qt_ref, k_ref, vt_ref, o_ref, s0_ref, s1_ref, m0_ref, m1_ref,
                   acc0_ref, acc1_ref, *, tk, nk):
    streams = ((s0_ref, m0_ref, acc0_ref), (s1_ref, m1_ref, acc1_ref))
    for _, m_ref, acc_ref in streams:
        _init_stats(m_ref, acc_ref)

    def score(c, slot):
        k0 = pl.multiple_of(c * tk, tk)
        for hh, (s_ref, _, _) in enumerate(streams):
            _scores(k_ref[hh, pl.ds(k0, tk), :], qt_ref[hh], s_ref, slot)

    def consume(c, slot):
        k0 = pl.multiple_of(c * tk, tk)
        for hh, (s_ref, m_ref, acc_ref) in enumerate(streams):
            _consume(s_ref, slot, vt_ref[hh, :, pl.ds(k0, tk)], m_ref, acc_ref)

    _flash_loop(nk, score, consume)
    o_ref[...] = jnp.concatenate([_normalised(acc0_ref, A_V), _normalised(acc1_ref, A_V)],
                                 axis=0).T


def _attn_a(qt, k, vt, tq=512, tk=512):
    s = k.shape[1]
    tk = min(tk, s // 2)
    rv = vt.shape[1]
    return pl.pallas_call(
        functools.partial(_attn_a_kernel, tk=tk, nk=s // tk),
        grid=(A_HEADS // 2, s // tq),
        in_specs=[pl.BlockSpec((2, LANES, tq), lambda p, i: (p, 0, i)),
                  pl.BlockSpec((2, s, LANES), lambda p, i: (p, 0, 0)),
                  pl.BlockSpec((2, rv, s), lambda p, i: (p, 0, 0))],
        out_specs=pl.BlockSpec((tq, LANES), lambda p, i: (i, p)),
        out_shape=jax.ShapeDtypeStruct((s, BRANCH_W), jnp.float32),
        scratch_shapes=_flash_scratch(2, tk, tq, rv),
        compiler_params=_cparams(("parallel", "arbitrary")),
        name="attn_a",
    )(qt, k, vt)


def _attn_b_kernel(qt_ref, k_ref, vt_ref, o_ref, s_ref, m_ref, acc_ref, *, tq, tk, nk):
    _init_stats(m_ref, acc_ref)

    def score(c, slot):
        _scores(k_ref[pl.ds(pl.multiple_of(c * tk, tk), tk), :], qt_ref[...], s_ref, slot)

    def consume(c, slot):
        _consume(s_ref, slot, vt_ref[:, pl.ds(pl.multiple_of(c * tk, tk), tk)], m_ref, acc_ref)

    _flash_loop(nk, score, consume)
    ot = _normalised(acc_ref, HEAD_DIM)
    o_ref[...] = jnp.concatenate([ot[:, h * tq:(h + 1) * tq] for h in range(4)], axis=0).T


def _attn_b(qt, k, vt, tq=B_TQ, tk=512):
    s = k.shape[0]
    tk = min(tk, s // 2)
    rv = vt.shape[1]
    return pl.pallas_call(
        functools.partial(_attn_b_kernel, tq=tq, tk=tk, nk=s // tk),
        grid=(2, s // tq),
        in_specs=[pl.BlockSpec((None, None, LANES, 4 * tq), lambda g, i: (g, i, 0, 0)),
                  pl.BlockSpec((s, LANES), lambda g, i: (0, 0)),
                  pl.BlockSpec((None, rv, s), lambda g, i: (g, 0, 0))],
        out_specs=pl.BlockSpec((tq, 2 * LANES), lambda g, i: (i, g)),
        out_shape=jax.ShapeDtypeStruct((s, BRANCH_W), jnp.float32),
        scratch_shapes=_flash_scratch(1, tk, 4 * tq, rv),
        compiler_params=_cparams(("parallel", "arbitrary")),
        name="attn_b",
    )(qt, k, vt)


def _attn_d_kernel(qt_ref, k_ref, vt_ref, band_ref, lam_ref, sub_ref, o_ref,
                   s_ref, m_ref, acc_ref, *, tq, tk, nk, lambda_init):
    _init_stats(m_ref, acc_ref)
    q0 = pl.program_id(1) * tq
    hi = tq + 2 * D_BAND - tk

    def score(c, slot):
        k0 = pl.multiple_of(c * tk, tk)
        b0 = pl.multiple_of(jnp.clip(k0 - q0 + D_BAND, 0, hi), LANES)
        bias = band_ref[pl.ds(b0, tk), :]
        _scores(k_ref[pl.ds(k0, tk), :], qt_ref[...], s_ref, slot,
                jnp.concatenate([bias, bias], axis=1))

    def consume(c, slot):
        _consume(s_ref, slot, vt_ref[:, pl.ds(pl.multiple_of(c * tk, tk), tk)], m_ref, acc_ref)

    _flash_loop(nk, score, consume)
    lv = lam_ref[...]
    lam = (jnp.exp(jnp.sum(lv[0:1] * lv[1:2], axis=-1, keepdims=True))
           - jnp.exp(jnp.sum(lv[2:3] * lv[3:4], axis=-1, keepdims=True)) + lambda_init)
    ot = _normalised(acc_ref, D_V)
    o = (ot[:, :tq] - lam * ot[:, tq:]).T
    o = o * lax.rsqrt(jnp.mean(o * o, axis=-1, keepdims=True) + EPS) * sub_ref[...]
    o_ref[...] = o * (1.0 - lambda_init)


def _attn_d(qt, k, vt, band, lam_vecs, subw, lambda_init, tq=D_TQ, tk=D_TK):
    s = k.shape[0]
    rv = vt.shape[1]
    return pl.pallas_call(
        functools.partial(_attn_d_kernel, tq=tq, tk=tk, nk=s // tk, lambda_init=lambda_init),
        grid=(D_HEADS, s // tq),
        in_specs=[pl.BlockSpec((None, None, LANES, 2 * tq), lambda h, i: (h, i, 0, 0)),
                  pl.BlockSpec((s, LANES), lambda h, i: (0, h)),
                  pl.BlockSpec((None, rv, s), lambda h, i: (h, 0, 0)),
                  pl.BlockSpec((None, tq + 2 * D_BAND, tq), lambda h, i: (h, 0, 0)),
                  pl.BlockSpec((4, HEAD_DIM), lambda h, i: (0, 0)),
                  pl.BlockSpec((1, LANES), lambda h, i: (0, 0))],
        out_specs=pl.BlockSpec((tq, LANES), lambda h, i: (i, h)),
        out_shape=jax.ShapeDtypeStruct((s, BRANCH_W), jnp.float32),
        scratch_shapes=_flash_scratch(1, tk, 2 * tq, rv),
        compiler_params=_cparams(("parallel", "arbitrary")),
        name="attn_d",
    )(qt, k, vt, band, lam_vecs, subw)


def _attn_c_kernel(q_ref, kt_ref, v_ref, band_ref, o_ref, lse_ref, *, nt, seg, win):
    j = pl.program_id(2)
    lo = _lane_lo((C_QT, LANES))

    def body(t, carry):
        r0 = pl.multiple_of(t * C_QT, C_QT)
        u0 = j * (nt * C_QT) + r0
        ws = pl.multiple_of(jnp.clip(u0 - C_QT, 0, seg - win), C_QT)
        x = pl.multiple_of(ws - u0 + 2 * C_QT, C_QT)
        q = q_ref[pl.ds(r0, C_QT), :]
        zero = jnp.zeros_like(q)
        q2 = jnp.concatenate([jnp.where(lo, q, zero), jnp.where(lo, zero, q)], axis=0)
        s = jnp.dot(q2, kt_ref[:, pl.ds(ws, win)], preferred_element_type=jnp.float32)
        s = s + jnp.concatenate([band_ref[0, :, pl.ds(x, win)], band_ref[1, :, pl.ds(x, win)]], axis=0)
        m = jnp.max(s, axis=1, keepdims=True)
        p = jnp.exp2(s - m)
        l = jnp.sum(p, axis=1, keepdims=True)
        pv = jnp.dot(p.astype(jnp.bfloat16), v_ref[pl.ds(ws, win), :],
                     preferred_element_type=jnp.float32)
        o = pv / l
        lse = m + jnp.log2(l)
        o_ref[pl.ds(r0, C_QT), :] = jnp.where(lo, o[:C_QT], o[C_QT:])
        lse_ref[pl.ds(r0, C_QT), :] = jnp.where(lo, lse[:C_QT], lse[C_QT:])
        return carry

    lax.fori_loop(0, nt, body, 0)


def _attn_c(q, kt, v, band, nt_max=8):
    dil, seg, w = q.shape
    nt = min(nt_max, seg // C_QT)
    win = min(3 * C_QT, seg)
    tqb = nt * C_QT
    out = jax.ShapeDtypeStruct((dil, seg, w), jnp.float32)
    ospec = pl.BlockSpec((None, tqb, LANES), lambda r, b, j: (r, j, b))
    return pl.pallas_call(
        functools.partial(_attn_c_kernel, nt=nt, seg=seg, win=win),
        grid=(dil, C_HEADS // 2, seg // tqb),
        in_specs=[pl.BlockSpec((None, tqb, LANES), lambda r, b, j: (r, j, b)),
                  pl.BlockSpec((None, LANES, seg), lambda r, b, j: (r, b, 0)),
                  pl.BlockSpec((None, seg, LANES), lambda r, b, j: (r, 0, b)),
                  pl.BlockSpec((2, C_QT, C_BAND_W), lambda r, b, j: (b, 0, 0))],
        out_specs=[ospec, ospec],
        out_shape=[out, out],
        compiler_params=_cparams(("parallel", "parallel", "arbitrary")),
        name=f"attn_c_d{dil}",
    )(q, kt, v, band)


def _combine_c_kernel(*refs):
    o_refs, l_refs, out_ref = refs[0:2 * C_GROUPS:2], refs[1:2 * C_GROUPS:2], refs[-1]
    n = out_ref.shape[0] // C_DIL_MAX
    for r in range(C_DIL_MAX):
        outs, lses = [], []
        for (_, dil), o_ref, l_ref in zip(C_CONFIGS, o_refs, l_refs):
            rows = pl.ds(r // dil, n, stride=C_DIL_MAX // dil)
            cls = pl.ds(r % dil, 1)
            outs.append(o_ref[cls, rows, :][0])
            lses.append(l_ref[cls, rows, :][0])
        m = functools.reduce(jnp.maximum, lses)
        es = [jnp.exp2(l - m) for l in lses]
        num = sum(e * o for e, o in zip(es, outs))
        out_ref[pl.ds(r, n, stride=C_DIL_MAX), :] = num / sum(es)


def _combine_c(os, lses, s, tn=2048):
    nb = C_HEADS // 2
    args, in_specs = [], []
    for (_, dil), o, l in zip(C_CONFIGS, os, lses):
        spec = pl.BlockSpec((dil, tn // dil, LANES), lambda i, b: (0, i, b))
        args += [o, l]
        in_specs += [spec, spec]
    return pl.pallas_call(
        _combine_c_kernel,
        grid=(s // tn, nb),
        in_specs=in_specs,
        out_specs=pl.BlockSpec((tn, LANES), lambda i, b: (i, b)),
        out_shape=jax.ShapeDtypeStruct((s, nb * LANES), jnp.float32),
        compiler_params=_cparams(("parallel", "parallel")),
        name="combine_c",
    )(*args)


def _merge_kernel(x_ref, h_ref, sg_ref, ya_ref, yb_ref, yc_ref, yd_ref,
                  wm0_ref, wm1_ref, wm2_ref, wm3_ref, wb_ref, wo_ref, o_ref, yg_ref):
    c = pl.program_id(1)

    @pl.when(c == 0)
    def _():
        sg = sg_ref[...]
        for n, y_ref in enumerate((ya_ref, yb_ref, yc_ref, yd_ref)):
            g = sg[:, n * BRANCH_W:(n + 1) * BRANCH_W]
            yg_ref[n] = (y_ref[...] * (g * jax.nn.sigmoid(g))).astype(yg_ref.dtype)
        o_ref[...] = x_ref[...]

    h = h_ref[...]
    mixed = 0.0
    for n, wm_ref in enumerate((wm0_ref, wm1_ref, wm2_ref, wm3_ref)):
        gate = jax.nn.sigmoid(jnp.dot(h, wm_ref[...], preferred_element_type=jnp.float32))
        z = jnp.dot(yg_ref[n], wb_ref[n], preferred_element_type=jnp.float32)
        mixed = mixed + gate * z
    o_ref[...] += jnp.dot(mixed.astype(jnp.bfloat16), wo_ref[...],
                          preferred_element_type=jnp.float32)


def _merge(x, h, proj, ys, wmerge_all, wbranch_all, wout_all, layer, tm=512, tn=256):
    s, d = x.shape
    nc = d // tn
    row = lambda w: pl.BlockSpec((tm, w), lambda i, c: (i, 0))
    wm = lambda n: pl.BlockSpec((None, d, tn), lambda i, c: (layer, 0, n * nc + c))
    return pl.pallas_call(
        _merge_kernel,
        grid=(s // tm, nc),
        in_specs=[row(d), row(d),
                  pl.BlockSpec((tm, N_BRANCH * BRANCH_W), lambda i, c: (i, OFF_SILU)),
                  row(BRANCH_W), row(BRANCH_W), row(BRANCH_W), row(BRANCH_W),
                  wm(0), wm(1), wm(2), wm(3),
                  pl.BlockSpec((None, N_BRANCH, BRANCH_W, tn), lambda i, c: (layer, 0, 0, c)),
                  pl.BlockSpec((None, tn, d), lambda i, c: (layer, c, 0))],
        out_specs=row(d),
        out_shape=jax.ShapeDtypeStruct((s, d), jnp.float32),
        scratch_shapes=[pltpu.VMEM((N_BRANCH, tm, BRANCH_W), jnp.bfloat16)],
        compiler_params=_cparams(("parallel", "arbitrary")),
        name="merge",
    )(x, h, proj, *ys, wmerge_all, wmerge_all, wmerge_all, wmerge_all, wbranch_all, wout_all)


def _rope_cos_sin(pos, dim):
    inv = ROPE_THETA ** (-jnp.arange(0, dim, 2, dtype=jnp.float32) / dim)
    ang = pos.astype(jnp.float32)[:, None] * inv[None, :]
    return jnp.cos(ang), jnp.sin(ang)


def _tables_a(s):
    cos, sin = _rope_cos_sin(jnp.arange(s, dtype=jnp.int32), A_ROPE)
    z16, z32 = jnp.zeros((s, 16), jnp.float32), jnp.zeros((s, 32), jnp.float32)
    one = jnp.ones((s, A_NOPE), jnp.float32)
    zero = jnp.zeros((s, A_NOPE), jnp.float32)
    c = jnp.concatenate([one, cos, cos, z32], axis=1)
    sa = jnp.concatenate([zero, -sin, z16, z32], axis=1)
    sb = jnp.concatenate([zero, z16, sin, z32], axis=1)
    return c, sa, sb


def _tables_b(s):
    rows = s // GRID_W
    row_pos = jnp.repeat(jnp.arange(rows, dtype=jnp.int32), GRID_W)
    col_pos = jnp.tile(jnp.arange(GRID_W, dtype=jnp.int32), rows)
    cr, sr = _rope_cos_sin(row_pos, HEAD_DIM // 2)
    cc, sc = _rope_cos_sin(col_pos, HEAD_DIM // 2)
    z = jnp.zeros_like(sr)
    c = jnp.concatenate([cr, cr, cc, cc] * 2, axis=1)
    sa = jnp.concatenate([-sr, z, -sc, z] * 2, axis=1)
    sb = jnp.concatenate([z, sr, z, sc] * 2, axis=1)
    return c, sa, sb


def _pad_lanes(v, width=LANES):
    return jnp.pad(v, ((0, width - v.shape[0]),))[None, :]


def kernel(x, norm_w, w_in, mla_q_norm, mla_kv_norm, mla_w_uq, mla_w_ukv, mla_qk_norm,
           gqa_qk_norm, dil_qk_norm, diff_qk_norm, diff_lambda, diff_subnorm, rel_bias,
           w_branch, w_out):
    b, s, d = x.shape
    assert b == 1 and d == D_MODEL
    depth = norm_w.shape[0]
    bf = jnp.bfloat16

    w_main = jnp.concatenate(
        [w_in[:, :, SRC_SILU:SRC_MERGE], w_in[:, :, SRC_C:SRC_D], w_in[:, :, SRC_D:SRC_SILU],
         w_in[:, :, SRC_B:SRC_C], w_in[:, :, SRC_A:SRC_B],
         jnp.zeros((depth, d, A_BLOCK - SRC_B), w_in.dtype)], axis=2).astype(bf)
    w_merge = w_in[:, :, SRC_MERGE:].astype(bf)
    w_branch_b = w_branch.astype(bf)
    w_out_b = w_out.astype(bf)
    wuq = mla_w_uq.reshape(depth, Q_LORA, A_HEADS, A_QK)
    wuq = jnp.pad(wuq, ((0, 0), (0, 0), (0, 0), (0, LANES - A_QK)))
    wuq = wuq.reshape(depth, Q_LORA, A_HEADS * LANES).astype(bf)
    wukv = mla_w_ukv.astype(bf)

    tabs_a = _tables_a(s)
    tabs_b = _tables_b(s)
    bands_c = [_band(rel_bias, C_HEADS, C_QT, C_BAND_W, C_QT, pad=2 * C_QT, dil=dil,
                     half=C_HALF, col0=g * C_HEADS, key_axis=1, name=f"band_c{g}")
               for g, (_, dil) in enumerate(C_CONFIGS)]
    band_r = D_TQ + 2 * D_BAND
    band_d = _band(rel_bias, D_HEADS, band_r, D_TQ, band_r // 2, pad=D_BAND, dil=1, half=None,
                   col0=C_GROUPS * C_HEADS, key_axis=0, name="band_d")
    c_ut = [max(C_QT, 512 // dil) for _, dil in C_CONFIGS]

    xs = x[0]
    for l in range(depth):
        h = _rmsnorm(xs, norm_w[l][None, :])
        proj = _in_proj(h, w_main, l)

        qa, kta, va = _prep_a(proj, mla_q_norm[l][None, :], mla_kv_norm[l][None, :],
                              wuq[l], wukv[l], _pad_lanes(mla_qk_norm[l, 0]),
                              _pad_lanes(mla_qk_norm[l, 1]), tabs_a)
        y_a = _attn_a(qa, kta, va)

        g2 = lambda v: jnp.tile(v, 2)[None, :]
        qb, ktb, vb = _prep_b(proj, g2(gqa_qk_norm[l, 0]), g2(gqa_qk_norm[l, 1]), tabs_b)
        y_b = _attn_b(qb, ktb, vb)

        os, lses = [], []
        for g in range(C_GROUPS):
            qg, ktg, vg = _prep_c(proj, g2(dil_qk_norm[l, 0, g]), g2(dil_qk_norm[l, 1, g]),
                                  g, c_ut[g])
            og, lg = _attn_c(qg, ktg, vg, bands_c[g])
            os.append(og)
            lses.append(lg)
        y_c = _combine_c(os, lses, s)

        qd, ktd, vd = _prep_d(proj, g2(diff_qk_norm[l, 0]), g2(diff_qk_norm[l, 1]))
        lambda_init = 0.8 - 0.6 * math.exp(-0.3 * l)
        y_d = _attn_d(qd, ktd, vd, band_d, diff_lambda[l], diff_subnorm[l][None, :], lambda_init)

        xs = _merge(xs, h, proj, (y_a, y_b, y_c, y_d), w_merge, w_branch_b, w_out_b, l)
    return xs[None]
```

```python
import functools
import math

import numpy as np
import jax
import jax.numpy as jnp
from jax import lax
from jax.experimental import pallas as pl
from jax.experimental.pallas import tpu as pltpu

D_MODEL = 2048
GRID_W = 64
HEAD_DIM = 64
BRANCH_W = 512
N_BRANCH = 4
ROPE_THETA = 10000.0
EPS = 1e-6
NEG = -1e30

A_HEADS = 8
A_NOPE = 64
A_ROPE = 32
A_V = 64
A_QK = A_NOPE + A_ROPE
Q_LORA = 384
KV_LORA = 128

B_HEADS = 8
B_KV_HEADS = 2

C_HEADS = 8
C_CONFIGS = ((128, 1), (512, 4), (2048, 16))
C_GROUPS = len(C_CONFIGS)

D_HEADS = 4
D_V = 2 * HEAD_DIM

NUM_BUCKETS = 32
T5_MAX_DISTANCE = 1024

LANES = 128

SRC_A, SRC_B, SRC_C, SRC_D, SRC_SILU, SRC_MERGE, SRC_END = 0, 544, 1312, 5920, 7456, 9504, 17696

W_BLK = 256
OFF_SILU = 0
OFF_CQ, OFF_CK, OFF_CV = 2048, 3584, 5120
OFF_DQ, OFF_DK, OFF_DV = 6656, 7168, 7680
OFF_BQ, OFF_BKV = 8192, 8704
N_MAIN = 8960
OFF_MERGE = N_MAIN
N_REST = N_MAIN + SRC_END - SRC_MERGE
A_BLOCK = 640
W_SHIFT = SRC_B % LANES
assert all((o - W_SHIFT) % LANES == 0 for o in (SRC_B, SRC_C, SRC_D, SRC_SILU, SRC_MERGE))

VMEM_LIMIT = 56 * 1024 * 1024

LOG2E = math.log2(math.e)

C_QT = 128
C_HALF = 64
C_DIL_MAX = max(d for _, d in C_CONFIGS)
C_BAND_W = 5 * C_QT

D_TQ = 512
D_TK = 512
D_BAND = 1152


def _bucket_thresholds():
    nb = NUM_BUCKETS // 2
    max_exact = nb // 2
    n = np.arange(max_exact, 4 * T5_MAX_DISTANCE, dtype=np.float32)
    large = max_exact + (np.log(n / np.float32(max_exact))
                         / np.float32(math.log(T5_MAX_DISTANCE / max_exact))
                         * np.float32(nb - max_exact)).astype(np.int32)
    large = np.minimum(large, nb - 1)
    return tuple(int(n[np.argmax(large >= max_exact + k)]) for k in range(1, nb - max_exact))


BUCKET_STEPS = _bucket_thresholds()
assert D_BAND - D_TK + 1 >= BUCKET_STEPS[-1]


def _cparams(sem):
    return pltpu.CompilerParams(dimension_semantics=sem, vmem_limit_bytes=VMEM_LIMIT)


def _rmsnorm_kernel(x_ref, w_ref, o_ref):
    x = x_ref[...]
    ms = jnp.mean(x * x, axis=-1, keepdims=True)
    o_ref[...] = (x * lax.rsqrt(ms + EPS) * w_ref[...]).astype(o_ref.dtype)


def _rmsnorm(x, w, tm=512):
    s, d = x.shape
    return pl.pallas_call(
        _rmsnorm_kernel,
        grid=(s // tm,),
        in_specs=[pl.BlockSpec((tm, d), lambda i: (i, 0)),
                  pl.BlockSpec((1, d), lambda i: (0, 0))],
        out_specs=pl.BlockSpec((tm, d), lambda i: (i, 0)),
        out_shape=jax.ShapeDtypeStruct((s, d), jnp.bfloat16),
        compiler_params=_cparams(("parallel",)),
        name="rmsnorm",
    )(x, w)


def _matmul_kernel(a_ref, b_ref, o_ref):
    o_ref[...] = jnp.dot(a_ref[...], b_ref[...], preferred_element_type=jnp.float32)


def _in_proj(h, w_all, layer, n, tm=1024, tn=1792):
    s, d = h.shape
    tn = min(tn, n)
    return pl.pallas_call(
        _matmul_kernel,
        grid=(s // tm, n // tn),
        in_specs=[pl.BlockSpec((tm, d), lambda i, j: (i, 0)),
                  pl.BlockSpec((None, d, tn), lambda i, j: (layer, 0, j))],
        out_specs=pl.BlockSpec((tm, tn), lambda i, j: (i, j)),
        out_shape=jax.ShapeDtypeStruct((s, n), jnp.float32),
        compiler_params=_cparams(("parallel", "arbitrary")),
        name="in_proj",
    )(h, w_all)


def _relayout_kernel(lo_ref, hi_ref, o_ref):
    rows = o_ref.shape[0]
    keep = lax.broadcasted_iota(jnp.int32, (rows, LANES), 1) < LANES - W_SHIFT
    nu = o_ref.shape[1] // LANES
    pieces = [pltpu.roll(lo_ref[:, u * LANES:(u + 1) * LANES], LANES - W_SHIFT, 1)
              for u in range(nu)]
    pieces.append(pltpu.roll(hi_ref[...], LANES - W_SHIFT, 1))
    for u in range(nu):
        o_ref[:, u * LANES:(u + 1) * LANES] = jnp.where(keep, pieces[u], pieces[u + 1]
                                                        ).astype(o_ref.dtype)


def _relayout_src_block(jb):
    seg = lambda dst, src: (src - W_SHIFT - dst) // W_BLK
    return jb + jnp.where(jb < OFF_CQ // W_BLK, seg(OFF_SILU, SRC_SILU),
                          jnp.where(jb < OFF_BQ // W_BLK, seg(OFF_CQ, SRC_C),
                                    jnp.where(jb < OFF_MERGE // W_BLK, seg(OFF_BQ, SRC_B),
                                              seg(OFF_MERGE, SRC_MERGE))))


def _relayout_w_in(w_in):
    depth, d, _ = w_in.shape
    per = W_BLK // LANES
    return pl.pallas_call(
        _relayout_kernel,
        grid=(depth, N_REST // W_BLK),
        in_specs=[pl.BlockSpec((None, d, W_BLK), lambda l, j: (l, 0, _relayout_src_block(j))),
                  pl.BlockSpec((None, d, LANES),
                               lambda l, j: (l, 0, (_relayout_src_block(j) + 1) * per))],
        out_specs=pl.BlockSpec((None, d, W_BLK), lambda l, j: (l, 0, j)),
        out_shape=jax.ShapeDtypeStruct((depth, d, N_REST), jnp.bfloat16),
        compiler_params=_cparams(("parallel", "parallel")),
        name="relayout_w_in",
    )(w_in, w_in)


def _band_kernel(tab_ref, o_ref, *, pad, dil, half, col0, key_axis):
    h = pl.program_id(0)
    rows, width = o_ref.shape
    col = lax.broadcasted_iota(jnp.int32, (rows, width), 1)
    row = lax.broadcasted_iota(jnp.int32, (rows, width), 0) + pl.program_id(1) * rows
    rel_u = (col - row if key_axis == 1 else row - col) - pad
    rel = rel_u * dil
    n = jnp.abs(rel)
    nb = NUM_BUCKETS // 2
    max_exact = nb // 2
    large = jnp.full((rows, width), max_exact, jnp.int32)
    for t in BUCKET_STEPS:
        large = large + jnp.where(n >= t, 1, 0)
    bucket = jnp.where(rel > 0, nb, 0) + jnp.where(n < max_exact, n, large)
    val = jnp.zeros((rows, width), jnp.float32)
    for b in range(NUM_BUCKETS):
        val = jnp.where(bucket == b, tab_ref[b, col0 + h], val)
    val = val * LOG2E
    if half is not None:
        val = jnp.where(jnp.abs(rel_u) <= half, val, NEG)
    o_ref[...] = val


def _band(rel_bias, heads, rows, width, rblk, *, pad, dil, half, col0, key_axis, name):
    return pl.pallas_call(
        functools.partial(_band_kernel, pad=pad, dil=dil, half=half, col0=col0,
                          key_axis=key_axis),
        grid=(heads, rows // rblk),
        in_specs=[pl.BlockSpec(memory_space=pltpu.SMEM)],
        out_specs=pl.BlockSpec((None, rblk, width), lambda h, j: (h, j, 0)),
        out_shape=jax.ShapeDtypeStruct((heads, rows, width), jnp.float32),
        compiler_params=_cparams(("parallel", "arbitrary")),
        name=name,
    )(rel_bias)


def _lane_lo(shape):
    return (lax.broadcasted_iota(jnp.int32, shape, len(shape) - 1) % LANES) < HEAD_DIM


def _rope(x, c, sa, sb):
    return x * c + pltpu.roll(x, LANES - 16, 1) * sa + pltpu.roll(x, 16, 1) * sb


def _halfnorm(x, gain, lo):
    sq = x * x
    s_lo = jnp.sum(jnp.where(lo, sq, 0.0), axis=-1, keepdims=True)
    s_hi = jnp.sum(jnp.where(lo, 0.0, sq), axis=-1, keepdims=True)
    r = jnp.where(lo, lax.rsqrt(s_lo * (1.0 / HEAD_DIM) + EPS),
                  lax.rsqrt(s_hi * (1.0 / HEAD_DIM) + EPS))
    return x * r * gain


ONES_ROWS = 16


def _prep_a_kernel(p_ref, qn_ref, kvn_ref, wuq_ref, wukv_ref, gq_ref, gk_ref,
                   c_ref, sa_ref, sb_ref, qt_out, k_out, vt_out):
    p = p_ref[...]
    cq = p[:, :Q_LORA]
    cq = cq * lax.rsqrt(jnp.mean(cq * cq, axis=-1, keepdims=True) + EPS) * qn_ref[...]
    q = jnp.dot(cq.astype(jnp.bfloat16), wuq_ref[...], preferred_element_type=jnp.float32)
    ckv = p[:, Q_LORA:Q_LORA + KV_LORA]
    ckv = ckv * lax.rsqrt(jnp.mean(ckv * ckv, axis=-1, keepdims=True) + EPS) * kvn_ref[...]
    kvu = jnp.dot(ckv.astype(jnp.bfloat16), wukv_ref[...], preferred_element_type=jnp.float32)
    kr = pltpu.roll(p[:, Q_LORA + KV_LORA:], HEAD_DIM, 1)
    lo = _lane_lo(kr.shape)
    c, sa, sb = c_ref[...], sa_ref[...], sb_ref[...]
    scale = A_QK ** -0.5 * LOG2E
    ones = jnp.ones((ONES_ROWS, p.shape[0]), vt_out.dtype)
    for h in range(A_HEADS):
        qh = q[:, h * LANES:(h + 1) * LANES]
        ss = jnp.sum(qh * qh, axis=-1, keepdims=True) * (1.0 / A_QK)
        qh = qh * lax.rsqrt(ss + EPS) * gq_ref[...]
        qt_out[h] = (_rope(qh, c, sa, sb) * scale).T.astype(qt_out.dtype)
        kvh = kvu[:, h * LANES:(h + 1) * LANES]
        kh = jnp.where(lo, kvh, kr)
        ss = jnp.sum(kh * kh, axis=-1, keepdims=True) * (1.0 / A_QK)
        kh = kh * lax.rsqrt(ss + EPS) * gk_ref[...]
        k_out[h] = _rope(kh, c, sa, sb).astype(k_out.dtype)
        vt_out[h, :A_V, :] = kvh.T[A_NOPE:].astype(vt_out.dtype)
        vt_out[h, A_V:, :] = ones


def _prep_a(proj, qn, kvn, wuq, wukv, gq, gk, tabs, tm=512):
    s = proj.shape[0]
    row = lambda w: pl.BlockSpec((1, w), lambda i: (0, 0))
    tab = pl.BlockSpec((tm, LANES), lambda i: (i, 0))
    return pl.pallas_call(
        _prep_a_kernel,
        grid=(s // tm,),
        in_specs=[pl.BlockSpec((tm, A_BLOCK), lambda i: (i, 0)),
                  row(Q_LORA), row(KV_LORA),
                  pl.BlockSpec(wuq.shape, lambda i: (0, 0)),
                  pl.BlockSpec(wukv.shape, lambda i: (0, 0)),
                  row(LANES), row(LANES), tab, tab, tab],
        out_specs=[pl.BlockSpec((A_HEADS, LANES, tm), lambda i: (0, 0, i)),
                   pl.BlockSpec((A_HEADS, tm, LANES), lambda i: (0, i, 0)),
                   pl.BlockSpec((A_HEADS, A_V + ONES_ROWS, tm), lambda i: (0, 0, i))],
        out_shape=[jax.ShapeDtypeStruct((A_HEADS, LANES, s), jnp.bfloat16),
                   jax.ShapeDtypeStruct((A_HEADS, s, LANES), jnp.bfloat16),
                   jax.ShapeDtypeStruct((A_HEADS, A_V + ONES_ROWS, s), jnp.bfloat16)],
        compiler_params=_cparams(("parallel",)),
        name="prep_a",
    )(proj, qn, kvn, wuq, wukv, gq, gk, *tabs)


B_TQ = 256


def _prep_b_kernel(q_ref, kv_ref, gq_ref, gk_ref, c_ref, sa_ref, sb_ref,
                   qt_out, k_out, vt_out):
    c, sa, sb = c_ref[...], sa_ref[...], sb_ref[...]
    lo = _lane_lo(c.shape)
    tm = c.shape[0]
    scale = HEAD_DIM ** -0.5 * LOG2E
    q = q_ref[...]
    for b in range(B_HEADS // 2):
        x = _rope(_halfnorm(q[:, b * LANES:(b + 1) * LANES], gq_ref[...], lo), c, sa, sb) * scale
        xr = pltpu.roll(x, HEAD_DIM, 1)
        g = b // 2
        even, odd = (x, xr) if g == 0 else (xr, x)
        keep = lo if g == 0 else jnp.logical_not(lo)
        for j, xh in enumerate((even, odd)):
            hh = 2 * (b % 2) + j
            qt_out[g, :, hh * tm:(hh + 1) * tm] = jnp.where(keep, xh, 0.0).T.astype(qt_out.dtype)
    kv = kv_ref[...]
    k_out[...] = _rope(_halfnorm(kv[:, :LANES], gk_ref[...], lo), c, sa, sb).astype(k_out.dtype)
    vt = kv[:, LANES:].T.astype(vt_out.dtype)
    ones = jnp.ones((ONES_ROWS, tm), vt_out.dtype)
    for g in range(B_KV_HEADS):
        vt_out[g, :HEAD_DIM, :] = vt[g * HEAD_DIM:(g + 1) * HEAD_DIM]
        vt_out[g, HEAD_DIM:, :] = ones


def _prep_b(proj, gq, gk, tabs, tm=B_TQ):
    s = proj.shape[0]
    row = pl.BlockSpec((1, LANES), lambda i: (0, 0))
    tab = pl.BlockSpec((tm, LANES), lambda i: (i, 0))
    rv = HEAD_DIM + ONES_ROWS
    return pl.pallas_call(
        _prep_b_kernel,
        grid=(s // tm,),
        in_specs=[pl.BlockSpec((tm, 512), lambda i: (i, OFF_BQ // 512)),
                  pl.BlockSpec((tm, 256), lambda i: (i, OFF_BKV // 256)),
                  row, row, tab, tab, tab],
        out_specs=[pl.BlockSpec((2, None, LANES, 4 * tm), lambda i: (0, i, 0, 0)),
                   pl.BlockSpec((tm, LANES), lambda i: (i, 0)),
                   pl.BlockSpec((2, rv, tm), lambda i: (0, 0, i))],
        out_shape=[jax.ShapeDtypeStruct((2, s // tm, LANES, 4 * tm), jnp.bfloat16),
                   jax.ShapeDtypeStruct((s, LANES), jnp.bfloat16),
                   jax.ShapeDtypeStruct((2, rv, s), jnp.bfloat16)],
        compiler_params=_cparams(("parallel",)),
        name="prep_b",
    )(proj, proj, gq, gk, *tabs)


def _prep_d_kernel(q_ref, k_ref, v_ref, gq_ref, gk_ref, qt_out, k_out, vt_out):
    q, k, v = q_ref[...], k_ref[...], v_ref[...]
    tm = q.shape[0]
    lo = _lane_lo((tm, LANES))
    scale = HEAD_DIM ** -0.5 * LOG2E
    ones = jnp.ones((ONES_ROWS, tm), vt_out.dtype)
    for h in range(D_HEADS):
        cols = slice(h * LANES, (h + 1) * LANES)
        x = _halfnorm(q[:, cols], gq_ref[...], lo) * scale
        qt_out[h, :, :tm] = jnp.where(lo, x, 0.0).T.astype(qt_out.dtype)
        qt_out[h, :, tm:] = jnp.where(lo, 0.0, x).T.astype(qt_out.dtype)
        k_out[:, cols] = _halfnorm(k[:, cols], gk_ref[...], lo).astype(k_out.dtype)
        vt_out[h, :D_V, :] = v[:, cols].T.astype(vt_out.dtype)
        vt_out[h, D_V:, :] = ones


def _prep_d(proj, gq, gk, tm=D_TQ):
    s = proj.shape[0]
    row = pl.BlockSpec((1, LANES), lambda i: (0, 0))
    blk = lambda off: pl.BlockSpec((tm, 512), lambda i: (i, off // 512))
    rv = D_V + ONES_ROWS
    return pl.pallas_call(
        _prep_d_kernel,
        grid=(s // tm,),
        in_specs=[blk(OFF_DQ), blk(OFF_DK), blk(OFF_DV), row, row],
        out_specs=[pl.BlockSpec((D_HEADS, None, LANES, 2 * tm), lambda i: (0, i, 0, 0)),
                   pl.BlockSpec((tm, 512), lambda i: (i, 0)),
                   pl.BlockSpec((D_HEADS, rv, tm), lambda i: (0, 0, i))],
        out_shape=[jax.ShapeDtypeStruct((D_HEADS, s // tm, LANES, 2 * tm), jnp.bfloat16),
                   jax.ShapeDtypeStruct((s, 512), jnp.bfloat16),
                   jax.ShapeDtypeStruct((D_HEADS, rv, s), jnp.bfloat16)],
        compiler_params=_cparams(("parallel",)),
        name="prep_d",
    )(proj, proj, proj, gq, gk)


def _prep_c_kernel(q_ref, k_ref, v_ref, gq_ref, gk_ref, q_out, kt_out, v_out, *, dil, ut):
    lo = _lane_lo((ut, LANES))
    scale = HEAD_DIM ** -0.5 * LOG2E
    for r in range(dil):
        rows = pl.ds(r, ut, stride=dil) if dil > 1 else pl.ds(0, ut)
        q_out[r] = (_halfnorm(q_ref[rows, :], gq_ref[...], lo) * scale).astype(q_out.dtype)
        kt_out[r] = _halfnorm(k_ref[rows, :], gk_ref[...], lo).T.astype(kt_out.dtype)
        v_out[r] = v_ref[rows, :].astype(v_out.dtype)


def _prep_c(proj, gq, gk, g, ut):
    s = proj.shape[0]
    dil = C_CONFIGS[g][1]
    seg = s // dil
    nb = C_HEADS // 2
    tn = ut * dil
    blk = lambda off: pl.BlockSpec((tn, LANES), lambda i, b: (i, off // LANES + g * nb + b))
    row = pl.BlockSpec((1, LANES), lambda i, b: (0, 0))
    return pl.pallas_call(
        functools.partial(_prep_c_kernel, dil=dil, ut=ut),
        grid=(s // tn, nb),
        in_specs=[blk(OFF_CQ), blk(OFF_CK), blk(OFF_CV), row, row],
        out_specs=[pl.BlockSpec((dil, ut, LANES), lambda i, b: (0, i, b)),
                   pl.BlockSpec((dil, LANES, ut), lambda i, b: (0, b, i)),
                   pl.BlockSpec((dil, ut, LANES), lambda i, b: (0, i, b))],
        out_shape=[jax.ShapeDtypeStruct((dil, seg, nb * LANES), jnp.bfloat16),
                   jax.ShapeDtypeStruct((dil, nb * LANES, seg), jnp.bfloat16),
                   jax.ShapeDtypeStruct((dil, seg, nb * LANES), jnp.bfloat16)],
        compiler_params=_cparams(("parallel", "parallel")),
        name=f"prep_c{g}",
    )(proj, proj, proj, gq, gk)


def _scores(k, qt, s_ref, slot, bias=None):
    st = jnp.dot(k, qt, preferred_element_type=jnp.float32)
    s_ref[slot] = st if bias is None else st + bias


def _consume(s_ref, slot, vt, m_ref, acc_ref):
    st = s_ref[slot]
    m_prev = m_ref[...]
    m_next = jnp.maximum(m_prev, jnp.max(st, axis=0, keepdims=True))
    p = jnp.exp2(st - m_next)
    alpha = jnp.exp2(m_prev - m_next)
    m_ref[...] = m_next
    pv = jnp.dot(vt, p.astype(jnp.bfloat16), preferred_element_type=jnp.float32)
    acc_ref[...] = alpha * acc_ref[...] + pv


FLASH_UNROLL = 4


def _flash_loop(nk, score, consume):
    u = min(FLASH_UNROLL, nk)
    assert nk % u == 0 and u % 2 == 0
    score(0, 0)

    def body(j, carry):
        for i in range(u):
            score(j * u + i + 1, (i + 1) % 2)
            consume(j * u + i, i % 2)
        return carry

    lax.fori_loop(0, nk // u - 1, body, 0)
    for c in range(nk - u, nk):
        if c + 1 < nk:
            score(c + 1, (c + 1) % 2)
        consume(c, c % 2)


def _init_stats(m_ref, acc_ref):
    m_ref[...] = jnp.full_like(m_ref, -jnp.inf)
    acc_ref[...] = jnp.zeros_like(acc_ref)


def _normalised(acc_ref, rv):
    acc = acc_ref[...]
    return acc[:rv] / acc[rv:rv + 1]


def _flash_scratch(streams, tk, n, rv):
    return ([pltpu.VMEM((2, tk, n), jnp.float32)] * streams
            + [pltpu.VMEM((1, n), jnp.float32)] * streams
            + [pltpu.VMEM((rv, n), jnp.float32)] * streams)


def _attn_a_kernel(qt_ref, k_ref, vt_ref, o_ref, s0_ref, s1_ref, m0_ref, m1_ref,
                   acc0_ref, acc1_ref, *, tk, nk):
    streams = ((s0_ref, m0_ref, acc0_ref), (s1_ref, m1_ref, acc1_ref))
    for _, m_ref, acc_ref in streams:
        _init_stats(m_ref, acc_ref)

    def score(c, slot):
        k0 = pl.multiple_of(c * tk, tk)
        for hh, (s_ref, _, _) in enumerate(streams):
            _scores(k_ref[hh, pl.ds(k0, tk), :], qt_ref[hh], s_ref, slot)

    def consume(c, slot):
        k0 = pl.multiple_of(c * tk, tk)
        for hh, (s_ref, m_ref, acc_ref) in enumerate(streams):
            _consume(s_ref, slot, vt_ref[hh, :, pl.ds(k0, tk)], m_ref, acc_ref)

    _flash_loop(nk, score, consume)
    o_ref[...] = jnp.concatenate([_normalised(acc0_ref, A_V), _normalised(acc1_ref, A_V)],
                                 axis=0).T


def _attn_a(qt, k, vt, tq=512, tk=512):
    s = k.shape[1]
    tk = min(tk, s // 2)
    rv = vt.shape[1]
    return pl.pallas_call(
        functools.partial(_attn_a_kernel, tk=tk, nk=s // tk),
        grid=(A_HEADS // 2, s // tq),
        in_specs=[pl.BlockSpec((2, LANES, tq), lambda p, i: (p, 0, i)),
                  pl.BlockSpec((2, s, LANES), lambda p, i: (p, 0, 0)),
                  pl.BlockSpec((2, rv, s), lambda p, i: (p, 0, 0))],
        out_specs=pl.BlockSpec((tq, LANES), lambda p, i: (i, p)),
        out_shape=jax.ShapeDtypeStruct((s, BRANCH_W), jnp.float32),
        scratch_shapes=_flash_scratch(2, tk, tq, rv),
        compiler_params=_cparams(("parallel", "arbitrary")),
        name="attn_a",
    )(qt, k, vt)


def _attn_b_kernel(qt_ref, k_ref, vt_ref, o_ref, s_ref, m_ref, acc_ref, *, tq, tk, nk):
    _init_stats(m_ref, acc_ref)

    def score(c, slot):
        _scores(k_ref[pl.ds(pl.multiple_of(c * tk, tk), tk), :], qt_ref[...], s_ref, slot)

    def consume(c, slot):
        _consume(s_ref, slot, vt_ref[:, pl.ds(pl.multiple_of(c * tk, tk), tk)], m_ref, acc_ref)

    _flash_loop(nk, score, consume)
    ot = _normalised(acc_ref, HEAD_DIM)
    o_ref[...] = jnp.concatenate([ot[:, h * tq:(h + 1) * tq] for h in range(4)], axis=0).T


def _attn_b(qt, k, vt, tq=B_TQ, tk=512):
    s = k.shape[0]
    tk = min(tk, s // 2)
    rv = vt.shape[1]
    return pl.pallas_call(
        functools.partial(_attn_b_kernel, tq=tq, tk=tk, nk=s // tk),
        grid=(2, s // tq),
        in_specs=[pl.BlockSpec((None, None, LANES, 4 * tq), lambda g, i: (g, i, 0, 0)),
                  pl.BlockSpec((s, LANES), lambda g, i: (0, 0)),
                  pl.BlockSpec((None, rv, s), lambda g, i: (g, 0, 0))],
        out_specs=pl.BlockSpec((tq, 2 * LANES), lambda g, i: (i, g)),
        out_shape=jax.ShapeDtypeStruct((s, BRANCH_W), jnp.float32),
        scratch_shapes=_flash_scratch(1, tk, 4 * tq, rv),
        compiler_params=_cparams(("parallel", "arbitrary")),
        name="attn_b",
    )(qt, k, vt)


def _attn_d_kernel(qt_ref, k_ref, vt_ref, band_ref, lam_ref, sub_ref, o_ref,
                   s_ref, m_ref, acc_ref, *, tq, tk, nk, lambda_init):
    _init_stats(m_ref, acc_ref)
    q0 = pl.program_id(1) * tq
    hi = tq + 2 * D_BAND - tk

    def score(c, slot):
        k0 = pl.multiple_of(c * tk, tk)
        b0 = pl.multiple_of(jnp.clip(k0 - q0 + D_BAND, 0, hi), LANES)
        bias = band_ref[pl.ds(b0, tk), :]
        _scores(k_ref[pl.ds(k0, tk), :], qt_ref[...], s_ref, slot,
                jnp.concatenate([bias, bias], axis=1))

    def consume(c, slot):
        _consume(s_ref, slot, vt_ref[:, pl.ds(pl.multiple_of(c * tk, tk), tk)], m_ref, acc_ref)

    _flash_loop(nk, score, consume)
    lv = lam_ref[...]
    lam = (jnp.exp(jnp.sum(lv[0:1] * lv[1:2], axis=-1, keepdims=True))
           - jnp.exp(jnp.sum(lv[2:3] * lv[3:4], axis=-1, keepdims=True)) + lambda_init)
    ot = _normalised(acc_ref, D_V)
    o = (ot[:, :tq] - lam * ot[:, tq:]).T
    o = o * lax.rsqrt(jnp.mean(o * o, axis=-1, keepdims=True) + EPS) * sub_ref[...]
    o_ref[...] = o * (1.0 - lambda_init)


def _attn_d(qt, k, vt, band, lam_vecs, subw, lambda_init, tq=D_TQ, tk=D_TK):
    s = k.shape[0]
    rv = vt.shape[1]
    return pl.pallas_call(
        functools.partial(_attn_d_kernel, tq=tq, tk=tk, nk=s // tk, lambda_init=lambda_init),
        grid=(D_HEADS, s // tq),
        in_specs=[pl.BlockSpec((None, None, LANES, 2 * tq), lambda h, i: (h, i, 0, 0)),
                  pl.BlockSpec((s, LANES), lambda h, i: (0, h)),
                  pl.BlockSpec((None, rv, s), lambda h, i: (h, 0, 0)),
                  pl.BlockSpec((None, tq + 2 * D_BAND, tq), lambda h, i: (h, 0, 0)),
                  pl.BlockSpec((4, HEAD_DIM), lambda h, i: (0, 0)),
                  pl.BlockSpec((1, LANES), lambda h, i: (0, 0))],
        out_specs=pl.BlockSpec((tq, LANES), lambda h, i: (i, h)),
        out_shape=jax.ShapeDtypeStruct((s, BRANCH_W), jnp.float32),
        scratch_shapes=_flash_scratch(1, tk, 2 * tq, rv),
        compiler_params=_cparams(("parallel", "arbitrary")),
        name="attn_d",
    )(qt, k, vt, band, lam_vecs, subw)


def _attn_c_kernel(q_ref, kt_ref, v_ref, band_ref, o_ref, lse_ref, *, nt, seg, win, unroll):
    j = pl.program_id(2)
    lo = _lane_lo((C_QT, LANES))

    def scores(t):
        r0 = pl.multiple_of(t * C_QT, C_QT)
        u0 = j * (nt * C_QT) + r0
        ws = pl.multiple_of(jnp.clip(u0 - C_QT, 0, seg - win), C_QT)
        x = pl.multiple_of(ws - u0 + 2 * C_QT, C_QT)
        q = q_ref[pl.ds(r0, C_QT), :]
        zero = jnp.zeros_like(q)
        q2 = jnp.concatenate([jnp.where(lo, q, zero), jnp.where(lo, zero, q)], axis=0)
        s = jnp.dot(q2, kt_ref[:, pl.ds(ws, win)], preferred_element_type=jnp.float32)
        s = s + jnp.concatenate([band_ref[0, :, pl.ds(x, win)], band_ref[1, :, pl.ds(x, win)]], axis=0)
        return r0, ws, s

    def finish(r0, ws, s):
        m = jnp.max(s, axis=1, keepdims=True)
        p = jnp.exp2(s - m)
        l = jnp.sum(p, axis=1, keepdims=True)
        pv = jnp.dot(p.astype(jnp.bfloat16), v_ref[pl.ds(ws, win), :],
                     preferred_element_type=jnp.float32)
        o = pv / l
        lse = m + jnp.log2(l)
        o_ref[pl.ds(r0, C_QT), :] = jnp.where(lo, o[:C_QT], o[C_QT:])
        lse_ref[pl.ds(r0, C_QT), :] = jnp.where(lo, lse[:C_QT], lse[C_QT:])

    def body(tt, carry):
        tiles = [scores(tt * unroll + i) for i in range(unroll)]
        for tile in tiles:
            finish(*tile)
        return carry

    lax.fori_loop(0, nt // unroll, body, 0)


def _attn_c(q, kt, v, band, nt_max=8):
    dil, seg, w = q.shape
    nt = min(nt_max, seg // C_QT)
    win = min(3 * C_QT, seg)
    tqb = nt * C_QT
    out = jax.ShapeDtypeStruct((dil, seg, w), jnp.float32)
    ospec = pl.BlockSpec((None, tqb, LANES), lambda r, b, j: (r, j, b))
    return pl.pallas_call(
        functools.partial(_attn_c_kernel, nt=nt, seg=seg, win=win, unroll=min(4, nt)),
        grid=(dil, C_HEADS // 2, seg // tqb),
        in_specs=[pl.BlockSpec((None, tqb, LANES), lambda r, b, j: (r, j, b)),
                  pl.BlockSpec((None, LANES, seg), lambda r, b, j: (r, b, 0)),
                  pl.BlockSpec((None, seg, LANES), lambda r, b, j: (r, 0, b)),
                  pl.BlockSpec((2, C_QT, C_BAND_W), lambda r, b, j: (b, 0, 0))],
        out_specs=[ospec, ospec],
        out_shape=[out, out],
        compiler_params=_cparams(("parallel", "parallel", "arbitrary")),
        name=f"attn_c_d{dil}",
    )(q, kt, v, band)


def _combine_c_kernel(*refs):
    o_refs, l_refs, out_ref = refs[0:2 * C_GROUPS:2], refs[1:2 * C_GROUPS:2], refs[-1]
    n = out_ref.shape[0] // C_DIL_MAX
    for r in range(C_DIL_MAX):
        outs, lses = [], []
        for (_, dil), o_ref, l_ref in zip(C_CONFIGS, o_refs, l_refs):
            rows = pl.ds(r // dil, n, stride=C_DIL_MAX // dil)
            cls = pl.ds(r % dil, 1)
            outs.append(o_ref[cls, rows, :][0])
            lses.append(l_ref[cls, rows, :][0])
        m = functools.reduce(jnp.maximum, lses)
        es = [jnp.exp2(l - m) for l in lses]
        num = sum(e * o for e, o in zip(es, outs))
        out_ref[pl.ds(r, n, stride=C_DIL_MAX), :] = num / sum(es)


def _combine_c(os, lses, s, tn=2048):
    nb = C_HEADS // 2
    args, in_specs = [], []
    for (_, dil), o, l in zip(C_CONFIGS, os, lses):
        spec = pl.BlockSpec((dil, tn // dil, LANES), lambda i, b: (0, i, b))
        args += [o, l]
        in_specs += [spec, spec]
    return pl.pallas_call(
        _combine_c_kernel,
        grid=(s // tn, nb),
        in_specs=in_specs,
        out_specs=pl.BlockSpec((tn, LANES), lambda i, b: (i, b)),
        out_shape=jax.ShapeDtypeStruct((s, nb * LANES), jnp.float32),
        compiler_params=_cparams(("parallel", "parallel")),
        name="combine_c",
    )(*args)


def _merge_kernel(x_ref, h_ref, sg_ref, ya_ref, yb_ref, yc_ref, yd_ref,
                  wm0_ref, wm1_ref, wm2_ref, wm3_ref, wb_ref, wo_ref, o_ref, yg_ref):
    c = pl.program_id(1)

    @pl.when(c == 0)
    def _():
        sg = sg_ref[...]
        for n, y_ref in enumerate((ya_ref, yb_ref, yc_ref, yd_ref)):
            g = sg[:, n * BRANCH_W:(n + 1) * BRANCH_W]
            yg_ref[n] = (y_ref[...] * (g * jax.nn.sigmoid(g))).astype(yg_ref.dtype)
        o_ref[...] = x_ref[...]

    h = h_ref[...]
    mixed = 0.0
    for n, wm_ref in enumerate((wm0_ref, wm1_ref, wm2_ref, wm3_ref)):
        gate = jax.nn.sigmoid(jnp.dot(h, wm_ref[...], preferred_element_type=jnp.float32))
        z = jnp.dot(yg_ref[n], wb_ref[n], preferred_element_type=jnp.float32)
        mixed = mixed + gate * z
    o_ref[...] += jnp.dot(mixed.astype(jnp.bfloat16), wo_ref[...],
                          preferred_element_type=jnp.float32)


def _merge(x, h, proj, ys, wmerge_all, wbranch_all, wout_all, layer, tm=512, tn=256):
    s, d = x.shape
    nc = d // tn
    row = lambda w: pl.BlockSpec((tm, w), lambda i, c: (i, 0))
    wm = lambda n: pl.BlockSpec((None, d, tn),
                                lambda i, c: (layer, 0, OFF_MERGE // tn + n * nc + c))
    return pl.pallas_call(
        _merge_kernel,
        grid=(s // tm, nc),
        in_specs=[row(d), row(d),
                  pl.BlockSpec((tm, N_BRANCH * BRANCH_W), lambda i, c: (i, OFF_SILU)),
                  row(BRANCH_W), row(BRANCH_W), row(BRANCH_W), row(BRANCH_W),
                  wm(0), wm(1), wm(2), wm(3),
                  pl.BlockSpec((None, N_BRANCH, BRANCH_W, tn), lambda i, c: (layer, 0, 0, c)),
                  pl.BlockSpec((None, tn, d), lambda i, c: (layer, c, 0))],
        out_specs=row(d),
        out_shape=jax.ShapeDtypeStruct((s, d), jnp.float32),
        scratch_shapes=[pltpu.VMEM((N_BRANCH, tm, BRANCH_W), jnp.bfloat16)],
        compiler_params=_cparams(("parallel", "arbitrary")),
        name="merge",
    )(x, h, proj, *ys, wmerge_all, wmerge_all, wmerge_all, wmerge_all, wbranch_all, wout_all)


def _rope_cos_sin(pos, dim):
    inv = ROPE_THETA ** (-jnp.arange(0, dim, 2, dtype=jnp.float32) / dim)
    ang = pos.astype(jnp.float32)[:, None] * inv[None, :]
    return jnp.cos(ang), jnp.sin(ang)


def _tables_a(s):
    cos, sin = _rope_cos_sin(jnp.arange(s, dtype=jnp.int32), A_ROPE)
    z16, z32 = jnp.zeros((s, 16), jnp.float32), jnp.zeros((s, 32), jnp.float32)
    one = jnp.ones((s, A_NOPE), jnp.float32)
    zero = jnp.zeros((s, A_NOPE), jnp.float32)
    c = jnp.concatenate([one, cos, cos, z32], axis=1)
    sa = jnp.concatenate([zero, -sin, z16, z32], axis=1)
    sb = jnp.concatenate([zero, z16, sin, z32], axis=1)
    return c, sa, sb


def _tables_b(s):
    rows = s // GRID_W
    row_pos = jnp.repeat(jnp.arange(rows, dtype=jnp.int32), GRID_W)
    col_pos = jnp.tile(jnp.arange(GRID_W, dtype=jnp.int32), rows)
    cr, sr = _rope_cos_sin(row_pos, HEAD_DIM // 2)
    cc, sc = _rope_cos_sin(col_pos, HEAD_DIM // 2)
    z = jnp.zeros_like(sr)
    c = jnp.concatenate([cr, cr, cc, cc] * 2, axis=1)
    sa = jnp.concatenate([-sr, z, -sc, z] * 2, axis=1)
    sb = jnp.concatenate([z, sr, z, sc] * 2, axis=1)
    return c, sa, sb


def _pad_lanes(v, width=LANES):
    return jnp.pad(v, ((0, width - v.shape[0]),))[None, :]


def kernel(x, norm_w, w_in, mla_q_norm, mla_kv_norm, mla_w_uq, mla_w_ukv, mla_qk_norm,
           gqa_qk_norm, dil_qk_norm, diff_qk_norm, diff_lambda, diff_subnorm, rel_bias,
           w_branch, w_out):
    b, s, d = x.shape
    assert b == 1 and d == D_MODEL
    depth = norm_w.shape[0]
    bf = jnp.bfloat16

    w_rest = _relayout_w_in(w_in)
    w_a = jnp.pad(w_in[:, :, SRC_A:SRC_B], ((0, 0), (0, 0), (0, A_BLOCK - SRC_B))).astype(bf)
    w_branch_b = w_branch.astype(bf)
    w_out_b = w_out.astype(bf)
    wuq = mla_w_uq.reshape(depth, Q_LORA, A_HEADS, A_QK)
    wuq = jnp.pad(wuq, ((0, 0), (0, 0), (0, 0), (0, LANES - A_QK)))
    wuq = wuq.reshape(depth, Q_LORA, A_HEADS * LANES).astype(bf)
    wukv = mla_w_ukv.astype(bf)

    tabs_a = _tables_a(s)
    tabs_b = _tables_b(s)
    bands_c = [_band(rel_bias, C_HEADS, C_QT, C_BAND_W, C_QT, pad=2 * C_QT, dil=dil,
                     half=C_HALF, col0=g * C_HEADS, key_axis=1, name=f"band_c{g}")
               for g, (_, dil) in enumerate(C_CONFIGS)]
    band_r = D_TQ + 2 * D_BAND
    band_d = _band(rel_bias, D_HEADS, band_r, D_TQ, band_r // 2, pad=D_BAND, dil=1, half=None,
                   col0=C_GROUPS * C_HEADS, key_axis=0, name="band_d")
    c_ut = [max(C_QT, 512 // dil) for _, dil in C_CONFIGS]

    xs = x[0]
    for l in range(depth):
        h = _rmsnorm(xs, norm_w[l][None, :])
        proj = _in_proj(h, w_rest, l, N_MAIN)
        proj_a = _in_proj(h, w_a, l, A_BLOCK)

        qa, kta, va = _prep_a(proj_a, mla_q_norm[l][None, :], mla_kv_norm[l][None, :],
                              wuq[l], wukv[l], _pad_lanes(mla_qk_norm[l, 0]),
                              _pad_lanes(mla_qk_norm[l, 1]), tabs_a)
        y_a = _attn_a(qa, kta, va)

        g2 = lambda v: jnp.tile(v, 2)[None, :]
        qb, ktb, vb = _prep_b(proj, g2(gqa_qk_norm[l, 0]), g2(gqa_qk_norm[l, 1]), tabs_b)
        y_b = _attn_b(qb, ktb, vb)

        os, lses = [], []
        for g in range(C_GROUPS):
            qg, ktg, vg = _prep_c(proj, g2(dil_qk_norm[l, 0, g]), g2(dil_qk_norm[l, 1, g]),
                                  g, c_ut[g])
            og, lg = _attn_c(qg, ktg, vg, bands_c[g])
            os.append(og)
            lses.append(lg)
        y_c = _combine_c(os, lses, s)

        qd, ktd, vd = _prep_d(proj, g2(diff_qk_norm[l, 0]), g2(diff_qk_norm[l, 1]))
        lambda_init = 0.8 - 0.6 * math.exp(-0.3 * l)
        y_d = _attn_d(qd, ktd, vd, band_d, diff_lambda[l], diff_subnorm[l][None, :], lambda_init)

        xs = _merge(xs, h, proj, (y_a, y_b, y_c, y_d), w_rest, w_branch_b, w_out_b, l)
    return xs[None]
```

```python
import functools
import math

import numpy as np
import jax
import jax.numpy as jnp
from jax import lax
from jax.experimental import pallas as pl
from jax.experimental.pallas import tpu as pltpu

D_MODEL = 2048
GRID_W = 64
HEAD_DIM = 64
BRANCH_W = 512
N_BRANCH = 4
ROPE_THETA = 10000.0
EPS = 1e-6
NEG = -1e30

A_HEADS = 8
A_NOPE = 64
A_ROPE = 32
A_V = 64
A_QK = A_NOPE + A_ROPE
Q_LORA = 384
KV_LORA = 128

B_HEADS = 8
B_KV_HEADS = 2

C_HEADS = 8
C_CONFIGS = ((128, 1), (512, 4), (2048, 16))
C_GROUPS = len(C_CONFIGS)

D_HEADS = 4
D_V = 2 * HEAD_DIM

NUM_BUCKETS = 32
T5_MAX_DISTANCE = 1024

LANES = 128

SRC_A, SRC_B, SRC_C, SRC_D, SRC_SILU, SRC_MERGE, SRC_END = 0, 544, 1312, 5920, 7456, 9504, 17696

OFF_SILU = 0
OFF_CQ, OFF_CK, OFF_CV = 2048, 3584, 5120
OFF_DQ, OFF_DK, OFF_DV = 6656, 7168, 7680
OFF_BQ, OFF_BKV = 8192, 8704
N_MAIN = 8960
A_BLOCK = 640

VMEM_LIMIT = 56 * 1024 * 1024

LOG2E = math.log2(math.e)

C_QT = 128
C_HALF = 64
C_DIL_MAX = max(d for _, d in C_CONFIGS)
C_BAND_W = 5 * C_QT

D_TQ = 512
D_TK = 512
D_BAND = 1152


def _bucket_thresholds():
    nb = NUM_BUCKETS // 2
    max_exact = nb // 2
    n = np.arange(max_exact, 4 * T5_MAX_DISTANCE, dtype=np.float32)
    large = max_exact + (np.log(n / np.float32(max_exact))
                         / np.float32(math.log(T5_MAX_DISTANCE / max_exact))
                         * np.float32(nb - max_exact)).astype(np.int32)
    large = np.minimum(large, nb - 1)
    return tuple(int(n[np.argmax(large >= max_exact + k)]) for k in range(1, nb - max_exact))


BUCKET_STEPS = _bucket_thresholds()
assert D_BAND - D_TK + 1 >= BUCKET_STEPS[-1]


def _cparams(sem):
    return pltpu.CompilerParams(dimension_semantics=sem, vmem_limit_bytes=VMEM_LIMIT)


def _rmsnorm_kernel(x_ref, w_ref, o_ref):
    x = x_ref[...]
    ms = jnp.mean(x * x, axis=-1, keepdims=True)
    o_ref[...] = (x * lax.rsqrt(ms + EPS) * w_ref[...]).astype(o_ref.dtype)


def _rmsnorm(x, w, tm=512):
    s, d = x.shape
    return pl.pallas_call(
        _rmsnorm_kernel,
        grid=(s // tm,),
        in_specs=[pl.BlockSpec((tm, d), lambda i: (i, 0)),
                  pl.BlockSpec((1, d), lambda i: (0, 0))],
        out_specs=pl.BlockSpec((tm, d), lambda i: (i, 0)),
        out_shape=jax.ShapeDtypeStruct((s, d), jnp.bfloat16),
        compiler_params=_cparams(("parallel",)),
        name="rmsnorm",
    )(x, w)


def _dot_nt(a, bt):
    return lax.dot_general(a, bt, (((1,), (1,)), ((), ())), preferred_element_type=jnp.float32)


def _matmul_nt_kernel(a_ref, bt_ref, o_ref):
    o_ref[...] = _dot_nt(a_ref[...], bt_ref[...])


def _in_proj(h, wt_all, layer, tm=1024, tn=1792):
    s, d = h.shape
    n = wt_all.shape[1]
    tn = min(tn, n)
    return pl.pallas_call(
        _matmul_nt_kernel,
        grid=(s // tm, n // tn),
        in_specs=[pl.BlockSpec((tm, d), lambda i, j: (i, 0)),
                  pl.BlockSpec((None, tn, d), lambda i, j: (layer, j, 0))],
        out_specs=pl.BlockSpec((tm, tn), lambda i, j: (i, j)),
        out_shape=jax.ShapeDtypeStruct((s, n), jnp.float32),
        compiler_params=_cparams(("parallel", "arbitrary")),
        name="in_proj",
    )(h, wt_all)


def _band_kernel(tab_ref, o_ref, *, pad, dil, half, col0, key_axis):
    h = pl.program_id(0)
    rows, width = o_ref.shape
    col = lax.broadcasted_iota(jnp.int32, (rows, width), 1)
    row = lax.broadcasted_iota(jnp.int32, (rows, width), 0) + pl.program_id(1) * rows
    rel_u = (col - row if key_axis == 1 else row - col) - pad
    rel = rel_u * dil
    n = jnp.abs(rel)
    nb = NUM_BUCKETS // 2
    max_exact = nb // 2
    large = jnp.full((rows, width), max_exact, jnp.int32)
    for t in BUCKET_STEPS:
        large = large + jnp.where(n >= t, 1, 0)
    bucket = jnp.where(rel > 0, nb, 0) + jnp.where(n < max_exact, n, large)
    val = jnp.zeros((rows, width), jnp.float32)
    for b in range(NUM_BUCKETS):
        val = jnp.where(bucket == b, tab_ref[b, col0 + h], val)
    val = val * LOG2E
    if half is not None:
        val = jnp.where(jnp.abs(rel_u) <= half, val, NEG)
    o_ref[...] = val


def _band(rel_bias, heads, rows, width, rblk, *, pad, dil, half, col0, key_axis, name):
    return pl.pallas_call(
        functools.partial(_band_kernel, pad=pad, dil=dil, half=half, col0=col0,
                          key_axis=key_axis),
        grid=(heads, rows // rblk),
        in_specs=[pl.BlockSpec(memory_space=pltpu.SMEM)],
        out_specs=pl.BlockSpec((None, rblk, width), lambda h, j: (h, j, 0)),
        out_shape=jax.ShapeDtypeStruct((heads, rows, width), jnp.float32),
        compiler_params=_cparams(("parallel", "arbitrary")),
        name=name,
    )(rel_bias)


def _lane_lo(shape):
    return (lax.broadcasted_iota(jnp.int32, shape, len(shape) - 1) % LANES) < HEAD_DIM


def _rope(x, c, sa, sb):
    return x * c + pltpu.roll(x, LANES - 16, 1) * sa + pltpu.roll(x, 16, 1) * sb


def _halfnorm(x, gain, lo):
    sq = x * x
    s_lo = jnp.sum(jnp.where(lo, sq, 0.0), axis=-1, keepdims=True)
    s_hi = jnp.sum(jnp.where(lo, 0.0, sq), axis=-1, keepdims=True)
    r = jnp.where(lo, lax.rsqrt(s_lo * (1.0 / HEAD_DIM) + EPS),
                  lax.rsqrt(s_hi * (1.0 / HEAD_DIM) + EPS))
    return x * r * gain


ONES_ROWS = 16


def _prep_a_kernel(p_ref, qn_ref, kvn_ref, wuq_ref, wukv_ref, gq_ref, gk_ref,
                   c_ref, sa_ref, sb_ref, qt_out, k_out, vt_out):
    p = p_ref[...]
    cq = p[:, :Q_LORA]
    cq = cq * lax.rsqrt(jnp.mean(cq * cq, axis=-1, keepdims=True) + EPS) * qn_ref[...]
    q = jnp.dot(cq.astype(jnp.bfloat16), wuq_ref[...], preferred_element_type=jnp.float32)
    ckv = p[:, Q_LORA:Q_LORA + KV_LORA]
    ckv = ckv * lax.rsqrt(jnp.mean(ckv * ckv, axis=-1, keepdims=True) + EPS) * kvn_ref[...]
    kvu = jnp.dot(ckv.astype(jnp.bfloat16), wukv_ref[...], preferred_element_type=jnp.float32)
    kr = pltpu.roll(p[:, Q_LORA + KV_LORA:], HEAD_DIM, 1)
    lo = _lane_lo(kr.shape)
    c, sa, sb = c_ref[...], sa_ref[...], sb_ref[...]
    scale = A_QK ** -0.5 * LOG2E
    ones = jnp.ones((ONES_ROWS, p.shape[0]), vt_out.dtype)
    for h in range(A_HEADS):
        qh = q[:, h * LANES:(h + 1) * LANES]
        ss = jnp.sum(qh * qh, axis=-1, keepdims=True) * (1.0 / A_QK)
        qh = qh * lax.rsqrt(ss + EPS) * gq_ref[...]
        qt_out[h] = (_rope(qh, c, sa, sb) * scale).T.astype(qt_out.dtype)
        kvh = kvu[:, h * LANES:(h + 1) * LANES]
        kh = jnp.where(lo, kvh, kr)
        ss = jnp.sum(kh * kh, axis=-1, keepdims=True) * (1.0 / A_QK)
        kh = kh * lax.rsqrt(ss + EPS) * gk_ref[...]
        k_out[h] = _rope(kh, c, sa, sb).astype(k_out.dtype)
        vt_out[h, :A_V, :] = kvh.T[A_NOPE:].astype(vt_out.dtype)
        vt_out[h, A_V:, :] = ones


def _prep_a(proj, qn, kvn, wuq, wukv, gq, gk, tabs, tm=512):
    s = proj.shape[0]
    row = lambda w: pl.BlockSpec((1, w), lambda i: (0, 0))
    tab = pl.BlockSpec((tm, LANES), lambda i: (i, 0))
    return pl.pallas_call(
        _prep_a_kernel,
        grid=(s // tm,),
        in_specs=[pl.BlockSpec((tm, A_BLOCK), lambda i: (i, 0)),
                  row(Q_LORA), row(KV_LORA),
                  pl.BlockSpec(wuq.shape, lambda i: (0, 0)),
                  pl.BlockSpec(wukv.shape, lambda i: (0, 0)),
                  row(LANES), row(LANES), tab, tab, tab],
        out_specs=[pl.BlockSpec((A_HEADS, LANES, tm), lambda i: (0, 0, i)),
                   pl.BlockSpec((A_HEADS, tm, LANES), lambda i: (0, i, 0)),
                   pl.BlockSpec((A_HEADS, A_V + ONES_ROWS, tm), lambda i: (0, 0, i))],
        out_shape=[jax.ShapeDtypeStruct((A_HEADS, LANES, s), jnp.bfloat16),
                   jax.ShapeDtypeStruct((A_HEADS, s, LANES), jnp.bfloat16),
                   jax.ShapeDtypeStruct((A_HEADS, A_V + ONES_ROWS, s), jnp.bfloat16)],
        compiler_params=_cparams(("parallel",)),
        name="prep_a",
    )(proj, qn, kvn, wuq, wukv, gq, gk, *tabs)


B_TQ = 256


def _prep_b_kernel(q_ref, kv_ref, gq_ref, gk_ref, c_ref, sa_ref, sb_ref,
                   qt_out, k_out, vt_out):
    c, sa, sb = c_ref[...], sa_ref[...], sb_ref[...]
    lo = _lane_lo(c.shape)
    tm = c.shape[0]
    scale = HEAD_DIM ** -0.5 * LOG2E
    q = q_ref[...]
    for b in range(B_HEADS // 2):
        x = _rope(_halfnorm(q[:, b * LANES:(b + 1) * LANES], gq_ref[...], lo), c, sa, sb) * scale
        xr = pltpu.roll(x, HEAD_DIM, 1)
        g = b // 2
        even, odd = (x, xr) if g == 0 else (xr, x)
        keep = lo if g == 0 else jnp.logical_not(lo)
        for j, xh in enumerate((even, odd)):
            hh = 2 * (b % 2) + j
            qt_out[g, :, hh * tm:(hh + 1) * tm] = jnp.where(keep, xh, 0.0).T.astype(qt_out.dtype)
    kv = kv_ref[...]
    k_out[...] = _rope(_halfnorm(kv[:, :LANES], gk_ref[...], lo), c, sa, sb).astype(k_out.dtype)
    vt = kv[:, LANES:].T.astype(vt_out.dtype)
    ones = jnp.ones((ONES_ROWS, tm), vt_out.dtype)
    for g in range(B_KV_HEADS):
        vt_out[g, :HEAD_DIM, :] = vt[g * HEAD_DIM:(g + 1) * HEAD_DIM]
        vt_out[g, HEAD_DIM:, :] = ones


def _prep_b(proj, gq, gk, tabs, tm=B_TQ):
    s = proj.shape[0]
    row = pl.BlockSpec((1, LANES), lambda i: (0, 0))
    tab = pl.BlockSpec((tm, LANES), lambda i: (i, 0))
    rv = HEAD_DIM + ONES_ROWS
    return pl.pallas_call(
        _prep_b_kernel,
        grid=(s // tm,),
        in_specs=[pl.BlockSpec((tm, 512), lambda i: (i, OFF_BQ // 512)),
                  pl.BlockSpec((tm, 256), lambda i: (i, OFF_BKV // 256)),
                  row, row, tab, tab, tab],
        out_specs=[pl.BlockSpec((2, None, LANES, 4 * tm), lambda i: (0, i, 0, 0)),
                   pl.BlockSpec((tm, LANES), lambda i: (i, 0)),
                   pl.BlockSpec((2, rv, tm), lambda i: (0, 0, i))],
        out_shape=[jax.ShapeDtypeStruct((2, s // tm, LANES, 4 * tm), jnp.bfloat16),
                   jax.ShapeDtypeStruct((s, LANES), jnp.bfloat16),
                   jax.ShapeDtypeStruct((2, rv, s), jnp.bfloat16)],
        compiler_params=_cparams(("parallel",)),
        name="prep_b",
    )(proj, proj, gq, gk, *tabs)


def _prep_d_kernel(q_ref, k_ref, v_ref, gq_ref, gk_ref, qt_out, k_out, vt_out):
    q, k, v = q_ref[...], k_ref[...], v_ref[...]
    tm = q.shape[0]
    lo = _lane_lo((tm, LANES))
    scale = HEAD_DIM ** -0.5 * LOG2E
    ones = jnp.ones((ONES_ROWS, tm), vt_out.dtype)
    for h in range(D_HEADS):
        cols = slice(h * LANES, (h + 1) * LANES)
        x = _halfnorm(q[:, cols], gq_ref[...], lo) * scale
        qt_out[h, :, :tm] = jnp.where(lo, x, 0.0).T.astype(qt_out.dtype)
        qt_out[h, :, tm:] = jnp.where(lo, 0.0, x).T.astype(qt_out.dtype)
        k_out[:, cols] = _halfnorm(k[:, cols], gk_ref[...], lo).astype(k_out.dtype)
        vt_out[h, :D_V, :] = v[:, cols].T.astype(vt_out.dtype)
        vt_out[h, D_V:, :] = ones


def _prep_d(proj, gq, gk, tm=D_TQ):
    s = proj.shape[0]
    row = pl.BlockSpec((1, LANES), lambda i: (0, 0))
    blk = lambda off: pl.BlockSpec((tm, 512), lambda i: (i, off // 512))
    rv = D_V + ONES_ROWS
    return pl.pallas_call(
        _prep_d_kernel,
        grid=(s // tm,),
        in_specs=[blk(OFF_DQ), blk(OFF_DK), blk(OFF_DV), row, row],
        out_specs=[pl.BlockSpec((D_HEADS, None, LANES, 2 * tm), lambda i: (0, i, 0, 0)),
                   pl.BlockSpec((tm, 512), lambda i: (i, 0)),
                   pl.BlockSpec((D_HEADS, rv, tm), lambda i: (0, 0, i))],
        out_shape=[jax.ShapeDtypeStruct((D_HEADS, s // tm, LANES, 2 * tm), jnp.bfloat16),
                   jax.ShapeDtypeStruct((s, 512), jnp.bfloat16),
                   jax.ShapeDtypeStruct((D_HEADS, rv, s), jnp.bfloat16)],
        compiler_params=_cparams(("parallel",)),
        name="prep_d",
    )(proj, proj, proj, gq, gk)


def _prep_c_kernel(q_ref, k_ref, v_ref, gq_ref, gk_ref, q_out, kt_out, v_out, *, dil, ut):
    lo = _lane_lo((ut, LANES))
    scale = HEAD_DIM ** -0.5 * LOG2E
    for r in range(dil):
        rows = pl.ds(r, ut, stride=dil) if dil > 1 else pl.ds(0, ut)
        q_out[r] = (_halfnorm(q_ref[rows, :], gq_ref[...], lo) * scale).astype(q_out.dtype)
        kt_out[r] = _halfnorm(k_ref[rows, :], gk_ref[...], lo).T.astype(kt_out.dtype)
        v_out[r] = v_ref[rows, :].astype(v_out.dtype)


def _prep_c(proj, gq, gk, g, ut):
    s = proj.shape[0]
    dil = C_CONFIGS[g][1]
    seg = s // dil
    nb = C_HEADS // 2
    tn = ut * dil
    blk = lambda off: pl.BlockSpec((tn, LANES), lambda i, b: (i, off // LANES + g * nb + b))
    row = pl.BlockSpec((1, LANES), lambda i, b: (0, 0))
    return pl.pallas_call(
        functools.partial(_prep_c_kernel, dil=dil, ut=ut),
        grid=(s // tn, nb),
        in_specs=[blk(OFF_CQ), blk(OFF_CK), blk(OFF_CV), row, row],
        out_specs=[pl.BlockSpec((dil, ut, LANES), lambda i, b: (0, i, b)),
                   pl.BlockSpec((dil, LANES, ut), lambda i, b: (0, b, i)),
                   pl.BlockSpec((dil, ut, LANES), lambda i, b: (0, i, b))],
        out_shape=[jax.ShapeDtypeStruct((dil, seg, nb * LANES), jnp.bfloat16),
                   jax.ShapeDtypeStruct((dil, nb * LANES, seg), jnp.bfloat16),
                   jax.ShapeDtypeStruct((dil, seg, nb * LANES), jnp.bfloat16)],
        compiler_params=_cparams(("parallel", "parallel")),
        name=f"prep_c{g}",
    )(proj, proj, proj, gq, gk)


def _scores(k, qt, s_ref, slot, bias=None):
    st = jnp.dot(k, qt, preferred_element_type=jnp.float32)
    s_ref[slot] = st if bias is None else st + bias


def _consume(s_ref, slot, vt, m_ref, acc_ref):
    st = s_ref[slot]
    m_prev = m_ref[...]
    m_next = jnp.maximum(m_prev, jnp.max(st, axis=0, keepdims=True))
    p = jnp.exp2(st - m_next)
    alpha = jnp.exp2(m_prev - m_next)
    m_ref[...] = m_next
    pv = jnp.dot(vt, p.astype(jnp.bfloat16), preferred_element_type=jnp.float32)
    acc_ref[...] = alpha * acc_ref[...] + pv


FLASH_UNROLL = 4


def _flash_loop(nk, score, consume):
    u = min(FLASH_UNROLL, nk)
    assert nk % u == 0 and u % 2 == 0
    score(0, 0)

    def body(j, carry):
        for i in range(u):
            score(j * u + i + 1, (i + 1) % 2)
            consume(j * u + i, i % 2)
        return carry

    lax.fori_loop(0, nk // u - 1, body, 0)
    for c in range(nk - u, nk):
        if c + 1 < nk:
            score(c + 1, (c + 1) % 2)
        consume(c, c % 2)


def _init_stats(m_ref, acc_ref):
    m_ref[...] = jnp.full_like(m_ref, -jnp.inf)
    acc_ref[...] = jnp.zeros_like(acc_ref)


def _normalised(acc_ref, rv):
    acc = acc_ref[...]
    return acc[:rv] / acc[rv:rv + 1]


def _flash_scratch(streams, tk, n, rv):
    return ([pltpu.VMEM((2, tk, n), jnp.float32)] * streams
            + [pltpu.VMEM((1, n), jnp.float32)] * streams
            + [pltpu.VMEM((rv, n), jnp.float32)] * streams)


def _attn_a_kernel(qt_ref, k_ref, vt_ref, o_ref, s0_ref, s1_ref, m0_ref, m1_ref,
                   acc0_ref, acc1_ref, *, tk, nk):
    streams = ((s0_ref, m0_ref, acc0_ref), (s1_ref, m1_ref, acc1_ref))
    for _, m_ref, acc_ref in streams:
        _init_stats(m_ref, acc_ref)

    def score(c, slot):
        k0 = pl.multiple_of(c * tk, tk)
        for hh, (s_ref, _, _) in enumerate(streams):
            _scores(k_ref[hh, pl.ds(k0, tk), :], qt_ref[hh], s_ref, slot)

    def consume(c, slot):
        k0 = pl.multiple_of(c * tk, tk)
        for hh, (s_ref, m_ref, acc_ref) in enumerate(streams):
            _consume(s_ref, slot, vt_ref[hh, :, pl.ds(k0, tk)], m_ref, acc_ref)

    _flash_loop(nk, score, consume)
    o_ref[...] = jnp.concatenate([_normalised(acc0_ref, A_V), _normalised(acc1_ref, A_V)],
                                 axis=0).T


def _attn_a(qt, k, vt, tq=512, tk=512):
    s = k.shape[1]
    tk = min(tk, s // 2)
    rv = vt.shape[1]
    return pl.pallas_call(
        functools.partial(_attn_a_kernel, tk=tk, nk=s // tk),
        grid=(A_HEADS // 2, s // tq),
        in_specs=[pl.BlockSpec((2, LANES, tq), lambda p, i: (p, 0, i)),
                  pl.BlockSpec((2, s, LANES), lambda p, i: (p, 0, 0)),
                  pl.BlockSpec((2, rv, s), lambda p, i: (p, 0, 0))],
        out_specs=pl.BlockSpec((tq, LANES), lambda p, i: (i, p)),
        out_shape=jax.ShapeDtypeStruct((s, BRANCH_W), jnp.float32),
        scratch_shapes=_flash_scratch(2, tk, tq, rv),
        compiler_params=_cparams(("parallel", "arbitrary")),
        name="attn_a",
    )(qt, k, vt)


def _attn_b_kernel(qt_ref, k_ref, vt_ref, o_ref, s_ref, m_ref, acc_ref, *, tq, tk, nk):
    _init_stats(m_ref, acc_ref)

    def score(c, slot):
        _scores(k_ref[pl.ds(pl.multiple_of(c * tk, tk), tk), :], qt_ref[...], s_ref, slot)

    def consume(c, slot):
        _consume(s_ref, slot, vt_ref[:, pl.ds(pl.multiple_of(c * tk, tk), tk)], m_ref, acc_ref)

    _flash_loop(nk, score, consume)
    ot = _normalised(acc_ref, HEAD_DIM)
    o_ref[...] = jnp.concatenate([ot[:, h * tq:(h + 1) * tq] for h in range(4)], axis=0).T


def _attn_b(qt, k, vt, tq=B_TQ, tk=512):
    s = k.shape[0]
    tk = min(tk, s // 2)
    rv = vt.shape[1]
    return pl.pallas_call(
        functools.partial(_attn_b_kernel, tq=tq, tk=tk, nk=s // tk),
        grid=(2, s // tq),
        in_specs=[pl.BlockSpec((None, None, LANES, 4 * tq), lambda g, i: (g, i, 0, 0)),
                  pl.BlockSpec((s, LANES), lambda g, i: (0, 0)),
                  pl.BlockSpec((None, rv, s), lambda g, i: (g, 0, 0))],
        out_specs=pl.BlockSpec((tq, 2 * LANES), lambda g, i: (i, g)),
        out_shape=jax.ShapeDtypeStruct((s, BRANCH_W), jnp.float32),
        scratch_shapes=_flash_scratch(1, tk, 4 * tq, rv),
        compiler_params=_cparams(("parallel", "arbitrary")),
        name="attn_b",
    )(qt, k, vt)


def _attn_d_kernel(qt_ref, k_ref, vt_ref, band_ref, lam_ref, sub_ref, o_ref,
                   s_ref, m_ref, acc_ref, *, tq, tk, nk, lambda_init):
    _init_stats(m_ref, acc_ref)
    q0 = pl.program_id(1) * tq
    hi = tq + 2 * D_BAND - tk

    def score(c, slot):
        k0 = pl.multiple_of(c * tk, tk)
        b0 = pl.multiple_of(jnp.clip(k0 - q0 + D_BAND, 0, hi), LANES)
        bias = band_ref[pl.ds(b0, tk), :]
        _scores(k_ref[pl.ds(k0, tk), :], qt_ref[...], s_ref, slot,
                jnp.concatenate([bias, bias], axis=1))

    def consume(c, slot):
        _consume(s_ref, slot, vt_ref[:, pl.ds(pl.multiple_of(c * tk, tk), tk)], m_ref, acc_ref)

    _flash_loop(nk, score, consume)
    lv = lam_ref[...]
    lam = (jnp.exp(jnp.sum(lv[0:1] * lv[1:2], axis=-1, keepdims=True))
           - jnp.exp(jnp.sum(lv[2:3] * lv[3:4], axis=-1, keepdims=True)) + lambda_init)
    ot = _normalised(acc_ref, D_V)
    o = (ot[:, :tq] - lam * ot[:, tq:]).T
    o = o * lax.rsqrt(jnp.mean(o * o, axis=-1, keepdims=True) + EPS) * sub_ref[...]
    o_ref[...] = o * (1.0 - lambda_init)


def _attn_d(qt, k, vt, band, lam_vecs, subw, lambda_init, tq=D_TQ, tk=D_TK):
    s = k.shape[0]
    rv = vt.shape[1]
    return pl.pallas_call(
        functools.partial(_attn_d_kernel, tq=tq, tk=tk, nk=s // tk, lambda_init=lambda_init),
        grid=(D_HEADS, s // tq),
        in_specs=[pl.BlockSpec((None, None, LANES, 2 * tq), lambda h, i: (h, i, 0, 0)),
                  pl.BlockSpec((s, LANES), lambda h, i: (0, h)),
                  pl.BlockSpec((None, rv, s), lambda h, i: (h, 0, 0)),
                  pl.BlockSpec((None, tq + 2 * D_BAND, tq), lambda h, i: (h, 0, 0)),
                  pl.BlockSpec((4, HEAD_DIM), lambda h, i: (0, 0)),
                  pl.BlockSpec((1, LANES), lambda h, i: (0, 0))],
        out_specs=pl.BlockSpec((tq, LANES), lambda h, i: (i, h)),
        out_shape=jax.ShapeDtypeStruct((s, BRANCH_W), jnp.float32),
        scratch_shapes=_flash_scratch(1, tk, 2 * tq, rv),
        compiler_params=_cparams(("parallel", "arbitrary")),
        name="attn_d",
    )(qt, k, vt, band, lam_vecs, subw)


def _attn_c_kernel(q_ref, kt_ref, v_ref, band_ref, o_ref, lse_ref, *, nt, seg, win, unroll):
    j = pl.program_id(2)
    lo = _lane_lo((C_QT, LANES))

    def scores(t):
        r0 = pl.multiple_of(t * C_QT, C_QT)
        u0 = j * (nt * C_QT) + r0
        ws = pl.multiple_of(jnp.clip(u0 - C_QT, 0, seg - win), C_QT)
        x = pl.multiple_of(ws - u0 + 2 * C_QT, C_QT)
        q = q_ref[pl.ds(r0, C_QT), :]
        zero = jnp.zeros_like(q)
        q2 = jnp.concatenate([jnp.where(lo, q, zero), jnp.where(lo, zero, q)], axis=0)
        s = jnp.dot(q2, kt_ref[:, pl.ds(ws, win)], preferred_element_type=jnp.float32)
        s = s + jnp.concatenate([band_ref[0, :, pl.ds(x, win)], band_ref[1, :, pl.ds(x, win)]], axis=0)
        return r0, ws, s

    def finish(r0, ws, s):
        m = jnp.max(s, axis=1, keepdims=True)
        p = jnp.exp2(s - m)
        l = jnp.sum(p, axis=1, keepdims=True)
        pv = jnp.dot(p.astype(jnp.bfloat16), v_ref[pl.ds(ws, win), :],
                     preferred_element_type=jnp.float32)
        o = pv / l
        lse = m + jnp.log2(l)
        o_ref[pl.ds(r0, C_QT), :] = jnp.where(lo, o[:C_QT], o[C_QT:])
        lse_ref[pl.ds(r0, C_QT), :] = jnp.where(lo, lse[:C_QT], lse[C_QT:])

    def body(tt, carry):
        tiles = [scores(tt * unroll + i) for i in range(unroll)]
        for tile in tiles:
            finish(*tile)
        return carry

    lax.fori_loop(0, nt // unroll, body, 0)


def _attn_c(q, kt, v, band, nt_max=8):
    dil, seg, w = q.shape
    nt = min(nt_max, seg // C_QT)
    win = min(3 * C_QT, seg)
    tqb = nt * C_QT
    out = jax.ShapeDtypeStruct((dil, seg, w), jnp.float32)
    ospec = pl.BlockSpec((None, tqb, LANES), lambda r, b, j: (r, j, b))
    return pl.pallas_call(
        functools.partial(_attn_c_kernel, nt=nt, seg=seg, win=win, unroll=min(4, nt)),
        grid=(dil, C_HEADS // 2, seg // tqb),
        in_specs=[pl.BlockSpec((None, tqb, LANES), lambda r, b, j: (r, j, b)),
                  pl.BlockSpec((None, LANES, seg), lambda r, b, j: (r, b, 0)),
                  pl.BlockSpec((None, seg, LANES), lambda r, b, j: (r, 0, b)),
                  pl.BlockSpec((2, C_QT, C_BAND_W), lambda r, b, j: (b, 0, 0))],
        out_specs=[ospec, ospec],
        out_shape=[out, out],
        compiler_params=_cparams(("parallel", "parallel", "arbitrary")),
        name=f"attn_c_d{dil}",
    )(q, kt, v, band)


def _combine_c_kernel(*refs):
    o_refs, l_refs, out_ref = refs[0:2 * C_GROUPS:2], refs[1:2 * C_GROUPS:2], refs[-1]
    n = out_ref.shape[0] // C_DIL_MAX
    for r in range(C_DIL_MAX):
        outs, lses = [], []
        for (_, dil), o_ref, l_ref in zip(C_CONFIGS, o_refs, l_refs):
            rows = pl.ds(r // dil, n, stride=C_DIL_MAX // dil)
            cls = pl.ds(r % dil, 1)
            outs.append(o_ref[cls, rows, :][0])
            lses.append(l_ref[cls, rows, :][0])
        m = functools.reduce(jnp.maximum, lses)
        es = [jnp.exp2(l - m) for l in lses]
        num = sum(e * o for e, o in zip(es, outs))
        out_ref[pl.ds(r, n, stride=C_DIL_MAX), :] = num / sum(es)


def _combine_c(os, lses, s, tn=2048):
    nb = C_HEADS // 2
    args, in_specs = [], []
    for (_, dil), o, l in zip(C_CONFIGS, os, lses):
        spec = pl.BlockSpec((dil, tn // dil, LANES), lambda i, b: (0, i, b))
        args += [o, l]
        in_specs += [spec, spec]
    return pl.pallas_call(
        _combine_c_kernel,
        grid=(s // tn, nb),
        in_specs=in_specs,
        out_specs=pl.BlockSpec((tn, LANES), lambda i, b: (i, b)),
        out_shape=jax.ShapeDtypeStruct((s, nb * LANES), jnp.float32),
        compiler_params=_cparams(("parallel", "parallel")),
        name="combine_c",
    )(*args)


def _merge_kernel(x_ref, h_ref, sg_ref, ya_ref, yb_ref, yc_ref, yd_ref,
                  wmt_ref, wb_ref, wo_ref, o_ref, yg_ref):
    c = pl.program_id(1)

    @pl.when(c == 0)
    def _():
        sg = sg_ref[...]
        for n, y_ref in enumerate((ya_ref, yb_ref, yc_ref, yd_ref)):
            g = sg[:, n * BRANCH_W:(n + 1) * BRANCH_W]
            yg_ref[n] = (y_ref[...] * (g * jax.nn.sigmoid(g))).astype(yg_ref.dtype)
        o_ref[...] = x_ref[...]

    h = h_ref[...]
    mixed = 0.0
    for n in range(N_BRANCH):
        gate = jax.nn.sigmoid(_dot_nt(h, wmt_ref[n]))
        z = jnp.dot(yg_ref[n], wb_ref[n], preferred_element_type=jnp.float32)
        mixed = mixed + gate * z
    o_ref[...] += jnp.dot(mixed.astype(jnp.bfloat16), wo_ref[...],
                          preferred_element_type=jnp.float32)


def _merge(x, h, proj, ys, wmerge_t, wbranch_all, wout_all, layer, tm=512, tn=256):
    s, d = x.shape
    nc = d // tn
    row = lambda w: pl.BlockSpec((tm, w), lambda i, c: (i, 0))
    return pl.pallas_call(
        _merge_kernel,
        grid=(s // tm, nc),
        in_specs=[row(d), row(d),
                  pl.BlockSpec((tm, N_BRANCH * BRANCH_W), lambda i, c: (i, OFF_SILU)),
                  row(BRANCH_W), row(BRANCH_W), row(BRANCH_W), row(BRANCH_W),
                  pl.BlockSpec((None, N_BRANCH, tn, d), lambda i, c: (layer, 0, c, 0)),
                  pl.BlockSpec((None, N_BRANCH, BRANCH_W, tn), lambda i, c: (layer, 0, 0, c)),
                  pl.BlockSpec((None, tn, d), lambda i, c: (layer, c, 0))],
        out_specs=row(d),
        out_shape=jax.ShapeDtypeStruct((s, d), jnp.float32),
        scratch_shapes=[pltpu.VMEM((N_BRANCH, tm, BRANCH_W), jnp.bfloat16)],
        compiler_params=_cparams(("parallel", "arbitrary")),
        name="merge",
    )(x, h, proj, *ys, wmerge_t, wbranch_all, wout_all)


def _rope_cos_sin(pos, dim):
    inv = ROPE_THETA ** (-jnp.arange(0, dim, 2, dtype=jnp.float32) / dim)
    ang = pos.astype(jnp.float32)[:, None] * inv[None, :]
    return jnp.cos(ang), jnp.sin(ang)


def _tables_a(s):
    cos, sin = _rope_cos_sin(jnp.arange(s, dtype=jnp.int32), A_ROPE)
    z16, z32 = jnp.zeros((s, 16), jnp.float32), jnp.zeros((s, 32), jnp.float32)
    one = jnp.ones((s, A_NOPE), jnp.float32)
    zero = jnp.zeros((s, A_NOPE), jnp.float32)
    c = jnp.concatenate([one, cos, cos, z32], axis=1)
    sa = jnp.concatenate([zero, -sin, z16, z32], axis=1)
    sb = jnp.concatenate([zero, z16, sin, z32], axis=1)
    return c, sa, sb


def _tables_b(s):
    rows = s // GRID_W
    row_pos = jnp.repeat(jnp.arange(rows, dtype=jnp.int32), GRID_W)
    col_pos = jnp.tile(jnp.arange(GRID_W, dtype=jnp.int32), rows)
    cr, sr = _rope_cos_sin(row_pos, HEAD_DIM // 2)
    cc, sc = _rope_cos_sin(col_pos, HEAD_DIM // 2)
    z = jnp.zeros_like(sr)
    c = jnp.concatenate([cr, cr, cc, cc] * 2, axis=1)
    sa = jnp.concatenate([-sr, z, -sc, z] * 2, axis=1)
    sb = jnp.concatenate([z, sr, z, sc] * 2, axis=1)
    return c, sa, sb


def _pad_lanes(v, width=LANES):
    return jnp.pad(v, ((0, width - v.shape[0]),))[None, :]


def kernel(x, norm_w, w_in, mla_q_norm, mla_kv_norm, mla_w_uq, mla_w_ukv, mla_qk_norm,
           gqa_qk_norm, dil_qk_norm, diff_qk_norm, diff_lambda, diff_subnorm, rel_bias,
           w_branch, w_out):
    b, s, d = x.shape
    assert b == 1 and d == D_MODEL
    depth = norm_w.shape[0]
    bf = jnp.bfloat16

    wt = jnp.swapaxes(w_in, 1, 2)
    seg = lambda a, b: wt[:, a:b].astype(bf)
    w_main_t = jnp.concatenate([seg(SRC_SILU, SRC_MERGE), seg(SRC_C, SRC_D),
                                seg(SRC_D, SRC_SILU), seg(SRC_B, SRC_C)], axis=1)
    w_a_t = jnp.pad(seg(SRC_A, SRC_B), ((0, 0), (0, A_BLOCK - SRC_B), (0, 0)))
    w_merge_t = seg(SRC_MERGE, SRC_END).reshape(depth, N_BRANCH, d, d)
    w_branch_b = w_branch.astype(bf)
    w_out_b = w_out.astype(bf)
    wuq = mla_w_uq.reshape(depth, Q_LORA, A_HEADS, A_QK)
    wuq = jnp.pad(wuq, ((0, 0), (0, 0), (0, 0), (0, LANES - A_QK)))
    wuq = wuq.reshape(depth, Q_LORA, A_HEADS * LANES).astype(bf)
    wukv = mla_w_ukv.astype(bf)

    tabs_a = _tables_a(s)
    tabs_b = _tables_b(s)
    bands_c = [_band(rel_bias, C_HEADS, C_QT, C_BAND_W, C_QT, pad=2 * C_QT, dil=dil,
                     half=C_HALF, col0=g * C_HEADS, key_axis=1, name=f"band_c{g}")
               for g, (_, dil) in enumerate(C_CONFIGS)]
    band_r = D_TQ + 2 * D_BAND
    band_d = _band(rel_bias, D_HEADS, band_r, D_TQ, band_r // 2, pad=D_BAND, dil=1, half=None,
                   col0=C_GROUPS * C_HEADS, key_axis=0, name="band_d")
    c_ut = [max(C_QT, 512 // dil) for _, dil in C_CONFIGS]

    xs = x[0]
    for l in range(depth):
        h = _rmsnorm(xs, norm_w[l][None, :])
        proj = _in_proj(h, w_main_t, l)
        proj_a = _in_proj(h, w_a_t, l)

        qa, kta, va = _prep_a(proj_a, mla_q_norm[l][None, :], mla_kv_norm[l][None, :],
                              wuq[l], wukv[l], _pad_lanes(mla_qk_norm[l, 0]),
                              _pad_lanes(mla_qk_norm[l, 1]), tabs_a)
        y_a = _attn_a(qa, kta, va)

        g2 = lambda v: jnp.tile(v, 2)[None, :]
        qb, ktb, vb = _prep_b(proj, g2(gqa_qk_norm[l, 0]), g2(gqa_qk_norm[l, 1]), tabs_b)
        y_b = _attn_b(qb, ktb, vb)

        os, lses = [], []
        for g in range(C_GROUPS):
            qg, ktg, vg = _prep_c(proj, g2(dil_qk_norm[l, 0, g]), g2(dil_qk_norm[l, 1, g]),
                                  g, c_ut[g])
            og, lg = _attn_c(qg, ktg, vg, bands_c[g])
            os.append(og)
            lses.append(lg)
        y_c = _combine_c(os, lses, s)

        qd, ktd, vd = _prep_d(proj, g2(diff_qk_norm[l, 0]), g2(diff_qk_norm[l, 1]))
        lambda_init = 0.8 - 0.6 * math.exp(-0.3 * l)
        y_d = _attn_d(qd, ktd, vd, band_d, diff_lambda[l], diff_subnorm[l][None, :], lambda_init)

        xs = _merge(xs, h, proj, (y_a, y_b, y_c, y_d), w_merge_t, w_branch_b, w_out_b, l)
    return xs[None]
```

```python
import functools
import math

import numpy as np
import jax
import jax.numpy as jnp
from jax import lax
from jax.experimental import pallas as pl
from jax.experimental.pallas import tpu as pltpu

D_MODEL = 2048
GRID_W = 64
HEAD_DIM = 64
BRANCH_W = 512
N_BRANCH = 4
ROPE_THETA = 10000.0
EPS = 1e-6
NEG = -1e30

A_HEADS = 8
A_NOPE = 64
A_ROPE = 32
A_V = 64
A_QK = A_NOPE + A_ROPE
Q_LORA = 384
KV_LORA = 128

B_HEADS = 8
B_KV_HEADS = 2

C_HEADS = 8
C_CONFIGS = ((128, 1), (512, 4), (2048, 16))
C_GROUPS = len(C_CONFIGS)

D_HEADS = 4
D_V = 2 * HEAD_DIM

NUM_BUCKETS = 32
T5_MAX_DISTANCE = 1024

LANES = 128

SRC_A, SRC_B, SRC_C, SRC_D, SRC_SILU, SRC_MERGE, SRC_END = 0, 544, 1312, 5920, 7456, 9504, 17696

OFF_SILU = 0
OFF_CQ, OFF_CK, OFF_CV = 2048, 3584, 5120
OFF_DQ, OFF_DK, OFF_DV = 6656, 7168, 7680
OFF_BQ, OFF_BKV = 8192, 8704
N_MAIN = 8960
A_BLOCK = 640

VMEM_LIMIT = 56 * 1024 * 1024

LOG2E = math.log2(math.e)

C_QT = 128
C_HALF = 64
C_DIL_MAX = max(d for _, d in C_CONFIGS)
C_BAND_W = 5 * C_QT

D_TQ = 512
D_TK = 512
D_BAND = 1152


def _bucket_thresholds():
    nb = NUM_BUCKETS // 2
    max_exact = nb // 2
    n = np.arange(max_exact, 4 * T5_MAX_DISTANCE, dtype=np.float32)
    large = max_exact + (np.log(n / np.float32(max_exact))
                         / np.float32(math.log(T5_MAX_DISTANCE / max_exact))
                         * np.float32(nb - max_exact)).astype(np.int32)
    large = np.minimum(large, nb - 1)
    return tuple(int(n[np.argmax(large >= max_exact + k)]) for k in range(1, nb - max_exact))


BUCKET_STEPS = _bucket_thresholds()
assert D_BAND - D_TK + 1 >= BUCKET_STEPS[-1]


def _cparams(sem):
    return pltpu.CompilerParams(dimension_semantics=sem, vmem_limit_bytes=VMEM_LIMIT)


def _rmsnorm_kernel(x_ref, w_ref, o_ref):
    x = x_ref[...]
    ms = jnp.mean(x * x, axis=-1, keepdims=True)
    o_ref[...] = (x * lax.rsqrt(ms + EPS) * w_ref[...]).astype(o_ref.dtype)


def _rmsnorm(x, w, tm=512):
    s, d = x.shape
    return pl.pallas_call(
        _rmsnorm_kernel,
        grid=(s // tm,),
        in_specs=[pl.BlockSpec((tm, d), lambda i: (i, 0)),
                  pl.BlockSpec((1, d), lambda i: (0, 0))],
        out_specs=pl.BlockSpec((tm, d), lambda i: (i, 0)),
        out_shape=jax.ShapeDtypeStruct((s, d), jnp.bfloat16),
        compiler_params=_cparams(("parallel",)),
        name="rmsnorm",
    )(x, w)


def _dot_nt(a, bt):
    return lax.dot_general(a, bt, (((1,), (1,)), ((), ())), preferred_element_type=jnp.float32)


def _matmul_nt_kernel(a_ref, bt_ref, o_ref):
    o_ref[...] = _dot_nt(a_ref[...], bt_ref[...])


def _in_proj(h, wt_all, layer, tm=1024, tn=1792):
    s, d = h.shape
    n = wt_all.shape[1]
    tn = min(tn, n)
    return pl.pallas_call(
        _matmul_nt_kernel,
        grid=(s // tm, n // tn),
        in_specs=[pl.BlockSpec((tm, d), lambda i, j: (i, 0)),
                  pl.BlockSpec((None, tn, d), lambda i, j: (layer, j, 0))],
        out_specs=pl.BlockSpec((tm, tn), lambda i, j: (i, j)),
        out_shape=jax.ShapeDtypeStruct((s, n), jnp.float32),
        compiler_params=_cparams(("parallel", "arbitrary")),
        name="in_proj",
    )(h, wt_all)


def _band_kernel(tab_ref, o_ref, *, pad, dil, half, col0, key_axis):
    h = pl.program_id(0)
    rows, width = o_ref.shape
    col = lax.broadcasted_iota(jnp.int32, (rows, width), 1)
    row = lax.broadcasted_iota(jnp.int32, (rows, width), 0) + pl.program_id(1) * rows
    rel_u = (col - row if key_axis == 1 else row - col) - pad
    rel = rel_u * dil
    n = jnp.abs(rel)
    nb = NUM_BUCKETS // 2
    max_exact = nb // 2
    large = jnp.full((rows, width), max_exact, jnp.int32)
    for t in BUCKET_STEPS:
        large = large + jnp.where(n >= t, 1, 0)
    bucket = jnp.where(rel > 0, nb, 0) + jnp.where(n < max_exact, n, large)
    val = jnp.zeros((rows, width), jnp.float32)
    for b in range(NUM_BUCKETS):
        val = jnp.where(bucket == b, tab_ref[b, col0 + h], val)
    val = val * LOG2E
    if half is not None:
        val = jnp.where(jnp.abs(rel_u) <= half, val, NEG)
    o_ref[...] = val


def _band(rel_bias, heads, rows, width, rblk, *, pad, dil, half, col0, key_axis, name):
    return pl.pallas_call(
        functools.partial(_band_kernel, pad=pad, dil=dil, half=half, col0=col0,
                          key_axis=key_axis),
        grid=(heads, rows // rblk),
        in_specs=[pl.BlockSpec(memory_space=pltpu.SMEM)],
        out_specs=pl.BlockSpec((None, rblk, width), lambda h, j: (h, j, 0)),
        out_shape=jax.ShapeDtypeStruct((heads, rows, width), jnp.float32),
        compiler_params=_cparams(("parallel", "arbitrary")),
        name=name,
    )(rel_bias)


def _lane_lo(shape):
    return (lax.broadcasted_iota(jnp.int32, shape, len(shape) - 1) % LANES) < HEAD_DIM


def _rope(x, c, sa, sb):
    return x * c + pltpu.roll(x, LANES - 16, 1) * sa + pltpu.roll(x, 16, 1) * sb


def _halfnorm(x, gain, lo):
    sq = x * x
    s_lo = jnp.sum(jnp.where(lo, sq, 0.0), axis=-1, keepdims=True)
    s_hi = jnp.sum(jnp.where(lo, 0.0, sq), axis=-1, keepdims=True)
    r = jnp.where(lo, lax.rsqrt(s_lo * (1.0 / HEAD_DIM) + EPS),
                  lax.rsqrt(s_hi * (1.0 / HEAD_DIM) + EPS))
    return x * r * gain


ONES_ROWS = 16


def _prep_a_kernel(p_ref, qn_ref, kvn_ref, wuq_ref, wukv_ref, gq_ref, gk_ref,
                   c_ref, sa_ref, sb_ref, qt_out, k_out, vt_out):
    p = p_ref[...]
    cq = p[:, :Q_LORA]
    cq = cq * lax.rsqrt(jnp.mean(cq * cq, axis=-1, keepdims=True) + EPS) * qn_ref[...]
    q = jnp.dot(cq.astype(jnp.bfloat16), wuq_ref[...], preferred_element_type=jnp.float32)
    ckv = p[:, Q_LORA:Q_LORA + KV_LORA]
    ckv = ckv * lax.rsqrt(jnp.mean(ckv * ckv, axis=-1, keepdims=True) + EPS) * kvn_ref[...]
    kvu = jnp.dot(ckv.astype(jnp.bfloat16), wukv_ref[...], preferred_element_type=jnp.float32)
    kr = pltpu.roll(p[:, Q_LORA + KV_LORA:], HEAD_DIM, 1)
    lo = _lane_lo(kr.shape)
    c, sa, sb = c_ref[...], sa_ref[...], sb_ref[...]
    scale = A_QK ** -0.5 * LOG2E
    ones = jnp.ones((ONES_ROWS, p.shape[0]), vt_out.dtype)
    for h in range(A_HEADS):
        qh = q[:, h * LANES:(h + 1) * LANES]
        ss = jnp.sum(qh * qh, axis=-1, keepdims=True) * (1.0 / A_QK)
        qh = qh * lax.rsqrt(ss + EPS) * gq_ref[...]
        qt_out[h] = (_rope(qh, c, sa, sb) * scale).T.astype(qt_out.dtype)
        kvh = kvu[:, h * LANES:(h + 1) * LANES]
        kh = jnp.where(lo, kvh, kr)
        ss = jnp.sum(kh * kh, axis=-1, keepdims=True) * (1.0 / A_QK)
        kh = kh * lax.rsqrt(ss + EPS) * gk_ref[...]
        k_out[h] = _rope(kh, c, sa, sb).astype(k_out.dtype)
        vt_out[h, :A_V, :] = kvh.T[A_NOPE:].astype(vt_out.dtype)
        vt_out[h, A_V:, :] = ones


def _prep_a(proj, qn, kvn, wuq, wukv, gq, gk, tabs, tm=512):
    s = proj.shape[0]
    row = lambda w: pl.BlockSpec((1, w), lambda i: (0, 0))
    tab = pl.BlockSpec((tm, LANES), lambda i: (i, 0))
    return pl.pallas_call(
        _prep_a_kernel,
        grid=(s // tm,),
        in_specs=[pl.BlockSpec((tm, A_BLOCK), lambda i: (i, 0)),
                  row(Q_LORA), row(KV_LORA),
                  pl.BlockSpec(wuq.shape, lambda i: (0, 0)),
                  pl.BlockSpec(wukv.shape, lambda i: (0, 0)),
                  row(LANES), row(LANES), tab, tab, tab],
        out_specs=[pl.BlockSpec((A_HEADS, LANES, tm), lambda i: (0, 0, i)),
                   pl.BlockSpec((A_HEADS, tm, LANES), lambda i: (0, i, 0)),
                   pl.BlockSpec((A_HEADS, A_V + ONES_ROWS, tm), lambda i: (0, 0, i))],
        out_shape=[jax.ShapeDtypeStruct((A_HEADS, LANES, s), jnp.bfloat16),
                   jax.ShapeDtypeStruct((A_HEADS, s, LANES), jnp.bfloat16),
                   jax.ShapeDtypeStruct((A_HEADS, A_V + ONES_ROWS, s), jnp.bfloat16)],
        compiler_params=_cparams(("parallel",)),
        name="prep_a",
    )(proj, qn, kvn, wuq, wukv, gq, gk, *tabs)


B_TQ = 256


def _prep_b_kernel(q_ref, kv_ref, gq_ref, gk_ref, c_ref, sa_ref, sb_ref,
                   qt_out, k_out, vt_out):
    c, sa, sb = c_ref[...], sa_ref[...], sb_ref[...]
    lo = _lane_lo(c.shape)
    tm = c.shape[0]
    scale = HEAD_DIM ** -0.5 * LOG2E
    q = q_ref[...]
    for b in range(B_HEADS // 2):
        x = _rope(_halfnorm(q[:, b * LANES:(b + 1) * LANES], gq_ref[...], lo), c, sa, sb) * scale
        xr = pltpu.roll(x, HEAD_DIM, 1)
        g = b // 2
        even, odd = (x, xr) if g == 0 else (xr, x)
        keep = lo if g == 0 else jnp.logical_not(lo)
        for j, xh in enumerate((even, odd)):
            hh = 2 * (b % 2) + j
            qt_out[g, :, hh * tm:(hh + 1) * tm] = jnp.where(keep, xh, 0.0).T.astype(qt_out.dtype)
    kv = kv_ref[...]
    k_out[...] = _rope(_halfnorm(kv[:, :LANES], gk_ref[...], lo), c, sa, sb).astype(k_out.dtype)
    vt = kv[:, LANES:].T.astype(vt_out.dtype)
    ones = jnp.ones((ONES_ROWS, tm), vt_out.dtype)
    for g in range(B_KV_HEADS):
        vt_out[g, :HEAD_DIM, :] = vt[g * HEAD_DIM:(g + 1) * HEAD_DIM]
        vt_out[g, HEAD_DIM:, :] = ones


def _prep_b(proj, gq, gk, tabs, tm=B_TQ):
    s = proj.shape[0]
    row = pl.BlockSpec((1, LANES), lambda i: (0, 0))
    tab = pl.BlockSpec((tm, LANES), lambda i: (i, 0))
    rv = HEAD_DIM + ONES_ROWS
    return pl.pallas_call(
        _prep_b_kernel,
        grid=(s // tm,),
        in_specs=[pl.BlockSpec((tm, 512), lambda i: (i, OFF_BQ // 512)),
                  pl.BlockSpec((tm, 256), lambda i: (i, OFF_BKV // 256)),
                  row, row, tab, tab, tab],
        out_specs=[pl.BlockSpec((2, None, LANES, 4 * tm), lambda i: (0, i, 0, 0)),
                   pl.BlockSpec((tm, LANES), lambda i: (i, 0)),
                   pl.BlockSpec((2, rv, tm), lambda i: (0, 0, i))],
        out_shape=[jax.ShapeDtypeStruct((2, s // tm, LANES, 4 * tm), jnp.bfloat16),
                   jax.ShapeDtypeStruct((s, LANES), jnp.bfloat16),
                   jax.ShapeDtypeStruct((2, rv, s), jnp.bfloat16)],
        compiler_params=_cparams(("parallel",)),
        name="prep_b",
    )(proj, proj, gq, gk, *tabs)


def _prep_d_kernel(q_ref, k_ref, v_ref, gq_ref, gk_ref, qt_out, k_out, vt_out):
    q, k, v = q_ref[...], k_ref[...], v_ref[...]
    tm = q.shape[0]
    lo = _lane_lo((tm, LANES))
    scale = HEAD_DIM ** -0.5 * LOG2E
    ones = jnp.ones((ONES_ROWS, tm), vt_out.dtype)
    for h in range(D_HEADS):
        cols = slice(h * LANES, (h + 1) * LANES)
        x = _halfnorm(q[:, cols], gq_ref[...], lo) * scale
        qt_out[h, :, :tm] = jnp.where(lo, x, 0.0).T.astype(qt_out.dtype)
        qt_out[h, :, tm:] = jnp.where(lo, 0.0, x).T.astype(qt_out.dtype)
        k_out[:, cols] = _halfnorm(k[:, cols], gk_ref[...], lo).astype(k_out.dtype)
        vt_out[h, :D_V, :] = v[:, cols].T.astype(vt_out.dtype)
        vt_out[h, D_V:, :] = ones


def _prep_d(proj, gq, gk, tm=D_TQ):
    s = proj.shape[0]
    row = pl.BlockSpec((1, LANES), lambda i: (0, 0))
    blk = lambda off: pl.BlockSpec((tm, 512), lambda i: (i, off // 512))
    rv = D_V + ONES_ROWS
    return pl.pallas_call(
        _prep_d_kernel,
        grid=(s // tm,),
        in_specs=[blk(OFF_DQ), blk(OFF_DK), blk(OFF_DV), row, row],
        out_specs=[pl.BlockSpec((D_HEADS, None, LANES, 2 * tm), lambda i: (0, i, 0, 0)),
                   pl.BlockSpec((tm, 512), lambda i: (i, 0)),
                   pl.BlockSpec((D_HEADS, rv, tm), lambda i: (0, 0, i))],
        out_shape=[jax.ShapeDtypeStruct((D_HEADS, s // tm, LANES, 2 * tm), jnp.bfloat16),
                   jax.ShapeDtypeStruct((s, 512), jnp.bfloat16),
                   jax.ShapeDtypeStruct((D_HEADS, rv, s), jnp.bfloat16)],
        compiler_params=_cparams(("parallel",)),
        name="prep_d",
    )(proj, proj, proj, gq, gk)


def _prep_c_kernel(q_ref, k_ref, v_ref, gq_ref, gk_ref, q_out, kt_out, v_out, *, dil, ut):
    lo = _lane_lo((ut, LANES))
    scale = HEAD_DIM ** -0.5 * LOG2E
    for r in range(dil):
        rows = pl.ds(r, ut, stride=dil) if dil > 1 else pl.ds(0, ut)
        q_out[r] = (_halfnorm(q_ref[rows, :], gq_ref[...], lo) * scale).astype(q_out.dtype)
        kt_out[r] = _halfnorm(k_ref[rows, :], gk_ref[...], lo).T.astype(kt_out.dtype)
        v_out[r] = v_ref[rows, :].astype(v_out.dtype)


def _prep_c(proj, gq, gk, g, ut):
    s = proj.shape[0]
    dil = C_CONFIGS[g][1]
    seg = s // dil
    nb = C_HEADS // 2
    tn = ut * dil
    blk = lambda off: pl.BlockSpec((tn, LANES), lambda i, b: (i, off // LANES + g * nb + b))
    row = pl.BlockSpec((1, LANES), lambda i, b: (0, 0))
    return pl.pallas_call(
        functools.partial(_prep_c_kernel, dil=dil, ut=ut),
        grid=(s // tn, nb),
        in_specs=[blk(OFF_CQ), blk(OFF_CK), blk(OFF_CV), row, row],
        out_specs=[pl.BlockSpec((dil, ut, LANES), lambda i, b: (0, i, b)),
                   pl.BlockSpec((dil, LANES, ut), lambda i, b: (0, b, i)),
                   pl.BlockSpec((dil, ut, LANES), lambda i, b: (0, i, b))],
        out_shape=[jax.ShapeDtypeStruct((dil, seg, nb * LANES), jnp.bfloat16),
                   jax.ShapeDtypeStruct((dil, nb * LANES, seg), jnp.bfloat16),
                   jax.ShapeDtypeStruct((dil, seg, nb * LANES), jnp.bfloat16)],
        compiler_params=_cparams(("parallel", "parallel")),
        name=f"prep_c{g}",
    )(proj, proj, proj, gq, gk)


def _scores(k, qt, s_ref, mc_ref, slot, bias=None):
    st = jnp.dot(k, qt, preferred_element_type=jnp.float32)
    if bias is not None:
        st = st + bias
    s_ref[slot] = st
    mc_ref[slot] = jnp.max(st, axis=0, keepdims=True)


def _consume(s_ref, mc_ref, slot, vt, m_ref, acc_ref):
    m_prev = m_ref[...]
    m_next = jnp.maximum(m_prev, mc_ref[slot])
    p = jnp.exp2(s_ref[slot] - m_next)
    alpha = jnp.exp2(m_prev - m_next)
    m_ref[...] = m_next
    pv = jnp.dot(vt, p.astype(jnp.bfloat16), preferred_element_type=jnp.float32)
    acc_ref[...] = alpha * acc_ref[...] + pv


FLASH_UNROLL = 4


def _flash_loop(nk, score, consume):
    u = min(FLASH_UNROLL, nk)
    assert nk % u == 0 and u % 2 == 0
    score(0, 0)

    def body(j, carry):
        for i in range(u):
            score(j * u + i + 1, (i + 1) % 2)
            consume(j * u + i, i % 2)
        return carry

    lax.fori_loop(0, nk // u - 1, body, 0)
    for c in range(nk - u, nk):
        if c + 1 < nk:
            score(c + 1, (c + 1) % 2)
        consume(c, c % 2)


def _init_stats(m_ref, acc_ref):
    m_ref[...] = jnp.full_like(m_ref, -jnp.inf)
    acc_ref[...] = jnp.zeros_like(acc_ref)


def _normalised(acc_ref, rv):
    acc = acc_ref[...]
    return acc[:rv] / acc[rv:rv + 1]


def _flash_scratch(streams, tk, n, rv):
    return ([pltpu.VMEM((2, tk, n), jnp.float32)] * streams
            + [pltpu.VMEM((2, 1, n), jnp.float32)] * streams
            + [pltpu.VMEM((1, n), jnp.float32)] * streams
            + [pltpu.VMEM((rv, n), jnp.float32)] * streams)


def _attn_a_kernel(qt_ref, k_ref, vt_ref, o_ref, s0_ref, s1_ref, mc0_ref, mc1_ref,
                   m0_ref, m1_ref, acc0_ref, acc1_ref, *, tk, nk):
    streams = ((s0_ref, mc0_ref, m0_ref, acc0_ref), (s1_ref, mc1_ref, m1_ref, acc1_ref))
    for _, _, m_ref, acc_ref in streams:
        _init_stats(m_ref, acc_ref)

    def score(c, slot):
        k0 = pl.multiple_of(c * tk, tk)
        for hh, (s_ref, mc_ref, _, _) in enumerate(streams):
            _scores(k_ref[hh, pl.ds(k0, tk), :], qt_ref[hh], s_ref, mc_ref, slot)

    def consume(c, slot):
        k0 = pl.multiple_of(c * tk, tk)
        for hh, (s_ref, mc_ref, m_ref, acc_ref) in enumerate(streams):
            _consume(s_ref, mc_ref, slot, vt_ref[hh, :, pl.ds(k0, tk)], m_ref, acc_ref)

    _flash_loop(nk, score, consume)
    o_ref[...] = jnp.concatenate([_normalised(acc0_ref, A_V), _normalised(acc1_ref, A_V)],
                                 axis=0).T


def _attn_a(qt, k, vt, tq=512, tk=512):
    s = k.shape[1]
    tk = min(tk, s // 2)
    rv = vt.shape[1]
    return pl.pallas_call(
        functools.partial(_attn_a_kernel, tk=tk, nk=s // tk),
        grid=(A_HEADS // 2, s // tq),
        in_specs=[pl.BlockSpec((2, LANES, tq), lambda p, i: (p, 0, i)),
                  pl.BlockSpec((2, s, LANES), lambda p, i: (p, 0, 0)),
                  pl.BlockSpec((2, rv, s), lambda p, i: (p, 0, 0))],
        out_specs=pl.BlockSpec((tq, LANES), lambda p, i: (i, p)),
        out_shape=jax.ShapeDtypeStruct((s, BRANCH_W), jnp.float32),
        scratch_shapes=_flash_scratch(2, tk, tq, rv),
        compiler_params=_cparams(("parallel", "arbitrary")),
        name="attn_a",
    )(qt, k, vt)


def _attn_b_kernel(qt_ref, k_ref, vt_ref, o_ref, s_ref, mc_ref, m_ref, acc_ref, *, tq, tk, nk):
    _init_stats(m_ref, acc_ref)

    def score(c, slot):
        _scores(k_ref[pl.ds(pl.multiple_of(c * tk, tk), tk), :], qt_ref[...], s_ref, mc_ref, slot)

    def consume(c, slot):
        _consume(s_ref, mc_ref, slot, vt_ref[:, pl.ds(pl.multiple_of(c * tk, tk), tk)],
                 m_ref, acc_ref)

    _flash_loop(nk, score, consume)
    ot = _normalised(acc_ref, HEAD_DIM)
    o_ref[...] = jnp.concatenate([ot[:, h * tq:(h + 1) * tq] for h in range(4)], axis=0).T


def _attn_b(qt, k, vt, tq=B_TQ, tk=512):
    s = k.shape[0]
    tk = min(tk, s // 2)
    rv = vt.shape[1]
    return pl.pallas_call(
        functools.partial(_attn_b_kernel, tq=tq, tk=tk, nk=s // tk),
        grid=(2, s // tq),
        in_specs=[pl.BlockSpec((None, None, LANES, 4 * tq), lambda g, i: (g, i, 0, 0)),
                  pl.BlockSpec((s, LANES), lambda g, i: (0, 0)),
                  pl.BlockSpec((None, rv, s), lambda g, i: (g, 0, 0))],
        out_specs=pl.BlockSpec((tq, 2 * LANES), lambda g, i: (i, g)),
        out_shape=jax.ShapeDtypeStruct((s, BRANCH_W), jnp.float32),
        scratch_shapes=_flash_scratch(1, tk, 4 * tq, rv),
        compiler_params=_cparams(("parallel", "arbitrary")),
        name="attn_b",
    )(qt, k, vt)


def _attn_d_kernel(qt_ref, k_ref, vt_ref, band_ref, lam_ref, sub_ref, o_ref,
                   s_ref, mc_ref, m_ref, acc_ref, *, tq, tk, nk, lambda_init):
    _init_stats(m_ref, acc_ref)
    q0 = pl.program_id(1) * tq
    hi = tq + 2 * D_BAND - tk

    def score(c, slot):
        k0 = pl.multiple_of(c * tk, tk)
        b0 = pl.multiple_of(jnp.clip(k0 - q0 + D_BAND, 0, hi), LANES)
        bias = band_ref[pl.ds(b0, tk), :]
        _scores(k_ref[pl.ds(k0, tk), :], qt_ref[...], s_ref, mc_ref, slot,
                jnp.concatenate([bias, bias], axis=1))

    def consume(c, slot):
        _consume(s_ref, mc_ref, slot, vt_ref[:, pl.ds(pl.multiple_of(c * tk, tk), tk)],
                 m_ref, acc_ref)

    _flash_loop(nk, score, consume)
    lv = lam_ref[...]
    lam = (jnp.exp(jnp.sum(lv[0:1] * lv[1:2], axis=-1, keepdims=True))
           - jnp.exp(jnp.sum(lv[2:3] * lv[3:4], axis=-1, keepdims=True)) + lambda_init)
    ot = _normalised(acc_ref, D_V)
    o = (ot[:, :tq] - lam * ot[:, tq:]).T
    o = o * lax.rsqrt(jnp.mean(o * o, axis=-1, keepdims=True) + EPS) * sub_ref[...]
    o_ref[...] = o * (1.0 - lambda_init)


def _attn_d(qt, k, vt, band, lam_vecs, subw, lambda_init, tq=D_TQ, tk=D_TK):
    s = k.shape[0]
    rv = vt.shape[1]
    return pl.pallas_call(
        functools.partial(_attn_d_kernel, tq=tq, tk=tk, nk=s // tk, lambda_init=lambda_init),
        grid=(D_HEADS, s // tq),
        in_specs=[pl.BlockSpec((None, None, LANES, 2 * tq), lambda h, i: (h, i, 0, 0)),
                  pl.BlockSpec((s, LANES), lambda h, i: (0, h)),
                  pl.BlockSpec((None, rv, s), lambda h, i: (h, 0, 0)),
                  pl.BlockSpec((None, tq + 2 * D_BAND, tq), lambda h, i: (h, 0, 0)),
                  pl.BlockSpec((4, HEAD_DIM), lambda h, i: (0, 0)),
                  pl.BlockSpec((1, LANES), lambda h, i: (0, 0))],
        out_specs=pl.BlockSpec((tq, LANES), lambda h, i: (i, h)),
        out_shape=jax.ShapeDtypeStruct((s, BRANCH_W), jnp.float32),
        scratch_shapes=_flash_scratch(1, tk, 2 * tq, rv),
        compiler_params=_cparams(("parallel", "arbitrary")),
        name="attn_d",
    )(qt, k, vt, band, lam_vecs, subw)


def _attn_c_kernel(q_ref, kt_ref, v_ref, band_ref, o_ref, lse_ref, *, nt, seg, win, unroll):
    j = pl.program_id(2)
    lo = _lane_lo((C_QT, LANES))

    def scores(t):
        r0 = pl.multiple_of(t * C_QT, C_QT)
        u0 = j * (nt * C_QT) + r0
        ws = pl.multiple_of(jnp.clip(u0 - C_QT, 0, seg - win), C_QT)
        x = pl.multiple_of(ws - u0 + 2 * C_QT, C_QT)
        q = q_ref[pl.ds(r0, C_QT), :]
        zero = jnp.zeros_like(q)
        q2 = jnp.concatenate([jnp.where(lo, q, zero), jnp.where(lo, zero, q)], axis=0)
        s = jnp.dot(q2, kt_ref[:, pl.ds(ws, win)], preferred_element_type=jnp.float32)
        s = s + jnp.concatenate([band_ref[0, :, pl.ds(x, win)], band_ref[1, :, pl.ds(x, win)]], axis=0)
        return r0, ws, s

    def finish(r0, ws, s):
        m = jnp.max(s, axis=1, keepdims=True)
        p = jnp.exp2(s - m)
        l = jnp.sum(p, axis=1, keepdims=True)
        pv = jnp.dot(p.astype(jnp.bfloat16), v_ref[pl.ds(ws, win), :],
                     preferred_element_type=jnp.float32)
        o = pv / l
        lse = m + jnp.log2(l)
        o_ref[pl.ds(r0, C_QT), :] = jnp.where(lo, o[:C_QT], o[C_QT:])
        lse_ref[pl.ds(r0, C_QT), :] = jnp.where(lo, lse[:C_QT], lse[C_QT:])

    def body(tt, carry):
        tiles = [scores(tt * unroll + i) for i in range(unroll)]
        for tile in tiles:
            finish(*tile)
        return carry

    lax.fori_loop(0, nt // unroll, body, 0)


def _attn_c(q, kt, v, band, nt_max=8):
    dil, seg, w = q.shape
    nt = min(nt_max, seg // C_QT)
    win = min(3 * C_QT, seg)
    tqb = nt * C_QT
    out = jax.ShapeDtypeStruct((dil, seg, w), jnp.float32)
    ospec = pl.BlockSpec((None, tqb, LANES), lambda r, b, j: (r, j, b))
    return pl.pallas_call(
        functools.partial(_attn_c_kernel, nt=nt, seg=seg, win=win, unroll=min(4, nt)),
        grid=(dil, C_HEADS // 2, seg // tqb),
        in_specs=[pl.BlockSpec((None, tqb, LANES), lambda r, b, j: (r, j, b)),
                  pl.BlockSpec((None, LANES, seg), lambda r, b, j: (r, b, 0)),
                  pl.BlockSpec((None, seg, LANES), lambda r, b, j: (r, 0, b)),
                  pl.BlockSpec((2, C_QT, C_BAND_W), lambda r, b, j: (b, 0, 0))],
        out_specs=[ospec, ospec],
        out_shape=[out, out],
        compiler_params=_cparams(("parallel", "parallel", "arbitrary")),
        name=f"attn_c_d{dil}",
    )(q, kt, v, band)


def _combine_c_kernel(*refs):
    o_refs, l_refs, out_ref = refs[0:2 * C_GROUPS:2], refs[1:2 * C_GROUPS:2], refs[-1]
    n = out_ref.shape[0] // C_DIL_MAX
    for r in range(C_DIL_MAX):
        outs, lses = [], []
        for (_, dil), o_ref, l_ref in zip(C_CONFIGS, o_refs, l_refs):
            rows = pl.ds(r // dil, n, stride=C_DIL_MAX // dil)
            cls = pl.ds(r % dil, 1)
            outs.append(o_ref[cls, rows, :][0])
            lses.append(l_ref[cls, rows, :][0])
        m = functools.reduce(jnp.maximum, lses)
        es = [jnp.exp2(l - m) for l in lses]
        num = sum(e * o for e, o in zip(es, outs))
        out_ref[pl.ds(r, n, stride=C_DIL_MAX), :] = num / sum(es)


def _combine_c(os, lses, s, tn=2048):
    nb = C_HEADS // 2
    args, in_specs = [], []
    for (_, dil), o, l in zip(C_CONFIGS, os, lses):
        spec = pl.BlockSpec((dil, tn // dil, LANES), lambda i, b: (0, i, b))
        args += [o, l]
        in_specs += [spec, spec]
    return pl.pallas_call(
        _combine_c_kernel,
        grid=(s // tn, nb),
        in_specs=in_specs,
        out_specs=pl.BlockSpec((tn, LANES), lambda i, b: (i, b)),
        out_shape=jax.ShapeDtypeStruct((s, nb * LANES), jnp.float32),
        compiler_params=_cparams(("parallel", "parallel")),
        name="combine_c",
    )(*args)


def _merge_kernel(x_ref, h_ref, sg_ref, ya_ref, yb_ref, yc_ref, yd_ref,
                  wmt_ref, wb_ref, wo_ref, o_ref, yg_ref):
    c = pl.program_id(1)

    @pl.when(c == 0)
    def _():
        sg = sg_ref[...]
        for n, y_ref in enumerate((ya_ref, yb_ref, yc_ref, yd_ref)):
            g = sg[:, n * BRANCH_W:(n + 1) * BRANCH_W]
            yg_ref[n] = (y_ref[...] * (g * jax.nn.sigmoid(g))).astype(yg_ref.dtype)
        o_ref[...] = x_ref[...]

    h = h_ref[...]
    mixed = 0.0
    for n in range(N_BRANCH):
        gate = jax.nn.sigmoid(_dot_nt(h, wmt_ref[n]))
        z = jnp.dot(yg_ref[n], wb_ref[n], preferred_element_type=jnp.float32)
        mixed = mixed + gate * z
    o_ref[...] += jnp.dot(mixed.astype(jnp.bfloat16), wo_ref[...],
                          preferred_element_type=jnp.float32)


def _merge(x, h, proj, ys, wmerge_t, wbranch_all, wout_all, layer, tm=512, tn=256):
    s, d = x.shape
    nc = d // tn
    row = lambda w: pl.BlockSpec((tm, w), lambda i, c: (i, 0))
    return pl.pallas_call(
        _merge_kernel,
        grid=(s // tm, nc),
        in_specs=[row(d), row(d),
                  pl.BlockSpec((tm, N_BRANCH * BRANCH_W), lambda i, c: (i, OFF_SILU)),
                  row(BRANCH_W), row(BRANCH_W), row(BRANCH_W), row(BRANCH_W),
                  pl.BlockSpec((None, N_BRANCH, tn, d), lambda i, c: (layer, 0, c, 0)),
                  pl.BlockSpec((None, N_BRANCH, BRANCH_W, tn), lambda i, c: (layer, 0, 0, c)),
                  pl.BlockSpec((None, tn, d), lambda i, c: (layer, c, 0))],
        out_specs=row(d),
        out_shape=jax.ShapeDtypeStruct((s, d), jnp.float32),
        scratch_shapes=[pltpu.VMEM((N_BRANCH, tm, BRANCH_W), jnp.bfloat16)],
        compiler_params=_cparams(("parallel", "arbitrary")),
        name="merge",
    )(x, h, proj, *ys, wmerge_t, wbranch_all, wout_all)


def _rope_cos_sin(pos, dim):
    inv = ROPE_THETA ** (-jnp.arange(0, dim, 2, dtype=jnp.float32) / dim)
    ang = pos.astype(jnp.float32)[:, None] * inv[None, :]
    return jnp.cos(ang), jnp.sin(ang)


def _tables_a(s):
    cos, sin = _rope_cos_sin(jnp.arange(s, dtype=jnp.int32), A_ROPE)
    z16, z32 = jnp.zeros((s, 16), jnp.float32), jnp.zeros((s, 32), jnp.float32)
    one = jnp.ones((s, A_NOPE), jnp.float32)
    zero = jnp.zeros((s, A_NOPE), jnp.float32)
    c = jnp.concatenate([one, cos, cos, z32], axis=1)
    sa = jnp.concatenate([zero, -sin, z16, z32], axis=1)
    sb = jnp.concatenate([zero, z16, sin, z32], axis=1)
    return c, sa, sb


def _tables_b(s):
    rows = s // GRID_W
    row_pos = jnp.repeat(jnp.arange(rows, dtype=jnp.int32), GRID_W)
    col_pos = jnp.tile(jnp.arange(GRID_W, dtype=jnp.int32), rows)
    cr, sr = _rope_cos_sin(row_pos, HEAD_DIM // 2)
    cc, sc = _rope_cos_sin(col_pos, HEAD_DIM // 2)
    z = jnp.zeros_like(sr)
    c = jnp.concatenate([cr, cr, cc, cc] * 2, axis=1)
    sa = jnp.concatenate([-sr, z, -sc, z] * 2, axis=1)
    sb = jnp.concatenate([z, sr, z, sc] * 2, axis=1)
    return c, sa, sb


def _pad_lanes(v, width=LANES):
    return jnp.pad(v, ((0, width - v.shape[0]),))[None, :]


def kernel(x, norm_w, w_in, mla_q_norm, mla_kv_norm, mla_w_uq, mla_w_ukv, mla_qk_norm,
           gqa_qk_norm, dil_qk_norm, diff_qk_norm, diff_lambda, diff_subnorm, rel_bias,
           w_branch, w_out):
    b, s, d = x.shape
    assert b == 1 and d == D_MODEL
    depth = norm_w.shape[0]
    bf = jnp.bfloat16

    wt = jnp.swapaxes(w_in, 1, 2)
    seg = lambda a, b: wt[:, a:b].astype(bf)
    w_main_t = jnp.concatenate([seg(SRC_SILU, SRC_MERGE), seg(SRC_C, SRC_D),
                                seg(SRC_D, SRC_SILU), seg(SRC_B, SRC_C)], axis=1)
    w_a_t = jnp.pad(seg(SRC_A, SRC_B), ((0, 0), (0, A_BLOCK - SRC_B), (0, 0)))
    w_merge_t = seg(SRC_MERGE, SRC_END).reshape(depth, N_BRANCH, d, d)
    w_branch_b = w_branch.astype(bf)
    w_out_b = w_out.astype(bf)
    wuq = mla_w_uq.reshape(depth, Q_LORA, A_HEADS, A_QK)
    wuq = jnp.pad(wuq, ((0, 0), (0, 0), (0, 0), (0, LANES - A_QK)))
    wuq = wuq.reshape(depth, Q_LORA, A_HEADS * LANES).astype(bf)
    wukv = mla_w_ukv.astype(bf)

    tabs_a = _tables_a(s)
    tabs_b = _tables_b(s)
    bands_c = [_band(rel_bias, C_HEADS, C_QT, C_BAND_W, C_QT, pad=2 * C_QT, dil=dil,
                     half=C_HALF, col0=g * C_HEADS, key_axis=1, name=f"band_c{g}")
               for g, (_, dil) in enumerate(C_CONFIGS)]
    band_r = D_TQ + 2 * D_BAND
    band_d = _band(rel_bias, D_HEADS, band_r, D_TQ, band_r // 2, pad=D_BAND, dil=1, half=None,
                   col0=C_GROUPS * C_HEADS, key_axis=0, name="band_d")
    c_ut = [max(C_QT, min(s, 2048) // dil) for _, dil in C_CONFIGS]

    xs = x[0]
    for l in range(depth):
        h = _rmsnorm(xs, norm_w[l][None, :])
        proj = _in_proj(h, w_main_t, l)
        proj_a = _in_proj(h, w_a_t, l)

        qa, kta, va = _prep_a(proj_a, mla_q_norm[l][None, :], mla_kv_norm[l][None, :],
                              wuq[l], wukv[l], _pad_lanes(mla_qk_norm[l, 0]),
                              _pad_lanes(mla_qk_norm[l, 1]), tabs_a)
        y_a = _attn_a(qa, kta, va)

        g2 = lambda v: jnp.tile(v, 2)[None, :]
        qb, ktb, vb = _prep_b(proj, g2(gqa_qk_norm[l, 0]), g2(gqa_qk_norm[l, 1]), tabs_b)
        y_b = _attn_b(qb, ktb, vb)

        os, lses = [], []
        for g in range(C_GROUPS):
            qg, ktg, vg = _prep_c(proj, g2(dil_qk_norm[l, 0, g]), g2(dil_qk_norm[l, 1, g]),
                                  g, c_ut[g])
            og, lg = _attn_c(qg, ktg, vg, bands_c[g])
            os.append(og)
            lses.append(lg)
        y_c = _combine_c(os, lses, s)

        qd, ktd, vd = _prep_d(proj, g2(diff_qk_norm[l, 0]), g2(diff_qk_norm[l, 1]))
        lambda_init = 0.8 - 0.6 * math.exp(-0.3 * l)
        y_d = _attn_d(qd, ktd, vd, band_d, diff_lambda[l], diff_subnorm[l][None, :], lambda_init)

        xs = _merge(xs, h, proj, (y_a, y_b, y_c, y_d), w_merge_t, w_branch_b, w_out_b, l)
    return xs[None]
```

```python
import functools
import math

import numpy as np
import jax
import jax.numpy as jnp
from jax import lax
from jax.experimental import pallas as pl
from jax.experimental.pallas import tpu as pltpu

D_MODEL = 2048
GRID_W = 64
HEAD_DIM = 64
BRANCH_W = 512
N_BRANCH = 4
ROPE_THETA = 10000.0
EPS = 1e-6
NEG = -1e30

A_HEADS = 8
A_NOPE = 64
A_ROPE = 32
A_V = 64
A_QK = A_NOPE + A_ROPE
Q_LORA = 384
KV_LORA = 128

B_HEADS = 8
B_KV_HEADS = 2

C_HEADS = 8
C_CONFIGS = ((128, 1), (512, 4), (2048, 16))
C_GROUPS = len(C_CONFIGS)

D_HEADS = 4
D_V = 2 * HEAD_DIM

NUM_BUCKETS = 32
T5_MAX_DISTANCE = 1024

LANES = 128

SRC_A, SRC_B, SRC_C, SRC_D, SRC_SILU, SRC_MERGE, SRC_END = 0, 544, 1312, 5920, 7456, 9504, 17696

OFF_SILU = 0
OFF_CQ, OFF_CK, OFF_CV = 2048, 3584, 5120
OFF_DQ, OFF_DK, OFF_DV = 6656, 7168, 7680
OFF_BQ, OFF_BKV = 8192, 8704
OFF_A = 8960
A_BLOCK = 640

VMEM_LIMIT = 56 * 1024 * 1024

LOG2E = math.log2(math.e)

C_QT = 128
C_HALF = 64
C_DIL_MAX = max(d for _, d in C_CONFIGS)
C_BAND_W = 5 * C_QT

D_TQ = 512
D_TK = 512
D_BAND = 1152


def _bucket_thresholds():
    nb = NUM_BUCKETS // 2
    max_exact = nb // 2
    n = np.arange(max_exact, 4 * T5_MAX_DISTANCE, dtype=np.float32)
    large = max_exact + (np.log(n / np.float32(max_exact))
                         / np.float32(math.log(T5_MAX_DISTANCE / max_exact))
                         * np.float32(nb - max_exact)).astype(np.int32)
    large = np.minimum(large, nb - 1)
    return tuple(int(n[np.argmax(large >= max_exact + k)]) for k in range(1, nb - max_exact))


BUCKET_STEPS = _bucket_thresholds()
assert D_BAND - D_TK + 1 >= BUCKET_STEPS[-1]


def _cparams(sem):
    return pltpu.CompilerParams(dimension_semantics=sem, vmem_limit_bytes=VMEM_LIMIT)


def _rmsnorm_kernel(x_ref, w_ref, o_ref):
    x = x_ref[...]
    ms = jnp.mean(x * x, axis=-1, keepdims=True)
    o_ref[...] = (x * lax.rsqrt(ms + EPS) * w_ref[...]).astype(o_ref.dtype)


def _rmsnorm(x, w, tm=512):
    s, d = x.shape
    return pl.pallas_call(
        _rmsnorm_kernel,
        grid=(s // tm,),
        in_specs=[pl.BlockSpec((tm, d), lambda i: (i, 0)),
                  pl.BlockSpec((1, d), lambda i: (0, 0))],
        out_specs=pl.BlockSpec((tm, d), lambda i: (i, 0)),
        out_shape=jax.ShapeDtypeStruct((s, d), jnp.bfloat16),
        compiler_params=_cparams(("parallel",)),
        name="rmsnorm",
    )(x, w)


def _dot_nt(a, bt):
    return lax.dot_general(a, bt, (((1,), (1,)), ((), ())), preferred_element_type=jnp.float32)


def _matmul_nt_kernel(a_ref, bt_ref, o_ref):
    o_ref[...] = _dot_nt(a_ref[...], bt_ref[...])


def _in_proj(h, wt_all, layer, tm=1024, tn=1920):
    s, d = h.shape
    n = wt_all.shape[1]
    tn = min(tn, n)
    return pl.pallas_call(
        _matmul_nt_kernel,
        grid=(s // tm, n // tn),
        in_specs=[pl.BlockSpec((tm, d), lambda i, j: (i, 0)),
                  pl.BlockSpec((None, tn, d), lambda i, j: (layer, j, 0))],
        out_specs=pl.BlockSpec((tm, tn), lambda i, j: (i, j)),
        out_shape=jax.ShapeDtypeStruct((s, n), jnp.float32),
        compiler_params=_cparams(("parallel", "arbitrary")),
        name="in_proj",
    )(h, wt_all)


def _band_kernel(tab_ref, o_ref, *, pad, dil, half, col0, key_axis):
    h = pl.program_id(0)
    rows, width = o_ref.shape
    col = lax.broadcasted_iota(jnp.int32, (rows, width), 1)
    row = lax.broadcasted_iota(jnp.int32, (rows, width), 0) + pl.program_id(1) * rows
    rel_u = (col - row if key_axis == 1 else row - col) - pad
    rel = rel_u * dil
    n = jnp.abs(rel)
    nb = NUM_BUCKETS // 2
    max_exact = nb // 2
    large = jnp.full((rows, width), max_exact, jnp.int32)
    for t in BUCKET_STEPS:
        large = large + jnp.where(n >= t, 1, 0)
    bucket = jnp.where(rel > 0, nb, 0) + jnp.where(n < max_exact, n, large)
    val = jnp.zeros((rows, width), jnp.float32)
    for b in range(NUM_BUCKETS):
        val = jnp.where(bucket == b, tab_ref[b, col0 + h], val)
    val = val * LOG2E
    if half is not None:
        val = jnp.where(jnp.abs(rel_u) <= half, val, NEG)
    o_ref[...] = val


def _band(rel_bias, heads, rows, width, rblk, *, pad, dil, half, col0, key_axis, name):
    return pl.pallas_call(
        functools.partial(_band_kernel, pad=pad, dil=dil, half=half, col0=col0,
                          key_axis=key_axis),
        grid=(heads, rows // rblk),
        in_specs=[pl.BlockSpec(memory_space=pltpu.SMEM)],
        out_specs=pl.BlockSpec((None, rblk, width), lambda h, j: (h, j, 0)),
        out_shape=jax.ShapeDtypeStruct((heads, rows, width), jnp.float32),
        compiler_params=_cparams(("parallel", "arbitrary")),
        name=name,
    )(rel_bias)


def _lane_lo(shape):
    return (lax.broadcasted_iota(jnp.int32, shape, len(shape) - 1) % LANES) < HEAD_DIM


def _rope(x, c, sa, sb):
    return x * c + pltpu.roll(x, LANES - 16, 1) * sa + pltpu.roll(x, 16, 1) * sb


def _halfnorm(x, gain, lo):
    sq = x * x
    s_lo = jnp.sum(jnp.where(lo, sq, 0.0), axis=-1, keepdims=True)
    s_hi = jnp.sum(jnp.where(lo, 0.0, sq), axis=-1, keepdims=True)
    r = jnp.where(lo, lax.rsqrt(s_lo * (1.0 / HEAD_DIM) + EPS),
                  lax.rsqrt(s_hi * (1.0 / HEAD_DIM) + EPS))
    return x * r * gain


ONES_ROWS = 16


def _prep_a_kernel(p_ref, qn_ref, kvn_ref, wuq_ref, wukv_ref, gq_ref, gk_ref,
                   c_ref, sa_ref, sb_ref, qt_out, k_out, vt_out):
    p = p_ref[...]
    cq = p[:, :Q_LORA]
    cq = cq * lax.rsqrt(jnp.mean(cq * cq, axis=-1, keepdims=True) + EPS) * qn_ref[...]
    q = jnp.dot(cq.astype(jnp.bfloat16), wuq_ref[...], preferred_element_type=jnp.float32)
    ckv = p[:, Q_LORA:Q_LORA + KV_LORA]
    ckv = ckv * lax.rsqrt(jnp.mean(ckv * ckv, axis=-1, keepdims=True) + EPS) * kvn_ref[...]
    kvu = jnp.dot(ckv.astype(jnp.bfloat16), wukv_ref[...], preferred_element_type=jnp.float32)
    kr = pltpu.roll(p[:, Q_LORA + KV_LORA:], HEAD_DIM, 1)
    lo = _lane_lo(kr.shape)
    c, sa, sb = c_ref[...], sa_ref[...], sb_ref[...]
    scale = A_QK ** -0.5 * LOG2E
    ones = jnp.ones((ONES_ROWS, p.shape[0]), vt_out.dtype)
    for h in range(A_HEADS):
        qh = q[:, h * LANES:(h + 1) * LANES]
        ss = jnp.sum(qh * qh, axis=-1, keepdims=True) * (1.0 / A_QK)
        qh = qh * lax.rsqrt(ss + EPS) * gq_ref[...]
        qt_out[h] = (_rope(qh, c, sa, sb) * scale).T.astype(qt_out.dtype)
        kvh = kvu[:, h * LANES:(h + 1) * LANES]
        kh = jnp.where(lo, kvh, kr)
        ss = jnp.sum(kh * kh, axis=-1, keepdims=True) * (1.0 / A_QK)
        kh = kh * lax.rsqrt(ss + EPS) * gk_ref[...]
        k_out[h] = _rope(kh, c, sa, sb).astype(k_out.dtype)
        vt_out[h, :A_V, :] = kvh.T[A_NOPE:].astype(vt_out.dtype)
        vt_out[h, A_V:, :] = ones


def _prep_a(proj, qn, kvn, wuq, wukv, gq, gk, tabs, tm=512):
    s = proj.shape[0]
    row = lambda w: pl.BlockSpec((1, w), lambda i: (0, 0))
    tab = pl.BlockSpec((tm, LANES), lambda i: (i, 0))
    return pl.pallas_call(
        _prep_a_kernel,
        grid=(s // tm,),
        in_specs=[pl.BlockSpec((tm, A_BLOCK), lambda i: (i, OFF_A // A_BLOCK)),
                  row(Q_LORA), row(KV_LORA),
                  pl.BlockSpec(wuq.shape, lambda i: (0, 0)),
                  pl.BlockSpec(wukv.shape, lambda i: (0, 0)),
                  row(LANES), row(LANES), tab, tab, tab],
        out_specs=[pl.BlockSpec((A_HEADS, LANES, tm), lambda i: (0, 0, i)),
                   pl.BlockSpec((A_HEADS, tm, LANES), lambda i: (0, i, 0)),
                   pl.BlockSpec((A_HEADS, A_V + ONES_ROWS, tm), lambda i: (0, 0, i))],
        out_shape=[jax.ShapeDtypeStruct((A_HEADS, LANES, s), jnp.bfloat16),
                   jax.ShapeDtypeStruct((A_HEADS, s, LANES), jnp.bfloat16),
                   jax.ShapeDtypeStruct((A_HEADS, A_V + ONES_ROWS, s), jnp.bfloat16)],
        compiler_params=_cparams(("parallel",)),
        name="prep_a",
    )(proj, qn, kvn, wuq, wukv, gq, gk, *tabs)


B_TQ = 256


def _prep_b_kernel(q_ref, kv_ref, gq_ref, gk_ref, c_ref, sa_ref, sb_ref,
                   qt_out, k_out, vt_out):
    c, sa, sb = c_ref[...], sa_ref[...], sb_ref[...]
    lo = _lane_lo(c.shape)
    tm = c.shape[0]
    scale = HEAD_DIM ** -0.5 * LOG2E
    q = q_ref[...]
    for b in range(B_HEADS // 2):
        x = _rope(_halfnorm(q[:, b * LANES:(b + 1) * LANES], gq_ref[...], lo), c, sa, sb) * scale
        xr = pltpu.roll(x, HEAD_DIM, 1)
        g = b // 2
        even, odd = (x, xr) if g == 0 else (xr, x)
        keep = lo if g == 0 else jnp.logical_not(lo)
        for j, xh in enumerate((even, odd)):
            hh = 2 * (b % 2) + j
            qt_out[g, :, hh * tm:(hh + 1) * tm] = jnp.where(keep, xh, 0.0).T.astype(qt_out.dtype)
    kv = kv_ref[...]
    k_out[...] = _rope(_halfnorm(kv[:, :LANES], gk_ref[...], lo), c, sa, sb).astype(k_out.dtype)
    vt = kv[:, LANES:].T.astype(vt_out.dtype)
    ones = jnp.ones((ONES_ROWS, tm), vt_out.dtype)
    for g in range(B_KV_HEADS):
        vt_out[g, :HEAD_DIM, :] = vt[g * HEAD_DIM:(g + 1) * HEAD_DIM]
        vt_out[g, HEAD_DIM:, :] = ones


def _prep_b(proj, gq, gk, tabs, tm=B_TQ):
    s = proj.shape[0]
    row = pl.BlockSpec((1, LANES), lambda i: (0, 0))
    tab = pl.BlockSpec((tm, LANES), lambda i: (i, 0))
    rv = HEAD_DIM + ONES_ROWS
    return pl.pallas_call(
        _prep_b_kernel,
        grid=(s // tm,),
        in_specs=[pl.BlockSpec((tm, 512), lambda i: (i, OFF_BQ // 512)),
                  pl.BlockSpec((tm, 256), lambda i: (i, OFF_BKV // 256)),
                  row, row, tab, tab, tab],
        out_specs=[pl.BlockSpec((2, None, LANES, 4 * tm), lambda i: (0, i, 0, 0)),
                   pl.BlockSpec((tm, LANES), lambda i: (i, 0)),
                   pl.BlockSpec((2, rv, tm), lambda i: (0, 0, i))],
        out_shape=[jax.ShapeDtypeStruct((2, s // tm, LANES, 4 * tm), jnp.bfloat16),
                   jax.ShapeDtypeStruct((s, LANES), jnp.bfloat16),
                   jax.ShapeDtypeStruct((2, rv, s), jnp.bfloat16)],
        compiler_params=_cparams(("parallel",)),
        name="prep_b",
    )(proj, proj, gq, gk, *tabs)


def _prep_d_kernel(q_ref, k_ref, v_ref, gq_ref, gk_ref, qt_out, k_out, vt_out):
    q, k, v = q_ref[...], k_ref[...], v_ref[...]
    tm = q.shape[0]
    lo = _lane_lo((tm, LANES))
    scale = HEAD_DIM ** -0.5 * LOG2E
    ones = jnp.ones((ONES_ROWS, tm), vt_out.dtype)
    for h in range(D_HEADS):
        cols = slice(h * LANES, (h + 1) * LANES)
        x = _halfnorm(q[:, cols], gq_ref[...], lo) * scale
        qt_out[h, :, :tm] = jnp.where(lo, x, 0.0).T.astype(qt_out.dtype)
        qt_out[h, :, tm:] = jnp.where(lo, 0.0, x).T.astype(qt_out.dtype)
        k_out[:, cols] = _halfnorm(k[:, cols], gk_ref[...], lo).astype(k_out.dtype)
        vt_out[h, :D_V, :] = v[:, cols].T.astype(vt_out.dtype)
        vt_out[h, D_V:, :] = ones


def _prep_d(proj, gq, gk, tm=D_TQ):
    s = proj.shape[0]
    row = pl.BlockSpec((1, LANES), lambda i: (0, 0))
    blk = lambda off: pl.BlockSpec((tm, 512), lambda i: (i, off // 512))
    rv = D_V + ONES_ROWS
    return pl.pallas_call(
        _prep_d_kernel,
        grid=(s // tm,),
        in_specs=[blk(OFF_DQ), blk(OFF_DK), blk(OFF_DV), row, row],
        out_specs=[pl.BlockSpec((D_HEADS, None, LANES, 2 * tm), lambda i: (0, i, 0, 0)),
                   pl.BlockSpec((tm, 512), lambda i: (i, 0)),
                   pl.BlockSpec((D_HEADS, rv, tm), lambda i: (0, 0, i))],
        out_shape=[jax.ShapeDtypeStruct((D_HEADS, s // tm, LANES, 2 * tm), jnp.bfloat16),
                   jax.ShapeDtypeStruct((s, 512), jnp.bfloat16),
                   jax.ShapeDtypeStruct((D_HEADS, rv, s), jnp.bfloat16)],
        compiler_params=_cparams(("parallel",)),
        name="prep_d",
    )(proj, proj, proj, gq, gk)


def _prep_c_kernel(q_ref, k_ref, v_ref, gq_ref, gk_ref, q_out, kt_out, v_out, *, dil, ut):
    lo = _lane_lo((ut, LANES))
    scale = HEAD_DIM ** -0.5 * LOG2E
    for r in range(dil):
        rows = pl.ds(r, ut, stride=dil) if dil > 1 else pl.ds(0, ut)
        q_out[r] = (_halfnorm(q_ref[rows, :], gq_ref[...], lo) * scale).astype(q_out.dtype)
        kt_out[r] = _halfnorm(k_ref[rows, :], gk_ref[...], lo).T.astype(kt_out.dtype)
        v_out[r] = v_ref[rows, :].astype(v_out.dtype)


def _prep_c(proj, gq, gk, g, ut):
    s = proj.shape[0]
    dil = C_CONFIGS[g][1]
    seg = s // dil
    nb = C_HEADS // 2
    tn = ut * dil
    blk = lambda off: pl.BlockSpec((tn, LANES), lambda i, b: (i, off // LANES + g * nb + b))
    row = pl.BlockSpec((1, LANES), lambda i, b: (0, 0))
    return pl.pallas_call(
        functools.partial(_prep_c_kernel, dil=dil, ut=ut),
        grid=(s // tn, nb),
        in_specs=[blk(OFF_CQ), blk(OFF_CK), blk(OFF_CV), row, row],
        out_specs=[pl.BlockSpec((dil, ut, LANES), lambda i, b: (0, i, b)),
                   pl.BlockSpec((dil, LANES, ut), lambda i, b: (0, b, i)),
                   pl.BlockSpec((dil, ut, LANES), lambda i, b: (0, i, b))],
        out_shape=[jax.ShapeDtypeStruct((dil, seg, nb * LANES), jnp.bfloat16),
                   jax.ShapeDtypeStruct((dil, nb * LANES, seg), jnp.bfloat16),
                   jax.ShapeDtypeStruct((dil, seg, nb * LANES), jnp.bfloat16)],
        compiler_params=_cparams(("parallel", "parallel")),
        name=f"prep_c{g}",
    )(proj, proj, proj, gq, gk)


def _scores(k, qt, s_ref, mc_ref, slot, bias=None):
    st = jnp.dot(k, qt, preferred_element_type=jnp.float32)
    if bias is not None:
        st = st + bias
    s_ref[slot] = st
    mc_ref[slot] = jnp.max(st, axis=0, keepdims=True)


def _consume(s_ref, mc_ref, slot, vt, m_ref, acc_ref):
    m_prev = m_ref[...]
    m_next = jnp.maximum(m_prev, mc_ref[slot])
    p = jnp.exp2(s_ref[slot] - m_next)
    alpha = jnp.exp2(m_prev - m_next)
    m_ref[...] = m_next
    pv = jnp.dot(vt, p.astype(jnp.bfloat16), preferred_element_type=jnp.float32)
    acc_ref[...] = alpha * acc_ref[...] + pv


FLASH_UNROLL = 4


def _flash_loop(nk, score, consume):
    u = min(FLASH_UNROLL, nk)
    assert nk % u == 0 and u % 2 == 0
    score(0, 0)

    def body(j, carry):
        for i in range(u):
            score(j * u + i + 1, (i + 1) % 2)
            consume(j * u + i, i % 2)
        return carry

    lax.fori_loop(0, nk // u - 1, body, 0)
    for c in range(nk - u, nk):
        if c + 1 < nk:
            score(c + 1, (c + 1) % 2)
        consume(c, c % 2)


def _init_stats(m_ref, acc_ref):
    m_ref[...] = jnp.full_like(m_ref, -jnp.inf)
    acc_ref[...] = jnp.zeros_like(acc_ref)


def _normalised(acc_ref, rv):
    acc = acc_ref[...]
    return acc[:rv] / acc[rv:rv + 1]


def _flash_scratch(streams, tk, n, rv):
    return ([pltpu.VMEM((2, tk, n), jnp.float32)] * streams
            + [pltpu.VMEM((2, 1, n), jnp.float32)] * streams
            + [pltpu.VMEM((1, n), jnp.float32)] * streams
            + [pltpu.VMEM((rv, n), jnp.float32)] * streams)


def _attn_a_kernel(qt_ref, k_ref, vt_ref, o_ref, s0_ref, s1_ref, mc0_ref, mc1_ref,
                   m0_ref, m1_ref, acc0_ref, acc1_ref, *, tk, nk):
    streams = ((s0_ref, mc0_ref, m0_ref, acc0_ref), (s1_ref, mc1_ref, m1_ref, acc1_ref))
    for _, _, m_ref, acc_ref in streams:
        _init_stats(m_ref, acc_ref)

    def score(c, slot):
        k0 = pl.multiple_of(c * tk, tk)
        for hh, (s_ref, mc_ref, _, _) in enumerate(streams):
            _scores(k_ref[hh, pl.ds(k0, tk), :], qt_ref[hh], s_ref, mc_ref, slot)

    def consume(c, slot):
        k0 = pl.multiple_of(c * tk, tk)
        for hh, (s_ref, mc_ref, m_ref, acc_ref) in enumerate(streams):
            _consume(s_ref, mc_ref, slot, vt_ref[hh, :, pl.ds(k0, tk)], m_ref, acc_ref)

    _flash_loop(nk, score, consume)
    o_ref[...] = jnp.concatenate([_normalised(acc0_ref, A_V), _normalised(acc1_ref, A_V)],
                                 axis=0).T


def _attn_a(qt, k, vt, tq=512, tk=512):
    s = k.shape[1]
    tk = min(tk, s // 2)
    rv = vt.shape[1]
    return pl.pallas_call(
        functools.partial(_attn_a_kernel, tk=tk, nk=s // tk),
        grid=(A_HEADS // 2, s // tq),
        in_specs=[pl.BlockSpec((2, LANES, tq), lambda p, i: (p, 0, i)),
                  pl.BlockSpec((2, s, LANES), lambda p, i: (p, 0, 0)),
                  pl.BlockSpec((2, rv, s), lambda p, i: (p, 0, 0))],
        out_specs=pl.BlockSpec((tq, LANES), lambda p, i: (i, p)),
        out_shape=jax.ShapeDtypeStruct((s, BRANCH_W), jnp.float32),
        scratch_shapes=_flash_scratch(2, tk, tq, rv),
        compiler_params=_cparams(("parallel", "arbitrary")),
        name="attn_a",
    )(qt, k, vt)


def _attn_b_kernel(qt_ref, k_ref, vt_ref, o_ref, s_ref, mc_ref, m_ref, acc_ref, *, tq, tk, nk):
    _init_stats(m_ref, acc_ref)

    def score(c, slot):
        _scores(k_ref[pl.ds(pl.multiple_of(c * tk, tk), tk), :], qt_ref[...], s_ref, mc_ref, slot)

    def consume(c, slot):
        _consume(s_ref, mc_ref, slot, vt_ref[:, pl.ds(pl.multiple_of(c * tk, tk), tk)],
                 m_ref, acc_ref)

    _flash_loop(nk, score, consume)
    ot = _normalised(acc_ref, HEAD_DIM)
    o_ref[...] = jnp.concatenate([ot[:, h * tq:(h + 1) * tq] for h in range(4)], axis=0).T


def _attn_b(qt, k, vt, tq=B_TQ, tk=512):
    s = k.shape[0]
    tk = min(tk, s // 2)
    rv = vt.shape[1]
    return pl.pallas_call(
        functools.partial(_attn_b_kernel, tq=tq, tk=tk, nk=s // tk),
        grid=(2, s // tq),
        in_specs=[pl.BlockSpec((None, None, LANES, 4 * tq), lambda g, i: (g, i, 0, 0)),
                  pl.BlockSpec((s, LANES), lambda g, i: (0, 0)),
                  pl.BlockSpec((None, rv, s), lambda g, i: (g, 0, 0))],
        out_specs=pl.BlockSpec((tq, 2 * LANES), lambda g, i: (i, g)),
        out_shape=jax.ShapeDtypeStruct((s, BRANCH_W), jnp.float32),
        scratch_shapes=_flash_scratch(1, tk, 4 * tq, rv),
        compiler_params=_cparams(("parallel", "arbitrary")),
        name="attn_b",
    )(qt, k, vt)


def _attn_d_kernel(qt_ref, k_ref, vt_ref, band_ref, lam_ref, sub_ref, o_ref,
                   s_ref, mc_ref, m_ref, acc_ref, *, tq, tk, nk, lambda_init):
    _init_stats(m_ref, acc_ref)
    q0 = pl.program_id(1) * tq
    hi = tq + 2 * D_BAND - tk

    def score(c, slot):
        k0 = pl.multiple_of(c * tk, tk)
        b0 = pl.multiple_of(jnp.clip(k0 - q0 + D_BAND, 0, hi), LANES)
        bias = band_ref[pl.ds(b0, tk), :]
        _scores(k_ref[pl.ds(k0, tk), :], qt_ref[...], s_ref, mc_ref, slot,
                jnp.concatenate([bias, bias], axis=1))

    def consume(c, slot):
        _consume(s_ref, mc_ref, slot, vt_ref[:, pl.ds(pl.multiple_of(c * tk, tk), tk)],
                 m_ref, acc_ref)

    _flash_loop(nk, score, consume)
    lv = lam_ref[...]
    lam = (jnp.exp(jnp.sum(lv[0:1] * lv[1:2], axis=-1, keepdims=True))
           - jnp.exp(jnp.sum(lv[2:3] * lv[3:4], axis=-1, keepdims=True)) + lambda_init)
    ot = _normalised(acc_ref, D_V)
    o = (ot[:, :tq] - lam * ot[:, tq:]).T
    o = o * lax.rsqrt(jnp.mean(o * o, axis=-1, keepdims=True) + EPS) * sub_ref[...]
    o_ref[...] = o * (1.0 - lambda_init)


def _attn_d(qt, k, vt, band, lam_vecs, subw, lambda_init, tq=D_TQ, tk=D_TK):
    s = k.shape[0]
    rv = vt.shape[1]
    return pl.pallas_call(
        functools.partial(_attn_d_kernel, tq=tq, tk=tk, nk=s // tk, lambda_init=lambda_init),
        grid=(D_HEADS, s // tq),
        in_specs=[pl.BlockSpec((None, None, LANES, 2 * tq), lambda h, i: (h, i, 0, 0)),
                  pl.BlockSpec((s, LANES), lambda h, i: (0, h)),
                  pl.BlockSpec((None, rv, s), lambda h, i: (h, 0, 0)),
                  pl.BlockSpec((None, tq + 2 * D_BAND, tq), lambda h, i: (h, 0, 0)),
                  pl.BlockSpec((4, HEAD_DIM), lambda h, i: (0, 0)),
                  pl.BlockSpec((1, LANES), lambda h, i: (0, 0))],
        out_specs=pl.BlockSpec((tq, LANES), lambda h, i: (i, h)),
        out_shape=jax.ShapeDtypeStruct((s, BRANCH_W), jnp.float32),
        scratch_shapes=_flash_scratch(1, tk, 2 * tq, rv),
        compiler_params=_cparams(("parallel", "arbitrary")),
        name="attn_d",
    )(qt, k, vt, band, lam_vecs, subw)


def _attn_c_kernel(q_ref, kt_ref, v_ref, band_ref, o_ref, lse_ref, *, nt, seg, win, unroll):
    j = pl.program_id(2)
    lo = _lane_lo((C_QT, LANES))

    def scores(t):
        r0 = pl.multiple_of(t * C_QT, C_QT)
        u0 = j * (nt * C_QT) + r0
        ws = pl.multiple_of(jnp.clip(u0 - C_QT, 0, seg - win), C_QT)
        x = pl.multiple_of(ws - u0 + 2 * C_QT, C_QT)
        q = q_ref[pl.ds(r0, C_QT), :]
        zero = jnp.zeros_like(q)
        q2 = jnp.concatenate([jnp.where(lo, q, zero), jnp.where(lo, zero, q)], axis=0)
        s = jnp.dot(q2, kt_ref[:, pl.ds(ws, win)], preferred_element_type=jnp.float32)
        s = s + jnp.concatenate([band_ref[0, :, pl.ds(x, win)], band_ref[1, :, pl.ds(x, win)]], axis=0)
        return r0, ws, s

    def finish(r0, ws, s):
        m = jnp.max(s, axis=1, keepdims=True)
        p = jnp.exp2(s - m)
        l = jnp.sum(p, axis=1, keepdims=True)
        pv = jnp.dot(p.astype(jnp.bfloat16), v_ref[pl.ds(ws, win), :],
                     preferred_element_type=jnp.float32)
        o = pv / l
        lse = m + jnp.log2(l)
        o_ref[pl.ds(r0, C_QT), :] = jnp.where(lo, o[:C_QT], o[C_QT:])
        lse_ref[pl.ds(r0, C_QT), :] = jnp.where(lo, lse[:C_QT], lse[C_QT:])

    def body(tt, carry):
        tiles = [scores(tt * unroll + i) for i in range(unroll)]
        for tile in tiles:
            finish(*tile)
        return carry

    lax.fori_loop(0, nt // unroll, body, 0)


def _attn_c(q, kt, v, band, nt_max=8):
    dil, seg, w = q.shape
    nt = min(nt_max, seg // C_QT)
    win = min(3 * C_QT, seg)
    tqb = nt * C_QT
    out = jax.ShapeDtypeStruct((dil, seg, w), jnp.float32)
    ospec = pl.BlockSpec((None, tqb, LANES), lambda r, b, j: (r, j, b))
    return pl.pallas_call(
        functools.partial(_attn_c_kernel, nt=nt, seg=seg, win=win, unroll=min(4, nt)),
        grid=(dil, C_HEADS // 2, seg // tqb),
        in_specs=[pl.BlockSpec((None, tqb, LANES), lambda r, b, j: (r, j, b)),
                  pl.BlockSpec((None, LANES, seg), lambda r, b, j: (r, b, 0)),
                  pl.BlockSpec((None, seg, LANES), lambda r, b, j: (r, 0, b)),
                  pl.BlockSpec((2, C_QT, C_BAND_W), lambda r, b, j: (b, 0, 0))],
        out_specs=[ospec, ospec],
        out_shape=[out, out],
        compiler_params=_cparams(("parallel", "parallel", "arbitrary")),
        name=f"attn_c_d{dil}",
    )(q, kt, v, band)


def _combine_c_kernel(*refs):
    o_refs, l_refs, out_ref = refs[0:2 * C_GROUPS:2], refs[1:2 * C_GROUPS:2], refs[-1]
    n = out_ref.shape[0] // C_DIL_MAX
    for r in range(C_DIL_MAX):
        outs, lses = [], []
        for (_, dil), o_ref, l_ref in zip(C_CONFIGS, o_refs, l_refs):
            rows = pl.ds(r // dil, n, stride=C_DIL_MAX // dil)
            cls = pl.ds(r % dil, 1)
            outs.append(o_ref[cls, rows, :][0])
            lses.append(l_ref[cls, rows, :][0])
        m = functools.reduce(jnp.maximum, lses)
        es = [jnp.exp2(l - m) for l in lses]
        num = sum(e * o for e, o in zip(es, outs))
        out_ref[pl.ds(r, n, stride=C_DIL_MAX), :] = num / sum(es)


def _combine_c(os, lses, s, tn=2048):
    nb = C_HEADS // 2
    args, in_specs = [], []
    for (_, dil), o, l in zip(C_CONFIGS, os, lses):
        spec = pl.BlockSpec((dil, tn // dil, LANES), lambda i, b: (0, i, b))
        args += [o, l]
        in_specs += [spec, spec]
    return pl.pallas_call(
        _combine_c_kernel,
        grid=(s // tn, nb),
        in_specs=in_specs,
        out_specs=pl.BlockSpec((tn, LANES), lambda i, b: (i, b)),
        out_shape=jax.ShapeDtypeStruct((s, nb * LANES), jnp.float32),
        compiler_params=_cparams(("parallel", "parallel")),
        name="combine_c",
    )(*args)


def _merge_kernel(x_ref, h_ref, sg_ref, ya_ref, yb_ref, yc_ref, yd_ref,
                  wmt_ref, wb_ref, wo_ref, o_ref, yg_ref):
    c = pl.program_id(1)

    @pl.when(c == 0)
    def _():
        sg = sg_ref[...]
        for n, y_ref in enumerate((ya_ref, yb_ref, yc_ref, yd_ref)):
            g = sg[:, n * BRANCH_W:(n + 1) * BRANCH_W]
            yg_ref[n] = (y_ref[...] * (g * jax.nn.sigmoid(g))).astype(yg_ref.dtype)
        o_ref[...] = x_ref[...]

    h = h_ref[...]
    mixed = 0.0
    for n in range(N_BRANCH):
        gate = jax.nn.sigmoid(_dot_nt(h, wmt_ref[n]))
        z = jnp.dot(yg_ref[n], wb_ref[n], preferred_element_type=jnp.float32)
        mixed = mixed + gate * z
    o_ref[...] += jnp.dot(mixed.astype(jnp.bfloat16), wo_ref[...],
                          preferred_element_type=jnp.float32)


def _merge(x, h, proj, ys, wmerge_t, wbranch_all, wout_all, layer, tm=512, tn=256):
    s, d = x.shape
    nc = d // tn
    ni = s // tm
    row = lambda w: pl.BlockSpec((tm, w), lambda i, c: (i, 0))

    def early(w, c_from):
        return pl.BlockSpec((tm, w), lambda i, c: (jnp.minimum(i + (c >= c_from), ni - 1), 0))

    return pl.pallas_call(
        _merge_kernel,
        grid=(ni, nc),
        in_specs=[early(d, 3), row(d), early(N_BRANCH * BRANCH_W, 1),
                  early(BRANCH_W, 5), early(BRANCH_W, 5), early(BRANCH_W, 6), early(BRANCH_W, 6),
                  pl.BlockSpec((None, N_BRANCH, tn, d), lambda i, c: (layer, 0, c, 0)),
                  pl.BlockSpec((None, N_BRANCH, BRANCH_W, tn), lambda i, c: (layer, 0, 0, c)),
                  pl.BlockSpec((None, tn, d), lambda i, c: (layer, c, 0))],
        out_specs=row(d),
        out_shape=jax.ShapeDtypeStruct((s, d), jnp.float32),
        scratch_shapes=[pltpu.VMEM((N_BRANCH, tm, BRANCH_W), jnp.bfloat16)],
        compiler_params=_cparams(("parallel", "arbitrary")),
        name="merge",
    )(x, h, proj, *ys, wmerge_t, wbranch_all, wout_all)


def _rope_cos_sin(pos, dim):
    inv = ROPE_THETA ** (-jnp.arange(0, dim, 2, dtype=jnp.float32) / dim)
    ang = pos.astype(jnp.float32)[:, None] * inv[None, :]
    return jnp.cos(ang), jnp.sin(ang)


def _tables_a(s):
    cos, sin = _rope_cos_sin(jnp.arange(s, dtype=jnp.int32), A_ROPE)
    z16, z32 = jnp.zeros((s, 16), jnp.float32), jnp.zeros((s, 32), jnp.float32)
    one = jnp.ones((s, A_NOPE), jnp.float32)
    zero = jnp.zeros((s, A_NOPE), jnp.float32)
    c = jnp.concatenate([one, cos, cos, z32], axis=1)
    sa = jnp.concatenate([zero, -sin, z16, z32], axis=1)
    sb = jnp.concatenate([zero, z16, sin, z32], axis=1)
    return c, sa, sb


def _tables_b(s):
    rows = s // GRID_W
    row_pos = jnp.repeat(jnp.arange(rows, dtype=jnp.int32), GRID_W)
    col_pos = jnp.tile(jnp.arange(GRID_W, dtype=jnp.int32), rows)
    cr, sr = _rope_cos_sin(row_pos, HEAD_DIM // 2)
    cc, sc = _rope_cos_sin(col_pos, HEAD_DIM // 2)
    z = jnp.zeros_like(sr)
    c = jnp.concatenate([cr, cr, cc, cc] * 2, axis=1)
    sa = jnp.concatenate([-sr, z, -sc, z] * 2, axis=1)
    sb = jnp.concatenate([z, sr, z, sc] * 2, axis=1)
    return c, sa, sb


def _pad_lanes(v, width=LANES):
    return jnp.pad(v, ((0, width - v.shape[0]),))[None, :]


def kernel(x, norm_w, w_in, mla_q_norm, mla_kv_norm, mla_w_uq, mla_w_ukv, mla_qk_norm,
           gqa_qk_norm, dil_qk_norm, diff_qk_norm, diff_lambda, diff_subnorm, rel_bias,
           w_branch, w_out):
    b, s, d = x.shape
    assert b == 1 and d == D_MODEL
    depth = norm_w.shape[0]
    bf = jnp.bfloat16

    wt = jnp.swapaxes(w_in, 1, 2)
    seg = lambda a, b: wt[:, a:b].astype(bf)
    w_a_t = jnp.pad(seg(SRC_A, SRC_B), ((0, 0), (0, A_BLOCK - SRC_B), (0, 0)))
    w_main_t = jnp.concatenate([seg(SRC_SILU, SRC_MERGE), seg(SRC_C, SRC_D),
                                seg(SRC_D, SRC_SILU), seg(SRC_B, SRC_C), w_a_t], axis=1)
    w_merge_t = seg(SRC_MERGE, SRC_END).reshape(depth, N_BRANCH, d, d)
    w_branch_b = w_branch.astype(bf)
    w_out_b = w_out.astype(bf)
    wuq = mla_w_uq.reshape(depth, Q_LORA, A_HEADS, A_QK)
    wuq = jnp.pad(wuq, ((0, 0), (0, 0), (0, 0), (0, LANES - A_QK)))
    wuq = wuq.reshape(depth, Q_LORA, A_HEADS * LANES).astype(bf)
    wukv = mla_w_ukv.astype(bf)

    tabs_a = _tables_a(s)
    tabs_b = _tables_b(s)
    bands_c = [_band(rel_bias, C_HEADS, C_QT, C_BAND_W, C_QT, pad=2 * C_QT, dil=dil,
                     half=C_HALF, col0=g * C_HEADS, key_axis=1, name=f"band_c{g}")
               for g, (_, dil) in enumerate(C_CONFIGS)]
    band_r = D_TQ + 2 * D_BAND
    band_d = _band(rel_bias, D_HEADS, band_r, D_TQ, band_r // 2, pad=D_BAND, dil=1, half=None,
                   col0=C_GROUPS * C_HEADS, key_axis=0, name="band_d")
    c_ut = [max(C_QT, min(s, 2048) // dil) for _, dil in C_CONFIGS]

    xs = x[0]
    for l in range(depth):
        h = _rmsnorm(xs, norm_w[l][None, :])
        proj = _in_proj(h, w_main_t, l)

        qa, kta, va = _prep_a(proj, mla_q_norm[l][None, :], mla_kv_norm[l][None, :],
                              wuq[l], wukv[l], _pad_lanes(mla_qk_norm[l, 0]),
                              _pad_lanes(mla_qk_norm[l, 1]), tabs_a)
        y_a = _attn_a(qa, kta, va)

        g2 = lambda v: jnp.tile(v, 2)[None, :]
        qb, ktb, vb = _prep_b(proj, g2(gqa_qk_norm[l, 0]), g2(gqa_qk_norm[l, 1]), tabs_b)
        y_b = _attn_b(qb, ktb, vb)

        os, lses = [], []
        for g in range(C_GROUPS):
            qg, ktg, vg = _prep_c(proj, g2(dil_qk_norm[l, 0, g]), g2(dil_qk_norm[l, 1, g]),
                                  g, c_ut[g])
            og, lg = _attn_c(qg, ktg, vg, bands_c[g])
            os.append(og)
            lses.append(lg)
        y_c = _combine_c(os, lses, s)

        qd, ktd, vd = _prep_d(proj, g2(diff_qk_norm[l, 0]), g2(diff_qk_norm[l, 1]))
        lambda_init = 0.8 - 0.6 * math.exp(-0.3 * l)
        y_d = _attn_d(qd, ktd, vd, band_d, diff_lambda[l], diff_subnorm[l][None, :], lambda_init)

        xs = _merge(xs, h, proj, (y_a, y_b, y_c, y_d), w_merge_t, w_branch_b, w_out_b, l)
    return xs[None]
```

```python
import functools
import math

import numpy as np
import jax
import jax.numpy as jnp
from jax import lax
from jax.experimental import pallas as pl
from jax.experimental.pallas import tpu as pltpu

D_MODEL = 2048
GRID_W = 64
HEAD_DIM = 64
BRANCH_W = 512
N_BRANCH = 4
ROPE_THETA = 10000.0
EPS = 1e-6
NEG = -1e30

A_HEADS = 8
A_NOPE = 64
A_ROPE = 32
A_V = 64
A_QK = A_NOPE + A_ROPE
Q_LORA = 384
KV_LORA = 128

B_HEADS = 8
B_KV_HEADS = 2

C_HEADS = 8
C_CONFIGS = ((128, 1), (512, 4), (2048, 16))
C_GROUPS = len(C_CONFIGS)

D_HEADS = 4
D_V = 2 * HEAD_DIM

NUM_BUCKETS = 32
T5_MAX_DISTANCE = 1024

LANES = 128

SRC_A, SRC_B, SRC_C, SRC_D, SRC_SILU, SRC_MERGE, SRC_END = 0, 544, 1312, 5920, 7456, 9504, 17696

OFF_SILU = 0
OFF_CQ, OFF_CK, OFF_CV = 2048, 3584, 5120
OFF_DQ, OFF_DK, OFF_DV = 6656, 7168, 7680
OFF_BQ, OFF_BKV = 8192, 8704
A_BLOCK = 640

VMEM_LIMIT = 56 * 1024 * 1024

LOG2E = math.log2(math.e)

C_QT = 128
C_HALF = 64
C_DIL_MAX = max(d for _, d in C_CONFIGS)
C_BAND_W = 5 * C_QT

D_TQ = 512
D_TK = 512
D_BAND = 1152


def _bucket_thresholds():
    nb = NUM_BUCKETS // 2
    max_exact = nb // 2
    n = np.arange(max_exact, 4 * T5_MAX_DISTANCE, dtype=np.float32)
    large = max_exact + (np.log(n / np.float32(max_exact))
                         / np.float32(math.log(T5_MAX_DISTANCE / max_exact))
                         * np.float32(nb - max_exact)).astype(np.int32)
    large = np.minimum(large, nb - 1)
    return tuple(int(n[np.argmax(large >= max_exact + k)]) for k in range(1, nb - max_exact))


BUCKET_STEPS = _bucket_thresholds()
assert D_BAND - D_TK + 1 >= BUCKET_STEPS[-1]


def _cparams(sem):
    return pltpu.CompilerParams(dimension_semantics=sem, vmem_limit_bytes=VMEM_LIMIT)


def _rmsnorm_kernel(x_ref, w_ref, o_ref):
    x = x_ref[...]
    ms = jnp.mean(x * x, axis=-1, keepdims=True)
    o_ref[...] = (x * lax.rsqrt(ms + EPS) * w_ref[...]).astype(o_ref.dtype)


def _rmsnorm(x, w, tm=512):
    s, d = x.shape
    return pl.pallas_call(
        _rmsnorm_kernel,
        grid=(s // tm,),
        in_specs=[pl.BlockSpec((tm, d), lambda i: (i, 0)),
                  pl.BlockSpec((1, d), lambda i: (0, 0))],
        out_specs=pl.BlockSpec((tm, d), lambda i: (i, 0)),
        out_shape=jax.ShapeDtypeStruct((s, d), jnp.bfloat16),
        compiler_params=_cparams(("parallel",)),
        name="rmsnorm",
    )(x, w)


def _dot_nt(a, bt):
    return lax.dot_general(a, bt, (((1,), (1,)), ((), ())), preferred_element_type=jnp.float32)


def _matmul_nt_kernel(a_ref, bt_ref, o_ref):
    o_ref[...] = _dot_nt(a_ref[...], bt_ref[...])


def _in_proj(h, wt_all, layer, tm=1024, tn=1792):
    s, d = h.shape
    n = wt_all.shape[1]
    tn = min(tn, n)
    return pl.pallas_call(
        _matmul_nt_kernel,
        grid=(s // tm, n // tn),
        in_specs=[pl.BlockSpec((tm, d), lambda i, j: (i, 0)),
                  pl.BlockSpec((None, tn, d), lambda i, j: (layer, j, 0))],
        out_specs=pl.BlockSpec((tm, tn), lambda i, j: (i, j)),
        out_shape=jax.ShapeDtypeStruct((s, n), jnp.float32),
        compiler_params=_cparams(("parallel", "arbitrary")),
        name="in_proj",
    )(h, wt_all)


def _band_kernel(tab_ref, o_ref, *, pad, dil, half, col0, key_axis):
    h = pl.program_id(0)
    rows, width = o_ref.shape
    col = lax.broadcasted_iota(jnp.int32, (rows, width), 1)
    row = lax.broadcasted_iota(jnp.int32, (rows, width), 0) + pl.program_id(1) * rows
    rel_u = (col - row if key_axis == 1 else row - col) - pad
    rel = rel_u * dil
    n = jnp.abs(rel)
    nb = NUM_BUCKETS // 2
    max_exact = nb // 2
    large = jnp.full((rows, width), max_exact, jnp.int32)
    for t in BUCKET_STEPS:
        large = large + jnp.where(n >= t, 1, 0)
    bucket = jnp.where(rel > 0, nb, 0) + jnp.where(n < max_exact, n, large)
    val = jnp.zeros((rows, width), jnp.float32)
    for b in range(NUM_BUCKETS):
        val = jnp.where(bucket == b, tab_ref[b, col0 + h], val)
    val = val * LOG2E
    if half is not None:
        val = jnp.where(jnp.abs(rel_u) <= half, val, NEG)
    o_ref[...] = val


def _band(rel_bias, heads, rows, width, rblk, *, pad, dil, half, col0, key_axis, name):
    return pl.pallas_call(
        functools.partial(_band_kernel, pad=pad, dil=dil, half=half, col0=col0,
                          key_axis=key_axis),
        grid=(heads, rows // rblk),
        in_specs=[pl.BlockSpec(memory_space=pltpu.SMEM)],
        out_specs=pl.BlockSpec((None, rblk, width), lambda h, j: (h, j, 0)),
        out_shape=jax.ShapeDtypeStruct((heads, rows, width), jnp.float32),
        compiler_params=_cparams(("parallel", "arbitrary")),
        name=name,
    )(rel_bias)


def _lane_lo(shape):
    return (lax.broadcasted_iota(jnp.int32, shape, len(shape) - 1) % LANES) < HEAD_DIM


def _rope(x, c, sa, sb):
    return x * c + pltpu.roll(x, LANES - 16, 1) * sa + pltpu.roll(x, 16, 1) * sb


def _halfnorm(x, gain, lo):
    sq = x * x
    s_lo = jnp.sum(jnp.where(lo, sq, 0.0), axis=-1, keepdims=True)
    s_hi = jnp.sum(jnp.where(lo, 0.0, sq), axis=-1, keepdims=True)
    r = jnp.where(lo, lax.rsqrt(s_lo * (1.0 / HEAD_DIM) + EPS),
                  lax.rsqrt(s_hi * (1.0 / HEAD_DIM) + EPS))
    return x * r * gain


ONES_ROWS = 16


def _prep_a_kernel(p_ref, qn_ref, kvn_ref, wuq_ref, wukv_ref, gq_ref, gk_ref,
                   c_ref, sa_ref, sb_ref, qt_out, k_out, vt_out):
    p = p_ref[...]
    cq = p[:, :Q_LORA]
    cq = cq * lax.rsqrt(jnp.mean(cq * cq, axis=-1, keepdims=True) + EPS) * qn_ref[...]
    q = jnp.dot(cq.astype(jnp.bfloat16), wuq_ref[...], preferred_element_type=jnp.float32)
    ckv = p[:, Q_LORA:Q_LORA + KV_LORA]
    ckv = ckv * lax.rsqrt(jnp.mean(ckv * ckv, axis=-1, keepdims=True) + EPS) * kvn_ref[...]
    kvu = jnp.dot(ckv.astype(jnp.bfloat16), wukv_ref[...], preferred_element_type=jnp.float32)
    kr = pltpu.roll(p[:, Q_LORA + KV_LORA:], HEAD_DIM, 1)
    lo = _lane_lo(kr.shape)
    c, sa, sb = c_ref[...], sa_ref[...], sb_ref[...]
    scale = A_QK ** -0.5 * LOG2E
    ones = jnp.ones((ONES_ROWS, p.shape[0]), vt_out.dtype)
    for h in range(A_HEADS):
        qh = q[:, h * LANES:(h + 1) * LANES]
        ss = jnp.sum(qh * qh, axis=-1, keepdims=True) * (1.0 / A_QK)
        qh = qh * lax.rsqrt(ss + EPS) * gq_ref[...]
        qt_out[h] = (_rope(qh, c, sa, sb) * scale).T.astype(qt_out.dtype)
        kvh = kvu[:, h * LANES:(h + 1) * LANES]
        kh = jnp.where(lo, kvh, kr)
        ss = jnp.sum(kh * kh, axis=-1, keepdims=True) * (1.0 / A_QK)
        kh = kh * lax.rsqrt(ss + EPS) * gk_ref[...]
        k_out[h] = _rope(kh, c, sa, sb).astype(k_out.dtype)
        vt_out[h, :A_V, :] = kvh.T[A_NOPE:].astype(vt_out.dtype)
        vt_out[h, A_V:, :] = ones


def _prep_a(proj, qn, kvn, wuq, wukv, gq, gk, tabs, tm=512):
    s = proj.shape[0]
    row = lambda w: pl.BlockSpec((1, w), lambda i: (0, 0))
    tab = pl.BlockSpec((tm, LANES), lambda i: (i, 0))
    return pl.pallas_call(
        _prep_a_kernel,
        grid=(s // tm,),
        in_specs=[pl.BlockSpec((tm, A_BLOCK), lambda i: (i, 0)),
                  row(Q_LORA), row(KV_LORA),
                  pl.BlockSpec(wuq.shape, lambda i: (0, 0)),
                  pl.BlockSpec(wukv.shape, lambda i: (0, 0)),
                  row(LANES), row(LANES), tab, tab, tab],
        out_specs=[pl.BlockSpec((A_HEADS, LANES, tm), lambda i: (0, 0, i)),
                   pl.BlockSpec((A_HEADS, tm, LANES), lambda i: (0, i, 0)),
                   pl.BlockSpec((A_HEADS, A_V + ONES_ROWS, tm), lambda i: (0, 0, i))],
        out_shape=[jax.ShapeDtypeStruct((A_HEADS, LANES, s), jnp.bfloat16),
                   jax.ShapeDtypeStruct((A_HEADS, s, LANES), jnp.bfloat16),
                   jax.ShapeDtypeStruct((A_HEADS, A_V + ONES_ROWS, s), jnp.bfloat16)],
        compiler_params=_cparams(("parallel",)),
        name="prep_a",
    )(proj, qn, kvn, wuq, wukv, gq, gk, *tabs)


B_TQ = 256


def _prep_b_kernel(q_ref, kv_ref, gq_ref, gk_ref, c_ref, sa_ref, sb_ref,
                   qt_out, k_out, vt_out):
    c, sa, sb = c_ref[...], sa_ref[...], sb_ref[...]
    lo = _lane_lo(c.shape)
    tm = c.shape[0]
    scale = HEAD_DIM ** -0.5 * LOG2E
    q = q_ref[...]
    for b in range(B_HEADS // 2):
        x = _rope(_halfnorm(q[:, b * LANES:(b + 1) * LANES], gq_ref[...], lo), c, sa, sb) * scale
        xr = pltpu.roll(x, HEAD_DIM, 1)
        g = b // 2
        even, odd = (x, xr) if g == 0 else (xr, x)
        keep = lo if g == 0 else jnp.logical_not(lo)
        for j, xh in enumerate((even, odd)):
            hh = 2 * (b % 2) + j
            qt_out[g, :, hh * tm:(hh + 1) * tm] = jnp.where(keep, xh, 0.0).T.astype(qt_out.dtype)
    kv = kv_ref[...]
    k_out[...] = _rope(_halfnorm(kv[:, :LANES], gk_ref[...], lo), c, sa, sb).astype(k_out.dtype)
    vt = kv[:, LANES:].T.astype(vt_out.dtype)
    ones = jnp.ones((ONES_ROWS, tm), vt_out.dtype)
    for g in range(B_KV_HEADS):
        vt_out[g, :HEAD_DIM, :] = vt[g * HEAD_DIM:(g + 1) * HEAD_DIM]
        vt_out[g, HEAD_DIM:, :] = ones


def _prep_b(proj, gq, gk, tabs, tm=B_TQ):
    s = proj.shape[0]
    row = pl.BlockSpec((1, LANES), lambda i: (0, 0))
    tab = pl.BlockSpec((tm, LANES), lambda i: (i, 0))
    rv = HEAD_DIM + ONES_ROWS
    return pl.pallas_call(
        _prep_b_kernel,
        grid=(s // tm,),
        in_specs=[pl.BlockSpec((tm, 512), lambda i: (i, OFF_BQ // 512)),
                  pl.BlockSpec((tm, 256), lambda i: (i, OFF_BKV // 256)),
                  row, row, tab, tab, tab],
        out_specs=[pl.BlockSpec((2, None, LANES, 4 * tm), lambda i: (0, i, 0, 0)),
                   pl.BlockSpec((tm, LANES), lambda i: (i, 0)),
                   pl.BlockSpec((2, rv, tm), lambda i: (0, 0, i))],
        out_shape=[jax.ShapeDtypeStruct((2, s // tm, LANES, 4 * tm), jnp.bfloat16),
                   jax.ShapeDtypeStruct((s, LANES), jnp.bfloat16),
                   jax.ShapeDtypeStruct((2, rv, s), jnp.bfloat16)],
        compiler_params=_cparams(("parallel",)),
        name="prep_b",
    )(proj, proj, gq, gk, *tabs)


def _prep_d_kernel(q_ref, k_ref, v_ref, gq_ref, gk_ref, qt_out, k_out, vt_out):
    q, k, v = q_ref[...], k_ref[...], v_ref[...]
    tm = q.shape[0]
    lo = _lane_lo((tm, LANES))
    scale = HEAD_DIM ** -0.5 * LOG2E
    ones = jnp.ones((ONES_ROWS, tm), vt_out.dtype)
    for h in range(D_HEADS):
        cols = slice(h * LANES, (h + 1) * LANES)
        x = _halfnorm(q[:, cols], gq_ref[...], lo) * scale
        qt_out[h, :, :tm] = jnp.where(lo, x, 0.0).T.astype(qt_out.dtype)
        qt_out[h, :, tm:] = jnp.where(lo, 0.0, x).T.astype(qt_out.dtype)
        k_out[:, cols] = _halfnorm(k[:, cols], gk_ref[...], lo).astype(k_out.dtype)
        vt_out[h, :D_V, :] = v[:, cols].T.astype(vt_out.dtype)
        vt_out[h, D_V:, :] = ones


def _prep_d(proj, gq, gk, tm=D_TQ):
    s = proj.shape[0]
    row = pl.BlockSpec((1, LANES), lambda i: (0, 0))
    blk = lambda off: pl.BlockSpec((tm, 512), lambda i: (i, off // 512))
    rv = D_V + ONES_ROWS
    return pl.pallas_call(
        _prep_d_kernel,
        grid=(s // tm,),
        in_specs=[blk(OFF_DQ), blk(OFF_DK), blk(OFF_DV), row, row],
        out_specs=[pl.BlockSpec((D_HEADS, None, LANES, 2 * tm), lambda i: (0, i, 0, 0)),
                   pl.BlockSpec((tm, 512), lambda i: (i, 0)),
                   pl.BlockSpec((D_HEADS, rv, tm), lambda i: (0, 0, i))],
        out_shape=[jax.ShapeDtypeStruct((D_HEADS, s // tm, LANES, 2 * tm), jnp.bfloat16),
                   jax.ShapeDtypeStruct((s, 512), jnp.bfloat16),
                   jax.ShapeDtypeStruct((D_HEADS, rv, s), jnp.bfloat16)],
        compiler_params=_cparams(("parallel",)),
        name="prep_d",
    )(proj, proj, proj, gq, gk)


def _prep_c_kernel(q_ref, k_ref, v_ref, gq_ref, gk_ref, q_out, kt_out, v_out, *, dil, ut):
    lo = _lane_lo((ut, LANES))
    scale = HEAD_DIM ** -0.5 * LOG2E
    for r in range(dil):
        rows = pl.ds(r, ut, stride=dil) if dil > 1 else pl.ds(0, ut)
        q_out[r] = (_halfnorm(q_ref[rows, :], gq_ref[...], lo) * scale).astype(q_out.dtype)
        kt_out[r] = _halfnorm(k_ref[rows, :], gk_ref[...], lo).T.astype(kt_out.dtype)
        v_out[r] = v_ref[rows, :].astype(v_out.dtype)


def _prep_c(proj, gq, gk, g, ut):
    s = proj.shape[0]
    dil = C_CONFIGS[g][1]
    seg = s // dil
    nb = C_HEADS // 2
    tn = ut * dil
    blk = lambda off: pl.BlockSpec((tn, LANES), lambda i, b: (i, off // LANES + g * nb + b))
    row = pl.BlockSpec((1, LANES), lambda i, b: (0, 0))
    return pl.pallas_call(
        functools.partial(_prep_c_kernel, dil=dil, ut=ut),
        grid=(s // tn, nb),
        in_specs=[blk(OFF_CQ), blk(OFF_CK), blk(OFF_CV), row, row],
        out_specs=[pl.BlockSpec((dil, ut, LANES), lambda i, b: (0, i, b)),
                   pl.BlockSpec((dil, LANES, ut), lambda i, b: (0, b, i)),
                   pl.BlockSpec((dil, ut, LANES), lambda i, b: (0, i, b))],
        out_shape=[jax.ShapeDtypeStruct((dil, seg, nb * LANES), jnp.bfloat16),
                   jax.ShapeDtypeStruct((dil, nb * LANES, seg), jnp.bfloat16),
                   jax.ShapeDtypeStruct((dil, seg, nb * LANES), jnp.bfloat16)],
        compiler_params=_cparams(("parallel", "parallel")),
        name=f"prep_c{g}",
    )(proj, proj, proj, gq, gk)


def _scores(k, qt, s_ref, mc_ref, slot, bias=None):
    st = jnp.dot(k, qt, preferred_element_type=jnp.float32)
    if bias is not None:
        st = st + bias
    s_ref[slot] = st
    mc_ref[slot] = jnp.max(st, axis=0, keepdims=True)


def _consume(s_ref, mc_ref, slot, vt, m_ref, acc_ref):
    m_prev = m_ref[...]
    m_next = jnp.maximum(m_prev, mc_ref[slot])
    p = jnp.exp2(s_ref[slot] - m_next)
    alpha = jnp.exp2(m_prev - m_next)
    m_ref[...] = m_next
    pv = jnp.dot(vt, p.astype(jnp.bfloat16), preferred_element_type=jnp.float32)
    acc_ref[...] = alpha * acc_ref[...] + pv


FLASH_UNROLL = 4


def _flash_loop(nk, score, consume):
    u = min(FLASH_UNROLL, nk)
    assert nk % u == 0 and u % 2 == 0
    score(0, 0)

    def body(j, carry):
        for i in range(u):
            score(j * u + i + 1, (i + 1) % 2)
            consume(j * u + i, i % 2)
        return carry

    lax.fori_loop(0, nk // u - 1, body, 0)
    for c in range(nk - u, nk):
        if c + 1 < nk:
            score(c + 1, (c + 1) % 2)
        consume(c, c % 2)


def _init_stats(m_ref, acc_ref):
    m_ref[...] = jnp.full_like(m_ref, -jnp.inf)
    acc_ref[...] = jnp.zeros_like(acc_ref)


def _normalised(acc_ref, rv):
    acc = acc_ref[...]
    return acc[:rv] / acc[rv:rv + 1]


def _flash_scratch(streams, tk, n, rv):
    return ([pltpu.VMEM((2, tk, n), jnp.float32)] * streams
            + [pltpu.VMEM((2, 1, n), jnp.float32)] * streams
            + [pltpu.VMEM((1, n), jnp.float32)] * streams
            + [pltpu.VMEM((rv, n), jnp.float32)] * streams)


def _attn_a_kernel(qt_ref, k_ref, vt_ref, o_ref, s0_ref, s1_ref, mc0_ref, mc1_ref,
                   m0_ref, m1_ref, acc0_ref, acc1_ref, *, tk, nk):
    streams = ((s0_ref, mc0_ref, m0_ref, acc0_ref), (s1_ref, mc1_ref, m1_ref, acc1_ref))
    for _, _, m_ref, acc_ref in streams:
        _init_stats(m_ref, acc_ref)

    def score(c, slot):
        k0 = pl.multiple_of(c * tk, tk)
        for hh, (s_ref, mc_ref, _, _) in enumerate(streams):
            _scores(k_ref[hh, pl.ds(k0, tk), :], qt_ref[hh], s_ref, mc_ref, slot)

    def consume(c, slot):
        k0 = pl.multiple_of(c * tk, tk)
        for hh, (s_ref, mc_ref, m_ref, acc_ref) in enumerate(streams):
            _consume(s_ref, mc_ref, slot, vt_ref[hh, :, pl.ds(k0, tk)], m_ref, acc_ref)

    _flash_loop(nk, score, consume)
    o_ref[...] = jnp.concatenate([_normalised(acc0_ref, A_V), _normalised(acc1_ref, A_V)],
                                 axis=0).T


def _attn_a(qt, k, vt, tq=512, tk=512):
    s = k.shape[1]
    tk = min(tk, s // 2)
    rv = vt.shape[1]
    return pl.pallas_call(
        functools.partial(_attn_a_kernel, tk=tk, nk=s // tk),
        grid=(A_HEADS // 2, s // tq),
        in_specs=[pl.BlockSpec((2, LANES, tq), lambda p, i: (p, 0, i)),
                  pl.BlockSpec((2, s, LANES), lambda p, i: (p, 0, 0)),
                  pl.BlockSpec((2, rv, s), lambda p, i: (p, 0, 0))],
        out_specs=pl.BlockSpec((tq, LANES), lambda p, i: (i, p)),
        out_shape=jax.ShapeDtypeStruct((s, BRANCH_W), jnp.float32),
        scratch_shapes=_flash_scratch(2, tk, tq, rv),
        compiler_params=_cparams(("parallel", "arbitrary")),
        name="attn_a",
    )(qt, k, vt)


def _attn_b_kernel(qt_ref, k_ref, vt_ref, o_ref, s_ref, mc_ref, m_ref, acc_ref, *, tq, tk, nk):
    _init_stats(m_ref, acc_ref)

    def score(c, slot):
        _scores(k_ref[pl.ds(pl.multiple_of(c * tk, tk), tk), :], qt_ref[...], s_ref, mc_ref, slot)

    def consume(c, slot):
        _consume(s_ref, mc_ref, slot, vt_ref[:, pl.ds(pl.multiple_of(c * tk, tk), tk)],
                 m_ref, acc_ref)

    _flash_loop(nk, score, consume)
    ot = _normalised(acc_ref, HEAD_DIM)
    o_ref[...] = jnp.concatenate([ot[:, h * tq:(h + 1) * tq] for h in range(4)], axis=0).T


def _attn_b(qt, k, vt, tq=B_TQ, tk=512):
    s = k.shape[0]
    tk = min(tk, s // 2)
    rv = vt.shape[1]
    return pl.pallas_call(
        functools.partial(_attn_b_kernel, tq=tq, tk=tk, nk=s // tk),
        grid=(2, s // tq),
        in_specs=[pl.BlockSpec((None, None, LANES, 4 * tq), lambda g, i: (g, i, 0, 0)),
                  pl.BlockSpec((s, LANES), lambda g, i: (0, 0)),
                  pl.BlockSpec((None, rv, s), lambda g, i: (g, 0, 0))],
        out_specs=pl.BlockSpec((tq, 2 * LANES), lambda g, i: (i, g)),
        out_shape=jax.ShapeDtypeStruct((s, BRANCH_W), jnp.float32),
        scratch_shapes=_flash_scratch(1, tk, 4 * tq, rv),
        compiler_params=_cparams(("parallel", "arbitrary")),
        name="attn_b",
    )(qt, k, vt)


def _attn_d_kernel(qt_ref, k_ref, vt_ref, band_ref, lam_ref, sub_ref, o_ref,
                   s_ref, mc_ref, m_ref, acc_ref, *, tq, tk, nk, lambda_init):
    _init_stats(m_ref, acc_ref)
    q0 = pl.program_id(1) * tq
    hi = tq + 2 * D_BAND - tk

    def score(c, slot):
        k0 = pl.multiple_of(c * tk, tk)
        b0 = pl.multiple_of(jnp.clip(k0 - q0 + D_BAND, 0, hi), LANES)
        bias = band_ref[pl.ds(b0, tk), :]
        _scores(k_ref[pl.ds(k0, tk), :], qt_ref[...], s_ref, mc_ref, slot,
                jnp.concatenate([bias, bias], axis=1))

    def consume(c, slot):
        _consume(s_ref, mc_ref, slot, vt_ref[:, pl.ds(pl.multiple_of(c * tk, tk), tk)],
                 m_ref, acc_ref)

    _flash_loop(nk, score, consume)
    lv = lam_ref[...]
    lam = (jnp.exp(jnp.sum(lv[0:1] * lv[1:2], axis=-1, keepdims=True))
           - jnp.exp(jnp.sum(lv[2:3] * lv[3:4], axis=-1, keepdims=True)) + lambda_init)
    ot = _normalised(acc_ref, D_V)
    o = (ot[:, :tq] - lam * ot[:, tq:]).T
    o = o * lax.rsqrt(jnp.mean(o * o, axis=-1, keepdims=True) + EPS) * sub_ref[...]
    o_ref[...] = o * (1.0 - lambda_init)


def _attn_d(qt, k, vt, band, lam_vecs, subw, lambda_init, tq=D_TQ, tk=D_TK):
    s = k.shape[0]
    rv = vt.shape[1]
    return pl.pallas_call(
        functools.partial(_attn_d_kernel, tq=tq, tk=tk, nk=s // tk, lambda_init=lambda_init),
        grid=(D_HEADS, s // tq),
        in_specs=[pl.BlockSpec((None, None, LANES, 2 * tq), lambda h, i: (h, i, 0, 0)),
                  pl.BlockSpec((s, LANES), lambda h, i: (0, h)),
                  pl.BlockSpec((None, rv, s), lambda h, i: (h, 0, 0)),
                  pl.BlockSpec((None, tq + 2 * D_BAND, tq), lambda h, i: (h, 0, 0)),
                  pl.BlockSpec((4, HEAD_DIM), lambda h, i: (0, 0)),
                  pl.BlockSpec((1, LANES), lambda h, i: (0, 0))],
        out_specs=pl.BlockSpec((tq, LANES), lambda h, i: (i, h)),
        out_shape=jax.ShapeDtypeStruct((s, BRANCH_W), jnp.float32),
        scratch_shapes=_flash_scratch(1, tk, 2 * tq, rv),
        compiler_params=_cparams(("parallel", "arbitrary")),
        name="attn_d",
    )(qt, k, vt, band, lam_vecs, subw)


def _attn_c_kernel(q_ref, kt_ref, v_ref, band_ref, o_ref, lse_ref, *, nt, seg, win, unroll):
    j = pl.program_id(2)
    lo = _lane_lo((C_QT, LANES))

    def scores(t):
        r0 = pl.multiple_of(t * C_QT, C_QT)
        u0 = j * (nt * C_QT) + r0
        ws = pl.multiple_of(jnp.clip(u0 - C_QT, 0, seg - win), C_QT)
        x = pl.multiple_of(ws - u0 + 2 * C_QT, C_QT)
        q = q_ref[pl.ds(r0, C_QT), :]
        zero = jnp.zeros_like(q)
        q2 = jnp.concatenate([jnp.where(lo, q, zero), jnp.where(lo, zero, q)], axis=0)
        s = jnp.dot(q2, kt_ref[:, pl.ds(ws, win)], preferred_element_type=jnp.float32)
        s = s + jnp.concatenate([band_ref[0, :, pl.ds(x, win)], band_ref[1, :, pl.ds(x, win)]], axis=0)
        return r0, ws, s

    def finish(r0, ws, s):
        m = jnp.max(s, axis=1, keepdims=True)
        p = jnp.exp2(s - m)
        l = jnp.sum(p, axis=1, keepdims=True)
        pv = jnp.dot(p.astype(jnp.bfloat16), v_ref[pl.ds(ws, win), :],
                     preferred_element_type=jnp.float32)
        o = pv / l
        lse = m + jnp.log2(l)
        o_ref[pl.ds(r0, C_QT), :] = jnp.where(lo, o[:C_QT], o[C_QT:])
        lse_ref[pl.ds(r0, C_QT), :] = jnp.where(lo, lse[:C_QT], lse[C_QT:])

    def body(tt, carry):
        tiles = [scores(tt * unroll + i) for i in range(unroll)]
        for tile in tiles:
            finish(*tile)
        return carry

    lax.fori_loop(0, nt // unroll, body, 0)


def _attn_c(q, kt, v, band, nt_max=8):
    dil, seg, w = q.shape
    nt = min(nt_max, seg // C_QT)
    win = min(3 * C_QT, seg)
    tqb = nt * C_QT
    out = jax.ShapeDtypeStruct((dil, seg, w), jnp.float32)
    ospec = pl.BlockSpec((None, tqb, LANES), lambda r, b, j: (r, j, b))
    return pl.pallas_call(
        functools.partial(_attn_c_kernel, nt=nt, seg=seg, win=win, unroll=min(4, nt)),
        grid=(dil, C_HEADS // 2, seg // tqb),
        in_specs=[pl.BlockSpec((None, tqb, LANES), lambda r, b, j: (r, j, b)),
                  pl.BlockSpec((None, LANES, seg), lambda r, b, j: (r, b, 0)),
                  pl.BlockSpec((None, seg, LANES), lambda r, b, j: (r, 0, b)),
                  pl.BlockSpec((2, C_QT, C_BAND_W), lambda r, b, j: (b, 0, 0))],
        out_specs=[ospec, ospec],
        out_shape=[out, out],
        compiler_params=_cparams(("parallel", "parallel", "arbitrary")),
        name=f"attn_c_d{dil}",
    )(q, kt, v, band)


def _combine_c_kernel(*refs):
    o_refs, l_refs, out_ref = refs[0:2 * C_GROUPS:2], refs[1:2 * C_GROUPS:2], refs[-1]
    n = out_ref.shape[0] // C_DIL_MAX
    for r in range(C_DIL_MAX):
        outs, lses = [], []
        for (_, dil), o_ref, l_ref in zip(C_CONFIGS, o_refs, l_refs):
            rows = pl.ds(r // dil, n, stride=C_DIL_MAX // dil)
            cls = pl.ds(r % dil, 1)
            outs.append(o_ref[cls, rows, :][0])
            lses.append(l_ref[cls, rows, :][0])
        m = functools.reduce(jnp.maximum, lses)
        es = [jnp.exp2(l - m) for l in lses]
        num = sum(e * o for e, o in zip(es, outs))
        out_ref[pl.ds(r, n, stride=C_DIL_MAX), :] = num / sum(es)


def _combine_c(os, lses, s, tn=2048):
    nb = C_HEADS // 2
    args, in_specs = [], []
    for (_, dil), o, l in zip(C_CONFIGS, os, lses):
        spec = pl.BlockSpec((dil, tn // dil, LANES), lambda i, b: (0, i, b))
        args += [o, l]
        in_specs += [spec, spec]
    return pl.pallas_call(
        _combine_c_kernel,
        grid=(s // tn, nb),
        in_specs=in_specs,
        out_specs=pl.BlockSpec((tn, LANES), lambda i, b: (i, b)),
        out_shape=jax.ShapeDtypeStruct((s, nb * LANES), jnp.float32),
        compiler_params=_cparams(("parallel", "parallel")),
        name="combine_c",
    )(*args)


def _merge_kernel(x_ref, h_ref, sg_ref, ya_ref, yb_ref, yc_ref, yd_ref,
                  wmt_ref, wb_ref, wo_ref, o_ref, yg_ref):
    c = pl.program_id(1)

    @pl.when(c == 0)
    def _():
        sg = sg_ref[...]
        for n, y_ref in enumerate((ya_ref, yb_ref, yc_ref, yd_ref)):
            g = sg[:, n * BRANCH_W:(n + 1) * BRANCH_W]
            yg_ref[n] = (y_ref[...] * (g * jax.nn.sigmoid(g))).astype(yg_ref.dtype)
        o_ref[...] = x_ref[...]

    h = h_ref[...]
    mixed = 0.0
    for n in range(N_BRANCH):
        gate = jax.nn.sigmoid(_dot_nt(h, wmt_ref[n]))
        z = jnp.dot(yg_ref[n], wb_ref[n], preferred_element_type=jnp.float32)
        mixed = mixed + gate * z
    o_ref[...] += jnp.dot(mixed.astype(jnp.bfloat16), wo_ref[...],
                          preferred_element_type=jnp.float32)


def _merge(x, h, proj, ys, wmerge_t, wbranch_all, wout_all, layer, tm=512, tn=256):
    s, d = x.shape
    nc = d // tn
    ni = s // tm
    row = lambda w: pl.BlockSpec((tm, w), lambda i, c: (i, 0))

    def early(w, c_from):
        return pl.BlockSpec((tm, w), lambda i, c: (jnp.minimum(i + (c >= c_from), ni - 1), 0))

    return pl.pallas_call(
        _merge_kernel,
        grid=(ni, nc),
        in_specs=[early(d, 3), row(d), early(N_BRANCH * BRANCH_W, 1),
                  early(BRANCH_W, 5), early(BRANCH_W, 5), early(BRANCH_W, 6), early(BRANCH_W, 6),
                  pl.BlockSpec((None, N_BRANCH, tn, d), lambda i, c: (layer, 0, c, 0)),
                  pl.BlockSpec((None, N_BRANCH, BRANCH_W, tn), lambda i, c: (layer, 0, 0, c)),
                  pl.BlockSpec((None, tn, d), lambda i, c: (layer, c, 0))],
        out_specs=row(d),
        out_shape=jax.ShapeDtypeStruct((s, d), jnp.float32),
        scratch_shapes=[pltpu.VMEM((N_BRANCH, tm, BRANCH_W), jnp.bfloat16)],
        compiler_params=_cparams(("parallel", "arbitrary")),
        name="merge",
    )(x, h, proj, *ys, wmerge_t, wbranch_all, wout_all)


def _rope_cos_sin(pos, dim):
    inv = ROPE_THETA ** (-jnp.arange(0, dim, 2, dtype=jnp.float32) / dim)
    ang = pos.astype(jnp.float32)[:, None] * inv[None, :]
    return jnp.cos(ang), jnp.sin(ang)


def _tables_a(s):
    cos, sin = _rope_cos_sin(jnp.arange(s, dtype=jnp.int32), A_ROPE)
    z16, z32 = jnp.zeros((s, 16), jnp.float32), jnp.zeros((s, 32), jnp.float32)
    one = jnp.ones((s, A_NOPE), jnp.float32)
    zero = jnp.zeros((s, A_NOPE), jnp.float32)
    c = jnp.concatenate([one, cos, cos, z32], axis=1)
    sa = jnp.concatenate([zero, -sin, z16, z32], axis=1)
    sb = jnp.concatenate([zero, z16, sin, z32], axis=1)
    return c, sa, sb


def _tables_b(s):
    rows = s // GRID_W
    row_pos = jnp.repeat(jnp.arange(rows, dtype=jnp.int32), GRID_W)
    col_pos = jnp.tile(jnp.arange(GRID_W, dtype=jnp.int32), rows)
    cr, sr = _rope_cos_sin(row_pos, HEAD_DIM // 2)
    cc, sc = _rope_cos_sin(col_pos, HEAD_DIM // 2)
    z = jnp.zeros_like(sr)
    c = jnp.concatenate([cr, cr, cc, cc] * 2, axis=1)
    sa = jnp.concatenate([-sr, z, -sc, z] * 2, axis=1)
    sb = jnp.concatenate([z, sr, z, sc] * 2, axis=1)
    return c, sa, sb


def _pad_lanes(v, width=LANES):
    return jnp.pad(v, ((0, width - v.shape[0]),))[None, :]


def kernel(x, norm_w, w_in, mla_q_norm, mla_kv_norm, mla_w_uq, mla_w_ukv, mla_qk_norm,
           gqa_qk_norm, dil_qk_norm, diff_qk_norm, diff_lambda, diff_subnorm, rel_bias,
           w_branch, w_out):
    b, s, d = x.shape
    assert b == 1 and d == D_MODEL
    depth = norm_w.shape[0]
    bf = jnp.bfloat16

    wt = jnp.swapaxes(w_in, 1, 2)
    seg = lambda a, b: wt[:, a:b].astype(bf)
    w_main_t = jnp.concatenate([seg(SRC_SILU, SRC_MERGE), seg(SRC_C, SRC_D),
                                seg(SRC_D, SRC_SILU), seg(SRC_B, SRC_C)], axis=1)
    w_a_t = jnp.pad(seg(SRC_A, SRC_B), ((0, 0), (0, A_BLOCK - SRC_B), (0, 0)))
    w_merge_t = seg(SRC_MERGE, SRC_END).reshape(depth, N_BRANCH, d, d)
    w_branch_b = w_branch.astype(bf)
    w_out_b = w_out.astype(bf)
    wuq = mla_w_uq.reshape(depth, Q_LORA, A_HEADS, A_QK)
    wuq = jnp.pad(wuq, ((0, 0), (0, 0), (0, 0), (0, LANES - A_QK)))
    wuq = wuq.reshape(depth, Q_LORA, A_HEADS * LANES).astype(bf)
    wukv = mla_w_ukv.astype(bf)

    tabs_a = _tables_a(s)
    tabs_b = _tables_b(s)
    bands_c = [_band(rel_bias, C_HEADS, C_QT, C_BAND_W, C_QT, pad=2 * C_QT, dil=dil,
                     half=C_HALF, col0=g * C_HEADS, key_axis=1, name=f"band_c{g}")
               for g, (_, dil) in enumerate(C_CONFIGS)]
    band_r = D_TQ + 2 * D_BAND
    band_d = _band(rel_bias, D_HEADS, band_r, D_TQ, band_r // 2, pad=D_BAND, dil=1, half=None,
                   col0=C_GROUPS * C_HEADS, key_axis=0, name="band_d")
    c_ut = [max(C_QT, min(s, 2048) // dil) for _, dil in C_CONFIGS]

    xs = x[0]
    for l in range(depth):
        h = _rmsnorm(xs, norm_w[l][None, :])
        proj = _in_proj(h, w_main_t, l)
        proj_a = _in_proj(h, w_a_t, l)

        qa, kta, va = _prep_a(proj_a, mla_q_norm[l][None, :], mla_kv_norm[l][None, :],
                              wuq[l], wukv[l], _pad_lanes(mla_qk_norm[l, 0]),
                              _pad_lanes(mla_qk_norm[l, 1]), tabs_a)
        y_a = _attn_a(qa, kta, va)

        g2 = lambda v: jnp.tile(v, 2)[None, :]
        qb, ktb, vb = _prep_b(proj, g2(gqa_qk_norm[l, 0]), g2(gqa_qk_norm[l, 1]), tabs_b)
        y_b = _attn_b(qb, ktb, vb)

        os, lses = [], []
        for g in range(C_GROUPS):
            qg, ktg, vg = _prep_c(proj, g2(dil_qk_norm[l, 0, g]), g2(dil_qk_norm[l, 1, g]),
                                  g, c_ut[g])
            og, lg = _attn_c(qg, ktg, vg, bands_c[g])
            os.append(og)
            lses.append(lg)
        y_c = _combine_c(os, lses, s)

        qd, ktd, vd = _prep_d(proj, g2(diff_qk_norm[l, 0]), g2(diff_qk_norm[l, 1]))
        lambda_init = 0.8 - 0.6 * math.exp(-0.3 * l)
        y_d = _attn_d(qd, ktd, vd, band_d, diff_lambda[l], diff_subnorm[l][None, :], lambda_init)

        xs = _merge(xs, h, proj, (y_a, y_b, y_c, y_d), w_merge_t, w_branch_b, w_out_b, l)
    return xs[None]
```

```python
import functools
import math

import numpy as np
import jax
import jax.numpy as jnp
from jax import lax
from jax.experimental import pallas as pl
from jax.experimental.pallas import tpu as pltpu

D_MODEL = 2048
GRID_W = 64
HEAD_DIM = 64
BRANCH_W = 512
N_BRANCH = 4
ROPE_THETA = 10000.0
EPS = 1e-6
NEG = -1e30

A_HEADS = 8
A_NOPE = 64
A_ROPE = 32
A_V = 64
A_QK = A_NOPE + A_ROPE
Q_LORA = 384
KV_LORA = 128

B_HEADS = 8
B_KV_HEADS = 2

C_HEADS = 8
C_CONFIGS = ((128, 1), (512, 4), (2048, 16))
C_GROUPS = len(C_CONFIGS)

D_HEADS = 4
D_V = 2 * HEAD_DIM

NUM_BUCKETS = 32
T5_MAX_DISTANCE = 1024

LANES = 128

SRC_A, SRC_B, SRC_C, SRC_D, SRC_SILU, SRC_MERGE, SRC_END = 0, 544, 1312, 5920, 7456, 9504, 17696

OFF_SILU = 0
OFF_CQ, OFF_CK, OFF_CV = 2048, 3584, 5120
OFF_DQ, OFF_DK, OFF_DV = 6656, 7168, 7680
OFF_BQ, OFF_BKV = 8192, 8704
A_BLOCK = 640

VMEM_LIMIT = 56 * 1024 * 1024

LOG2E = math.log2(math.e)

C_QT = 128
C_HALF = 64
C_DIL_MAX = max(d for _, d in C_CONFIGS)
C_BAND_W = 5 * C_QT

D_TQ = 512
D_TK = 512
D_BAND = 1152
D_SLOTS = 4


def _bucket_thresholds():
    nb = NUM_BUCKETS // 2
    max_exact = nb // 2
    n = np.arange(max_exact, 4 * T5_MAX_DISTANCE, dtype=np.float32)
    large = max_exact + (np.log(n / np.float32(max_exact))
                         / np.float32(math.log(T5_MAX_DISTANCE / max_exact))
                         * np.float32(nb - max_exact)).astype(np.int32)
    large = np.minimum(large, nb - 1)
    return tuple(int(n[np.argmax(large >= max_exact + k)]) for k in range(1, nb - max_exact))


BUCKET_STEPS = _bucket_thresholds()
assert D_BAND - D_TK + 1 >= BUCKET_STEPS[-1]


def _cparams(sem):
    return pltpu.CompilerParams(dimension_semantics=sem, vmem_limit_bytes=VMEM_LIMIT)


def _rmsnorm_kernel(x_ref, w_ref, o_ref):
    x = x_ref[...]
    ms = jnp.mean(x * x, axis=-1, keepdims=True)
    o_ref[...] = (x * lax.rsqrt(ms + EPS) * w_ref[...]).astype(o_ref.dtype)


def _rmsnorm(x, w, tm=512):
    s, d = x.shape
    return pl.pallas_call(
        _rmsnorm_kernel,
        grid=(s // tm,),
        in_specs=[pl.BlockSpec((tm, d), lambda i: (i, 0)),
                  pl.BlockSpec((1, d), lambda i: (0, 0))],
        out_specs=pl.BlockSpec((tm, d), lambda i: (i, 0)),
        out_shape=jax.ShapeDtypeStruct((s, d), jnp.bfloat16),
        compiler_params=_cparams(("parallel",)),
        name="rmsnorm",
    )(x, w)


def _dot_nt(a, bt):
    return lax.dot_general(a, bt, (((1,), (1,)), ((), ())), preferred_element_type=jnp.float32)


def _matmul_nt_kernel(a_ref, bt_ref, o_ref):
    o_ref[...] = _dot_nt(a_ref[...], bt_ref[...])


def _in_proj(h, wt_all, layer, tm=1024, tn=1792):
    s, d = h.shape
    n = wt_all.shape[1]
    tn = min(tn, n)
    return pl.pallas_call(
        _matmul_nt_kernel,
        grid=(s // tm, n // tn),
        in_specs=[pl.BlockSpec((tm, d), lambda i, j: (i, 0)),
                  pl.BlockSpec((None, tn, d), lambda i, j: (layer, j, 0))],
        out_specs=pl.BlockSpec((tm, tn), lambda i, j: (i, j)),
        out_shape=jax.ShapeDtypeStruct((s, n), jnp.float32),
        compiler_params=_cparams(("parallel", "arbitrary")),
        name="in_proj",
    )(h, wt_all)


def _band_kernel(tab_ref, o_ref, *, pad, dil, half, col0, key_axis):
    h = pl.program_id(0)
    rows, width = o_ref.shape
    col = lax.broadcasted_iota(jnp.int32, (rows, width), 1)
    row = lax.broadcasted_iota(jnp.int32, (rows, width), 0) + pl.program_id(1) * rows
    rel_u = (col - row if key_axis == 1 else row - col) - pad
    rel = rel_u * dil
    n = jnp.abs(rel)
    nb = NUM_BUCKETS // 2
    max_exact = nb // 2
    large = jnp.full((rows, width), max_exact, jnp.int32)
    for t in BUCKET_STEPS:
        large = large + jnp.where(n >= t, 1, 0)
    bucket = jnp.where(rel > 0, nb, 0) + jnp.where(n < max_exact, n, large)
    val = jnp.zeros((rows, width), jnp.float32)
    for b in range(NUM_BUCKETS):
        val = jnp.where(bucket == b, tab_ref[b, col0 + h], val)
    val = val * LOG2E
    if half is not None:
        val = jnp.where(jnp.abs(rel_u) <= half, val, NEG)
    o_ref[...] = val


def _band(rel_bias, heads, rows, width, rblk, *, pad, dil, half, col0, key_axis, name):
    return pl.pallas_call(
        functools.partial(_band_kernel, pad=pad, dil=dil, half=half, col0=col0,
                          key_axis=key_axis),
        grid=(heads, rows // rblk),
        in_specs=[pl.BlockSpec(memory_space=pltpu.SMEM)],
        out_specs=pl.BlockSpec((None, rblk, width), lambda h, j: (h, j, 0)),
        out_shape=jax.ShapeDtypeStruct((heads, rows, width), jnp.float32),
        compiler_params=_cparams(("parallel", "arbitrary")),
        name=name,
    )(rel_bias)


def _lane_lo(shape):
    return (lax.broadcasted_iota(jnp.int32, shape, len(shape) - 1) % LANES) < HEAD_DIM


def _rope(x, c, sa, sb):
    return x * c + pltpu.roll(x, LANES - 16, 1) * sa + pltpu.roll(x, 16, 1) * sb


def _halfnorm(x, gain, lo):
    sq = x * x
    s_lo = jnp.sum(jnp.where(lo, sq, 0.0), axis=-1, keepdims=True)
    s_hi = jnp.sum(jnp.where(lo, 0.0, sq), axis=-1, keepdims=True)
    r = jnp.where(lo, lax.rsqrt(s_lo * (1.0 / HEAD_DIM) + EPS),
                  lax.rsqrt(s_hi * (1.0 / HEAD_DIM) + EPS))
    return x * r * gain


ONES_ROWS = 16


def _prep_a_kernel(p_ref, qn_ref, kvn_ref, wuq_ref, wukv_ref, gq_ref, gk_ref,
                   c_ref, sa_ref, sb_ref, qt_out, k_out, vt_out):
    p = p_ref[...]
    cq = p[:, :Q_LORA]
    cq = cq * lax.rsqrt(jnp.mean(cq * cq, axis=-1, keepdims=True) + EPS) * qn_ref[...]
    q = jnp.dot(cq.astype(jnp.bfloat16), wuq_ref[...], preferred_element_type=jnp.float32)
    ckv = p[:, Q_LORA:Q_LORA + KV_LORA]
    ckv = ckv * lax.rsqrt(jnp.mean(ckv * ckv, axis=-1, keepdims=True) + EPS) * kvn_ref[...]
    kvu = jnp.dot(ckv.astype(jnp.bfloat16), wukv_ref[...], preferred_element_type=jnp.float32)
    kr = pltpu.roll(p[:, Q_LORA + KV_LORA:], HEAD_DIM, 1)
    lo = _lane_lo(kr.shape)
    c, sa, sb = c_ref[...], sa_ref[...], sb_ref[...]
    scale = A_QK ** -0.5 * LOG2E
    ones = jnp.ones((ONES_ROWS, p.shape[0]), vt_out.dtype)
    for h in range(A_HEADS):
        qh = q[:, h * LANES:(h + 1) * LANES]
        ss = jnp.sum(qh * qh, axis=-1, keepdims=True) * (1.0 / A_QK)
        qh = qh * lax.rsqrt(ss + EPS) * gq_ref[...]
        qt_out[h] = (_rope(qh, c, sa, sb) * scale).T.astype(qt_out.dtype)
        kvh = kvu[:, h * LANES:(h + 1) * LANES]
        kh = jnp.where(lo, kvh, kr)
        ss = jnp.sum(kh * kh, axis=-1, keepdims=True) * (1.0 / A_QK)
        kh = kh * lax.rsqrt(ss + EPS) * gk_ref[...]
        k_out[h] = _rope(kh, c, sa, sb).astype(k_out.dtype)
        vt_out[h, :A_V, :] = kvh.T[A_NOPE:].astype(vt_out.dtype)
        vt_out[h, A_V:, :] = ones


def _prep_a(proj, qn, kvn, wuq, wukv, gq, gk, tabs, tm=512):
    s = proj.shape[0]
    row = lambda w: pl.BlockSpec((1, w), lambda i: (0, 0))
    tab = pl.BlockSpec((tm, LANES), lambda i: (i, 0))
    return pl.pallas_call(
        _prep_a_kernel,
        grid=(s // tm,),
        in_specs=[pl.BlockSpec((tm, A_BLOCK), lambda i: (i, 0)),
                  row(Q_LORA), row(KV_LORA),
                  pl.BlockSpec(wuq.shape, lambda i: (0, 0)),
                  pl.BlockSpec(wukv.shape, lambda i: (0, 0)),
                  row(LANES), row(LANES), tab, tab, tab],
        out_specs=[pl.BlockSpec((A_HEADS, LANES, tm), lambda i: (0, 0, i)),
                   pl.BlockSpec((A_HEADS, tm, LANES), lambda i: (0, i, 0)),
                   pl.BlockSpec((A_HEADS, A_V + ONES_ROWS, tm), lambda i: (0, 0, i))],
        out_shape=[jax.ShapeDtypeStruct((A_HEADS, LANES, s), jnp.bfloat16),
                   jax.ShapeDtypeStruct((A_HEADS, s, LANES), jnp.bfloat16),
                   jax.ShapeDtypeStruct((A_HEADS, A_V + ONES_ROWS, s), jnp.bfloat16)],
        compiler_params=_cparams(("parallel",)),
        name="prep_a",
    )(proj, qn, kvn, wuq, wukv, gq, gk, *tabs)


B_TQ = 256


def _prep_b_kernel(q_ref, kv_ref, gq_ref, gk_ref, c_ref, sa_ref, sb_ref,
                   qt_out, k_out, vt_out):
    c, sa, sb = c_ref[...], sa_ref[...], sb_ref[...]
    lo = _lane_lo(c.shape)
    tm = c.shape[0]
    scale = HEAD_DIM ** -0.5 * LOG2E
    q = q_ref[...]
    for b in range(B_HEADS // 2):
        x = _rope(_halfnorm(q[:, b * LANES:(b + 1) * LANES], gq_ref[...], lo), c, sa, sb) * scale
        xr = pltpu.roll(x, HEAD_DIM, 1)
        g = b // 2
        even, odd = (x, xr) if g == 0 else (xr, x)
        keep = lo if g == 0 else jnp.logical_not(lo)
        for j, xh in enumerate((even, odd)):
            hh = 2 * (b % 2) + j
            qt_out[g, :, hh * tm:(hh + 1) * tm] = jnp.where(keep, xh, 0.0).T.astype(qt_out.dtype)
    kv = kv_ref[...]
    k_out[...] = _rope(_halfnorm(kv[:, :LANES], gk_ref[...], lo), c, sa, sb).astype(k_out.dtype)
    vt = kv[:, LANES:].T.astype(vt_out.dtype)
    ones = jnp.ones((ONES_ROWS, tm), vt_out.dtype)
    for g in range(B_KV_HEADS):
        vt_out[g, :HEAD_DIM, :] = vt[g * HEAD_DIM:(g + 1) * HEAD_DIM]
        vt_out[g, HEAD_DIM:, :] = ones


def _prep_b(proj, gq, gk, tabs, tm=B_TQ):
    s = proj.shape[0]
    row = pl.BlockSpec((1, LANES), lambda i: (0, 0))
    tab = pl.BlockSpec((tm, LANES), lambda i: (i, 0))
    rv = HEAD_DIM + ONES_ROWS
    return pl.pallas_call(
        _prep_b_kernel,
        grid=(s // tm,),
        in_specs=[pl.BlockSpec((tm, 512), lambda i: (i, OFF_BQ // 512)),
                  pl.BlockSpec((tm, 256), lambda i: (i, OFF_BKV // 256)),
                  row, row, tab, tab, tab],
        out_specs=[pl.BlockSpec((2, None, LANES, 4 * tm), lambda i: (0, i, 0, 0)),
                   pl.BlockSpec((tm, LANES), lambda i: (i, 0)),
                   pl.BlockSpec((2, rv, tm), lambda i: (0, 0, i))],
        out_shape=[jax.ShapeDtypeStruct((2, s // tm, LANES, 4 * tm), jnp.bfloat16),
                   jax.ShapeDtypeStruct((s, LANES), jnp.bfloat16),
                   jax.ShapeDtypeStruct((2, rv, s), jnp.bfloat16)],
        compiler_params=_cparams(("parallel",)),
        name="prep_b",
    )(proj, proj, gq, gk, *tabs)


def _prep_d_kernel(q_ref, k_ref, v_ref, gq_ref, gk_ref, qt_out, k_out, vt_out):
    q, k, v = q_ref[...], k_ref[...], v_ref[...]
    tm = q.shape[0]
    lo = _lane_lo((tm, LANES))
    scale = HEAD_DIM ** -0.5 * LOG2E
    ones = jnp.ones((ONES_ROWS, tm), vt_out.dtype)
    for h in range(D_HEADS):
        cols = slice(h * LANES, (h + 1) * LANES)
        x = _halfnorm(q[:, cols], gq_ref[...], lo) * scale
        qt_out[h, :, :tm] = jnp.where(lo, x, 0.0).T.astype(qt_out.dtype)
        qt_out[h, :, tm:] = jnp.where(lo, 0.0, x).T.astype(qt_out.dtype)
        k_out[:, cols] = _halfnorm(k[:, cols], gk_ref[...], lo).astype(k_out.dtype)
        vt_out[h, :D_V, :] = v[:, cols].T.astype(vt_out.dtype)
        vt_out[h, D_V:, :] = ones


def _prep_d(proj, gq, gk, tm=D_TQ):
    s = proj.shape[0]
    row = pl.BlockSpec((1, LANES), lambda i: (0, 0))
    blk = lambda off: pl.BlockSpec((tm, 512), lambda i: (i, off // 512))
    rv = D_V + ONES_ROWS
    return pl.pallas_call(
        _prep_d_kernel,
        grid=(s // tm,),
        in_specs=[blk(OFF_DQ), blk(OFF_DK), blk(OFF_DV), row, row],
        out_specs=[pl.BlockSpec((D_HEADS, None, LANES, 2 * tm), lambda i: (0, i, 0, 0)),
                   pl.BlockSpec((tm, 512), lambda i: (i, 0)),
                   pl.BlockSpec((D_HEADS, rv, tm), lambda i: (0, 0, i))],
        out_shape=[jax.ShapeDtypeStruct((D_HEADS, s // tm, LANES, 2 * tm), jnp.bfloat16),
                   jax.ShapeDtypeStruct((s, 512), jnp.bfloat16),
                   jax.ShapeDtypeStruct((D_HEADS, rv, s), jnp.bfloat16)],
        compiler_params=_cparams(("parallel",)),
        name="prep_d",
    )(proj, proj, proj, gq, gk)


def _prep_c_kernel(q_ref, k_ref, v_ref, gq_ref, gk_ref, q_out, kt_out, v_out, *, dil, ut):
    lo = _lane_lo((ut, LANES))
    scale = HEAD_DIM ** -0.5 * LOG2E
    for r in range(dil):
        rows = pl.ds(r, ut, stride=dil) if dil > 1 else pl.ds(0, ut)
        q_out[r] = (_halfnorm(q_ref[rows, :], gq_ref[...], lo) * scale).astype(q_out.dtype)
        kt_out[r] = _halfnorm(k_ref[rows, :], gk_ref[...], lo).T.astype(kt_out.dtype)
        v_out[r] = v_ref[rows, :].astype(v_out.dtype)


def _prep_c(proj, gq, gk, g, ut):
    s = proj.shape[0]
    dil = C_CONFIGS[g][1]
    seg = s // dil
    nb = C_HEADS // 2
    tn = ut * dil
    blk = lambda off: pl.BlockSpec((tn, LANES), lambda i, b: (i, off // LANES + g * nb + b))
    row = pl.BlockSpec((1, LANES), lambda i, b: (0, 0))
    return pl.pallas_call(
        functools.partial(_prep_c_kernel, dil=dil, ut=ut),
        grid=(s // tn, nb),
        in_specs=[blk(OFF_CQ), blk(OFF_CK), blk(OFF_CV), row, row],
        out_specs=[pl.BlockSpec((dil, ut, LANES), lambda i, b: (0, i, b)),
                   pl.BlockSpec((dil, LANES, ut), lambda i, b: (0, b, i)),
                   pl.BlockSpec((dil, ut, LANES), lambda i, b: (0, i, b))],
        out_shape=[jax.ShapeDtypeStruct((dil, seg, nb * LANES), jnp.bfloat16),
                   jax.ShapeDtypeStruct((dil, nb * LANES, seg), jnp.bfloat16),
                   jax.ShapeDtypeStruct((dil, seg, nb * LANES), jnp.bfloat16)],
        compiler_params=_cparams(("parallel", "parallel")),
        name=f"prep_c{g}",
    )(proj, proj, proj, gq, gk)


def _scores(k, qt, s_ref, mc_ref, slot, bias=None):
    st = jnp.dot(k, qt, preferred_element_type=jnp.float32)
    if bias is not None:
        st = st + bias
    s_ref[slot] = st
    mc_ref[slot] = jnp.max(st, axis=0, keepdims=True)


def _consume(s_ref, mc_ref, slot, vt, m_ref, acc_ref):
    m_prev = m_ref[...]
    m_next = jnp.maximum(m_prev, mc_ref[slot])
    p = jnp.exp2(s_ref[slot] - m_next)
    alpha = jnp.exp2(m_prev - m_next)
    m_ref[...] = m_next
    pv = jnp.dot(vt, p.astype(jnp.bfloat16), preferred_element_type=jnp.float32)
    acc_ref[...] = alpha * acc_ref[...] + pv


FLASH_UNROLL = 4


def _flash_loop(nk, score, consume, slots=2):
    u = min(FLASH_UNROLL, nk)
    ahead = slots - 1
    assert nk % u == 0 and u % slots == 0
    for c in range(ahead):
        score(c, c % slots)

    def body(j, carry):
        for i in range(u):
            score(j * u + i + ahead, (i + ahead) % slots)
            consume(j * u + i, i % slots)
        return carry

    lax.fori_loop(0, nk // u - 1, body, 0)
    for c in range(nk - u, nk):
        if c + ahead < nk:
            score(c + ahead, (c + ahead) % slots)
        consume(c, c % slots)


def _init_stats(m_ref, acc_ref):
    m_ref[...] = jnp.full_like(m_ref, -jnp.inf)
    acc_ref[...] = jnp.zeros_like(acc_ref)


def _normalised(acc_ref, rv):
    acc = acc_ref[...]
    return acc[:rv] / acc[rv:rv + 1]


def _flash_scratch(streams, tk, n, rv, slots=2):
    return ([pltpu.VMEM((slots, tk, n), jnp.float32)] * streams
            + [pltpu.VMEM((slots, 1, n), jnp.float32)] * streams
            + [pltpu.VMEM((1, n), jnp.float32)] * streams
            + [pltpu.VMEM((rv, n), jnp.float32)] * streams)


def _attn_a_kernel(qt_ref, k_ref, vt_ref, o_ref, s0_ref, s1_ref, mc0_ref, mc1_ref,
                   m0_ref, m1_ref, acc0_ref, acc1_ref, *, tk, nk):
    streams = ((s0_ref, mc0_ref, m0_ref, acc0_ref), (s1_ref, mc1_ref, m1_ref, acc1_ref))
    for _, _, m_ref, acc_ref in streams:
        _init_stats(m_ref, acc_ref)

    def score(c, slot):
        k0 = pl.multiple_of(c * tk, tk)
        for hh, (s_ref, mc_ref, _, _) in enumerate(streams):
            _scores(k_ref[hh, pl.ds(k0, tk), :], qt_ref[hh], s_ref, mc_ref, slot)

    def consume(c, slot):
        k0 = pl.multiple_of(c * tk, tk)
        for hh, (s_ref, mc_ref, m_ref, acc_ref) in enumerate(streams):
            _consume(s_ref, mc_ref, slot, vt_ref[hh, :, pl.ds(k0, tk)], m_ref, acc_ref)

    _flash_loop(nk, score, consume)
    o_ref[...] = jnp.concatenate([_normalised(acc0_ref, A_V), _normalised(acc1_ref, A_V)],
                                 axis=0).T


def _attn_a(qt, k, vt, tq=512, tk=512):
    s = k.shape[1]
    tk = min(tk, s // 2)
    rv = vt.shape[1]
    return pl.pallas_call(
        functools.partial(_attn_a_kernel, tk=tk, nk=s // tk),
        grid=(A_HEADS // 2, s // tq),
        in_specs=[pl.BlockSpec((2, LANES, tq), lambda p, i: (p, 0, i)),
                  pl.BlockSpec((2, s, LANES), lambda p, i: (p, 0, 0)),
                  pl.BlockSpec((2, rv, s), lambda p, i: (p, 0, 0))],
        out_specs=pl.BlockSpec((tq, LANES), lambda p, i: (i, p)),
        out_shape=jax.ShapeDtypeStruct((s, BRANCH_W), jnp.float32),
        scratch_shapes=_flash_scratch(2, tk, tq, rv),
        compiler_params=_cparams(("parallel", "arbitrary")),
        name="attn_a",
    )(qt, k, vt)


def _attn_b_kernel(qt_ref, k_ref, vt_ref, o_ref, s_ref, mc_ref, m_ref, acc_ref, *, tq, tk, nk):
    _init_stats(m_ref, acc_ref)

    def score(c, slot):
        _scores(k_ref[pl.ds(pl.multiple_of(c * tk, tk), tk), :], qt_ref[...], s_ref, mc_ref, slot)

    def consume(c, slot):
        _consume(s_ref, mc_ref, slot, vt_ref[:, pl.ds(pl.multiple_of(c * tk, tk), tk)],
                 m_ref, acc_ref)

    _flash_loop(nk, score, consume)
    ot = _normalised(acc_ref, HEAD_DIM)
    o_ref[...] = jnp.concatenate([ot[:, h * tq:(h + 1) * tq] for h in range(4)], axis=0).T


def _attn_b(qt, k, vt, tq=B_TQ, tk=512):
    s = k.shape[0]
    tk = min(tk, s // 2)
    rv = vt.shape[1]
    return pl.pallas_call(
        functools.partial(_attn_b_kernel, tq=tq, tk=tk, nk=s // tk),
        grid=(2, s // tq),
        in_specs=[pl.BlockSpec((None, None, LANES, 4 * tq), lambda g, i: (g, i, 0, 0)),
                  pl.BlockSpec((s, LANES), lambda g, i: (0, 0)),
                  pl.BlockSpec((None, rv, s), lambda g, i: (g, 0, 0))],
        out_specs=pl.BlockSpec((tq, 2 * LANES), lambda g, i: (i, g)),
        out_shape=jax.ShapeDtypeStruct((s, BRANCH_W), jnp.float32),
        scratch_shapes=_flash_scratch(1, tk, 4 * tq, rv),
        compiler_params=_cparams(("parallel", "arbitrary")),
        name="attn_b",
    )(qt, k, vt)


def _attn_d_kernel(qt_ref, k_ref, vt_ref, band_ref, lam_ref, sub_ref, o_ref,
                   s_ref, mc_ref, m_ref, acc_ref, *, tq, tk, nk, lambda_init):
    _init_stats(m_ref, acc_ref)
    q0 = pl.program_id(1) * tq
    hi = tq + 2 * D_BAND - tk

    def score(c, slot):
        k0 = pl.multiple_of(c * tk, tk)
        b0 = pl.multiple_of(jnp.clip(k0 - q0 + D_BAND, 0, hi), LANES)
        bias = band_ref[pl.ds(b0, tk), :]
        _scores(k_ref[pl.ds(k0, tk), :], qt_ref[...], s_ref, mc_ref, slot,
                jnp.concatenate([bias, bias], axis=1))

    def consume(c, slot):
        _consume(s_ref, mc_ref, slot, vt_ref[:, pl.ds(pl.multiple_of(c * tk, tk), tk)],
                 m_ref, acc_ref)

    _flash_loop(nk, score, consume, D_SLOTS)
    lv = lam_ref[...]
    lam = (jnp.exp(jnp.sum(lv[0:1] * lv[1:2], axis=-1, keepdims=True))
           - jnp.exp(jnp.sum(lv[2:3] * lv[3:4], axis=-1, keepdims=True)) + lambda_init)
    ot = _normalised(acc_ref, D_V)
    o = (ot[:, :tq] - lam * ot[:, tq:]).T
    o = o * lax.rsqrt(jnp.mean(o * o, axis=-1, keepdims=True) + EPS) * sub_ref[...]
    o_ref[...] = o * (1.0 - lambda_init)


def _attn_d(qt, k, vt, band, lam_vecs, subw, lambda_init, tq=D_TQ, tk=D_TK):
    s = k.shape[0]
    rv = vt.shape[1]
    return pl.pallas_call(
        functools.partial(_attn_d_kernel, tq=tq, tk=tk, nk=s // tk, lambda_init=lambda_init),
        grid=(D_HEADS, s // tq),
        in_specs=[pl.BlockSpec((None, None, LANES, 2 * tq), lambda h, i: (h, i, 0, 0)),
                  pl.BlockSpec((s, LANES), lambda h, i: (0, h)),
                  pl.BlockSpec((None, rv, s), lambda h, i: (h, 0, 0)),
                  pl.BlockSpec((None, tq + 2 * D_BAND, tq), lambda h, i: (h, 0, 0)),
                  pl.BlockSpec((4, HEAD_DIM), lambda h, i: (0, 0)),
                  pl.BlockSpec((1, LANES), lambda h, i: (0, 0))],
        out_specs=pl.BlockSpec((tq, LANES), lambda h, i: (i, h)),
        out_shape=jax.ShapeDtypeStruct((s, BRANCH_W), jnp.float32),
        scratch_shapes=_flash_scratch(1, tk, 2 * tq, rv, D_SLOTS),
        compiler_params=_cparams(("parallel", "arbitrary")),
        name="attn_d",
    )(qt, k, vt, band, lam_vecs, subw)


def _attn_c_kernel(q_ref, kt_ref, v_ref, band_ref, o_ref, lse_ref, *, nt, seg, win, unroll):
    j = pl.program_id(2)
    lo = _lane_lo((C_QT, LANES))

    def scores(t):
        r0 = pl.multiple_of(t * C_QT, C_QT)
        u0 = j * (nt * C_QT) + r0
        ws = pl.multiple_of(jnp.clip(u0 - C_QT, 0, seg - win), C_QT)
        x = pl.multiple_of(ws - u0 + 2 * C_QT, C_QT)
        q = q_ref[pl.ds(r0, C_QT), :]
        zero = jnp.zeros_like(q)
        q2 = jnp.concatenate([jnp.where(lo, q, zero), jnp.where(lo, zero, q)], axis=0)
        s = jnp.dot(q2, kt_ref[:, pl.ds(ws, win)], preferred_element_type=jnp.float32)
        s = s + jnp.concatenate([band_ref[0, :, pl.ds(x, win)], band_ref[1, :, pl.ds(x, win)]], axis=0)
        return r0, ws, s

    def finish(r0, ws, s):
        m = jnp.max(s, axis=1, keepdims=True)
        p = jnp.exp2(s - m)
        l = jnp.sum(p, axis=1, keepdims=True)
        pv = jnp.dot(p.astype(jnp.bfloat16), v_ref[pl.ds(ws, win), :],
                     preferred_element_type=jnp.float32)
        o = pv / l
        lse = m + jnp.log2(l)
        o_ref[pl.ds(r0, C_QT), :] = jnp.where(lo, o[:C_QT], o[C_QT:])
        lse_ref[pl.ds(r0, C_QT), :] = jnp.where(lo, lse[:C_QT], lse[C_QT:])

    def body(tt, carry):
        tiles = [scores(tt * unroll + i) for i in range(unroll)]
        for tile in tiles:
            finish(*tile)
        return carry

    lax.fori_loop(0, nt // unroll, body, 0)


def _attn_c(q, kt, v, band, nt_max=8):
    dil, seg, w = q.shape
    nt = min(nt_max, seg // C_QT)
    win = min(3 * C_QT, seg)
    tqb = nt * C_QT
    out = jax.ShapeDtypeStruct((dil, seg, w), jnp.float32)
    ospec = pl.BlockSpec((None, tqb, LANES), lambda r, b, j: (r, j, b))
    return pl.pallas_call(
        functools.partial(_attn_c_kernel, nt=nt, seg=seg, win=win, unroll=min(4, nt)),
        grid=(dil, C_HEADS // 2, seg // tqb),
        in_specs=[pl.BlockSpec((None, tqb, LANES), lambda r, b, j: (r, j, b)),
                  pl.BlockSpec((None, LANES, seg), lambda r, b, j: (r, b, 0)),
                  pl.BlockSpec((None, seg, LANES), lambda r, b, j: (r, 0, b)),
                  pl.BlockSpec((2, C_QT, C_BAND_W), lambda r, b, j: (b, 0, 0))],
        out_specs=[ospec, ospec],
        out_shape=[out, out],
        compiler_params=_cparams(("parallel", "parallel", "arbitrary")),
        name=f"attn_c_d{dil}",
    )(q, kt, v, band)


def _combine_c_kernel(*refs):
    o_refs, l_refs, out_ref = refs[0:2 * C_GROUPS:2], refs[1:2 * C_GROUPS:2], refs[-1]
    n = out_ref.shape[0] // C_DIL_MAX
    for r in range(C_DIL_MAX):
        outs, lses = [], []
        for (_, dil), o_ref, l_ref in zip(C_CONFIGS, o_refs, l_refs):
            rows = pl.ds(r // dil, n, stride=C_DIL_MAX // dil)
            cls = pl.ds(r % dil, 1)
            outs.append(o_ref[cls, rows, :][0])
            lses.append(l_ref[cls, rows, :][0])
        m = functools.reduce(jnp.maximum, lses)
        es = [jnp.exp2(l - m) for l in lses]
        num = sum(e * o for e, o in zip(es, outs))
        out_ref[pl.ds(r, n, stride=C_DIL_MAX), :] = num / sum(es)


def _combine_c(os, lses, s, tn=2048):
    nb = C_HEADS // 2
    args, in_specs = [], []
    for (_, dil), o, l in zip(C_CONFIGS, os, lses):
        spec = pl.BlockSpec((dil, tn // dil, LANES), lambda i, b: (0, i, b))
        args += [o, l]
        in_specs += [spec, spec]
    return pl.pallas_call(
        _combine_c_kernel,
        grid=(s // tn, nb),
        in_specs=in_specs,
        out_specs=pl.BlockSpec((tn, LANES), lambda i, b: (i, b)),
        out_shape=jax.ShapeDtypeStruct((s, nb * LANES), jnp.float32),
        compiler_params=_cparams(("parallel", "parallel")),
        name="combine_c",
    )(*args)


def _merge_kernel(x_ref, h_ref, sg_ref, ya_ref, yb_ref, yc_ref, yd_ref,
                  wmt_ref, wb_ref, wo_ref, o_ref, yg_ref):
    c = pl.program_id(1)

    @pl.when(c == 0)
    def _():
        sg = sg_ref[...]
        for n, y_ref in enumerate((ya_ref, yb_ref, yc_ref, yd_ref)):
            g = sg[:, n * BRANCH_W:(n + 1) * BRANCH_W]
            yg_ref[n] = (y_ref[...] * (g * jax.nn.sigmoid(g))).astype(yg_ref.dtype)
        o_ref[...] = x_ref[...]

    h = h_ref[...]
    mixed = 0.0
    for n in range(N_BRANCH):
        gate = jax.nn.sigmoid(_dot_nt(h, wmt_ref[n]))
        z = jnp.dot(yg_ref[n], wb_ref[n], preferred_element_type=jnp.float32)
        mixed = mixed + gate * z
    o_ref[...] += jnp.dot(mixed.astype(jnp.bfloat16), wo_ref[...],
                          preferred_element_type=jnp.float32)


def _merge(x, h, proj, ys, wmerge_t, wbranch_all, wout_all, layer, tm=512, tn=256):
    s, d = x.shape
    nc = d // tn
    ni = s // tm
    row = lambda w: pl.BlockSpec((tm, w), lambda i, c: (i, 0))

    def early(w, c_from):
        return pl.BlockSpec((tm, w), lambda i, c: (jnp.minimum(i + (c >= c_from), ni - 1), 0))

    return pl.pallas_call(
        _merge_kernel,
        grid=(ni, nc),
        in_specs=[early(d, 3), row(d), early(N_BRANCH * BRANCH_W, 1),
                  early(BRANCH_W, 5), early(BRANCH_W, 5), early(BRANCH_W, 6), early(BRANCH_W, 6),
                  pl.BlockSpec((None, N_BRANCH, tn, d), lambda i, c: (layer, 0, c, 0)),
                  pl.BlockSpec((None, N_BRANCH, BRANCH_W, tn), lambda i, c: (layer, 0, 0, c)),
                  pl.BlockSpec((None, tn, d), lambda i, c: (layer, c, 0))],
        out_specs=row(d),
        out_shape=jax.ShapeDtypeStruct((s, d), jnp.float32),
        scratch_shapes=[pltpu.VMEM((N_BRANCH, tm, BRANCH_W), jnp.bfloat16)],
        compiler_params=_cparams(("parallel", "arbitrary")),
        name="merge",
    )(x, h, proj, *ys, wmerge_t, wbranch_all, wout_all)


def _rope_cos_sin(pos, dim):
    inv = ROPE_THETA ** (-jnp.arange(0, dim, 2, dtype=jnp.float32) / dim)
    ang = pos.astype(jnp.float32)[:, None] * inv[None, :]
    return jnp.cos(ang), jnp.sin(ang)


def _tables_a(s):
    cos, sin = _rope_cos_sin(jnp.arange(s, dtype=jnp.int32), A_ROPE)
    z16, z32 = jnp.zeros((s, 16), jnp.float32), jnp.zeros((s, 32), jnp.float32)
    one = jnp.ones((s, A_NOPE), jnp.float32)
    zero = jnp.zeros((s, A_NOPE), jnp.float32)
    c = jnp.concatenate([one, cos, cos, z32], axis=1)
    sa = jnp.concatenate([zero, -sin, z16, z32], axis=1)
    sb = jnp.concatenate([zero, z16, sin, z32], axis=1)
    return c, sa, sb


def _tables_b(s):
    rows = s // GRID_W
    row_pos = jnp.repeat(jnp.arange(rows, dtype=jnp.int32), GRID_W)
    col_pos = jnp.tile(jnp.arange(GRID_W, dtype=jnp.int32), rows)
    cr, sr = _rope_cos_sin(row_pos, HEAD_DIM // 2)
    cc, sc = _rope_cos_sin(col_pos, HEAD_DIM // 2)
    z = jnp.zeros_like(sr)
    c = jnp.concatenate([cr, cr, cc, cc] * 2, axis=1)
    sa = jnp.concatenate([-sr, z, -sc, z] * 2, axis=1)
    sb = jnp.concatenate([z, sr, z, sc] * 2, axis=1)
    return c, sa, sb


def _pad_lanes(v, width=LANES):
    return jnp.pad(v, ((0, width - v.shape[0]),))[None, :]


def kernel(x, norm_w, w_in, mla_q_norm, mla_kv_norm, mla_w_uq, mla_w_ukv, mla_qk_norm,
           gqa_qk_norm, dil_qk_norm, diff_qk_norm, diff_lambda, diff_subnorm, rel_bias,
           w_branch, w_out):
    b, s, d = x.shape
    assert b == 1 and d == D_MODEL
    depth = norm_w.shape[0]
    bf = jnp.bfloat16

    wt = jnp.swapaxes(w_in, 1, 2)
    seg = lambda a, b: wt[:, a:b].astype(bf)
    w_main_t = jnp.concatenate([seg(SRC_SILU, SRC_MERGE), seg(SRC_C, SRC_D),
                                seg(SRC_D, SRC_SILU), seg(SRC_B, SRC_C)], axis=1)
    w_a_t = jnp.pad(seg(SRC_A, SRC_B), ((0, 0), (0, A_BLOCK - SRC_B), (0, 0)))
    w_merge_t = seg(SRC_MERGE, SRC_END).reshape(depth, N_BRANCH, d, d)
    w_branch_b = w_branch.astype(bf)
    w_out_b = w_out.astype(bf)
    wuq = mla_w_uq.reshape(depth, Q_LORA, A_HEADS, A_QK)
    wuq = jnp.pad(wuq, ((0, 0), (0, 0), (0, 0), (0, LANES - A_QK)))
    wuq = wuq.reshape(depth, Q_LORA, A_HEADS * LANES).astype(bf)
    wukv = mla_w_ukv.astype(bf)

    tabs_a = _tables_a(s)
    tabs_b = _tables_b(s)
    bands_c = [_band(rel_bias, C_HEADS, C_QT, C_BAND_W, C_QT, pad=2 * C_QT, dil=dil,
                     half=C_HALF, col0=g * C_HEADS, key_axis=1, name=f"band_c{g}")
               for g, (_, dil) in enumerate(C_CONFIGS)]
    band_r = D_TQ + 2 * D_BAND
    band_d = _band(rel_bias, D_HEADS, band_r, D_TQ, band_r // 2, pad=D_BAND, dil=1, half=None,
                   col0=C_GROUPS * C_HEADS, key_axis=0, name="band_d")
    c_ut = [max(C_QT, min(s, 2048) // dil) for _, dil in C_CONFIGS]

    xs = x[0]
    for l in range(depth):
        h = _rmsnorm(xs, norm_w[l][None, :])
        proj = _in_proj(h, w_main_t, l)
        proj_a = _in_proj(h, w_a_t, l)

        qa, kta, va = _prep_a(proj_a, mla_q_norm[l][None, :], mla_kv_norm[l][None, :],
                              wuq[l], wukv[l], _pad_lanes(mla_qk_norm[l, 0]),
                              _pad_lanes(mla_qk_norm[l, 1]), tabs_a)
        y_a = _attn_a(qa, kta, va)

        g2 = lambda v: jnp.tile(v, 2)[None, :]
        qb, ktb, vb = _prep_b(proj, g2(gqa_qk_norm[l, 0]), g2(gqa_qk_norm[l, 1]), tabs_b)
        y_b = _attn_b(qb, ktb, vb)

        os, lses = [], []
        for g in range(C_GROUPS):
            qg, ktg, vg = _prep_c(proj, g2(dil_qk_norm[l, 0, g]), g2(dil_qk_norm[l, 1, g]),
                                  g, c_ut[g])
            og, lg = _attn_c(qg, ktg, vg, bands_c[g])
            os.append(og)
            lses.append(lg)
        y_c = _combine_c(os, lses, s)

        qd, ktd, vd = _prep_d(proj, g2(diff_qk_norm[l, 0]), g2(diff_qk_norm[l, 1]))
        lambda_init = 0.8 - 0.6 * math.exp(-0.3 * l)
        y_d = _attn_d(qd, ktd, vd, band_d, diff_lambda[l], diff_subnorm[l][None, :], lambda_init)

        xs = _merge(xs, h, proj, (y_a, y_b, y_c, y_d), w_merge_t, w_branch_b, w_out_b, l)
    return xs[None]
```

```python
import functools
import math

import numpy as np
import jax
import jax.numpy as jnp
from jax import lax
from jax.experimental import pallas as pl
from jax.experimental.pallas import tpu as pltpu

D_MODEL = 2048
GRID_W = 64
HEAD_DIM = 64
BRANCH_W = 512
N_BRANCH = 4
ROPE_THETA = 10000.0
EPS = 1e-6
NEG = -1e30

A_HEADS = 8
A_NOPE = 64
A_ROPE = 32
A_V = 64
A_QK = A_NOPE + A_ROPE
Q_LORA = 384
KV_LORA = 128

B_HEADS = 8
B_KV_HEADS = 2

C_HEADS = 8
C_CONFIGS = ((128, 1), (512, 4), (2048, 16))
C_GROUPS = len(C_CONFIGS)

D_HEADS = 4
D_V = 2 * HEAD_DIM

NUM_BUCKETS = 32
T5_MAX_DISTANCE = 1024

LANES = 128

SRC_A, SRC_B, SRC_C, SRC_D, SRC_SILU, SRC_MERGE, SRC_END = 0, 544, 1312, 5920, 7456, 9504, 17696

OFF_SILU = 0
OFF_CQ, OFF_CK, OFF_CV = 2048, 3584, 5120
OFF_DQ, OFF_DK, OFF_DV = 6656, 7168, 7680
OFF_BQ, OFF_BKV = 8192, 8704
A_BLOCK = 640

VMEM_LIMIT = 56 * 1024 * 1024

LOG2E = math.log2(math.e)

C_QT = 128
C_HALF = 64
C_DIL_MAX = max(d for _, d in C_CONFIGS)
C_BAND_W = 5 * C_QT

D_TQ = 512
D_TK = 512
D_BAND = 1152
D_SLOTS = 4


def _bucket_thresholds():
    nb = NUM_BUCKETS // 2
    max_exact = nb // 2
    n = np.arange(max_exact, 4 * T5_MAX_DISTANCE, dtype=np.float32)
    large = max_exact + (np.log(n / np.float32(max_exact))
                         / np.float32(math.log(T5_MAX_DISTANCE / max_exact))
                         * np.float32(nb - max_exact)).astype(np.int32)
    large = np.minimum(large, nb - 1)
    return tuple(int(n[np.argmax(large >= max_exact + k)]) for k in range(1, nb - max_exact))


BUCKET_STEPS = _bucket_thresholds()
assert D_BAND - D_TK + 1 >= BUCKET_STEPS[-1]


def _cparams(sem):
    return pltpu.CompilerParams(dimension_semantics=sem, vmem_limit_bytes=VMEM_LIMIT)


def _rmsnorm_kernel(x_ref, w_ref, o_ref):
    x = x_ref[...]
    ms = jnp.mean(x * x, axis=-1, keepdims=True)
    o_ref[...] = (x * lax.rsqrt(ms + EPS) * w_ref[...]).astype(o_ref.dtype)


def _rmsnorm(x, w, tm=512):
    s, d = x.shape
    return pl.pallas_call(
        _rmsnorm_kernel,
        grid=(s // tm,),
        in_specs=[pl.BlockSpec((tm, d), lambda i: (i, 0)),
                  pl.BlockSpec((1, d), lambda i: (0, 0))],
        out_specs=pl.BlockSpec((tm, d), lambda i: (i, 0)),
        out_shape=jax.ShapeDtypeStruct((s, d), jnp.bfloat16),
        compiler_params=_cparams(("parallel",)),
        name="rmsnorm",
    )(x, w)


def _dot_nt(a, bt):
    return lax.dot_general(a, bt, (((1,), (1,)), ((), ())), preferred_element_type=jnp.float32)


def _matmul_nt_kernel(a_ref, bt_ref, o_ref):
    o_ref[...] = _dot_nt(a_ref[...], bt_ref[...])


def _in_proj(h, wt_all, layer, tm=1024, tn=1792):
    s, d = h.shape
    n = wt_all.shape[1]
    tn = min(tn, n)
    return pl.pallas_call(
        _matmul_nt_kernel,
        grid=(s // tm, n // tn),
        in_specs=[pl.BlockSpec((tm, d), lambda i, j: (i, 0)),
                  pl.BlockSpec((None, tn, d), lambda i, j: (layer, j, 0))],
        out_specs=pl.BlockSpec((tm, tn), lambda i, j: (i, j)),
        out_shape=jax.ShapeDtypeStruct((s, n), jnp.float32),
        compiler_params=_cparams(("parallel", "arbitrary")),
        name="in_proj",
    )(h, wt_all)


def _band_kernel(tab_ref, o_ref, *, pad, dil, half, col0, key_axis):
    h = pl.program_id(0)
    rows, width = o_ref.shape
    col = lax.broadcasted_iota(jnp.int32, (rows, width), 1)
    row = lax.broadcasted_iota(jnp.int32, (rows, width), 0) + pl.program_id(1) * rows
    rel_u = (col - row if key_axis == 1 else row - col) - pad
    rel = rel_u * dil
    n = jnp.abs(rel)
    nb = NUM_BUCKETS // 2
    max_exact = nb // 2
    large = jnp.full((rows, width), max_exact, jnp.int32)
    for t in BUCKET_STEPS:
        large = large + jnp.where(n >= t, 1, 0)
    bucket = jnp.where(rel > 0, nb, 0) + jnp.where(n < max_exact, n, large)
    val = jnp.zeros((rows, width), jnp.float32)
    for b in range(NUM_BUCKETS):
        val = jnp.where(bucket == b, tab_ref[b, col0 + h], val)
    val = val * LOG2E
    if half is not None:
        val = jnp.where(jnp.abs(rel_u) <= half, val, NEG)
    o_ref[...] = val


def _band(rel_bias, heads, rows, width, rblk, *, pad, dil, half, col0, key_axis, name):
    return pl.pallas_call(
        functools.partial(_band_kernel, pad=pad, dil=dil, half=half, col0=col0,
                          key_axis=key_axis),
        grid=(heads, rows // rblk),
        in_specs=[pl.BlockSpec(memory_space=pltpu.SMEM)],
        out_specs=pl.BlockSpec((None, rblk, width), lambda h, j: (h, j, 0)),
        out_shape=jax.ShapeDtypeStruct((heads, rows, width), jnp.float32),
        compiler_params=_cparams(("parallel", "arbitrary")),
        name=name,
    )(rel_bias)


def _lane_lo(shape):
    return (lax.broadcasted_iota(jnp.int32, shape, len(shape) - 1) % LANES) < HEAD_DIM


def _rope(x, c, sa, sb):
    return x * c + pltpu.roll(x, LANES - 16, 1) * sa + pltpu.roll(x, 16, 1) * sb


def _halfnorm(x, gain, lo):
    sq = x * x
    s_lo = jnp.sum(jnp.where(lo, sq, 0.0), axis=-1, keepdims=True)
    s_hi = jnp.sum(jnp.where(lo, 0.0, sq), axis=-1, keepdims=True)
    r = jnp.where(lo, lax.rsqrt(s_lo * (1.0 / HEAD_DIM) + EPS),
                  lax.rsqrt(s_hi * (1.0 / HEAD_DIM) + EPS))
    return x * r * gain


ONES_ROWS = 16


def _prep_a_kernel(p_ref, qn_ref, kvn_ref, wuq_ref, wukv_ref, gq_ref, gk_ref,
                   c_ref, sa_ref, sb_ref, qt_out, k_out, vt_out):
    p = p_ref[...]
    cq = p[:, :Q_LORA]
    cq = cq * lax.rsqrt(jnp.mean(cq * cq, axis=-1, keepdims=True) + EPS) * qn_ref[...]
    q = jnp.dot(cq.astype(jnp.bfloat16), wuq_ref[...], preferred_element_type=jnp.float32)
    ckv = p[:, Q_LORA:Q_LORA + KV_LORA]
    ckv = ckv * lax.rsqrt(jnp.mean(ckv * ckv, axis=-1, keepdims=True) + EPS) * kvn_ref[...]
    kvu = jnp.dot(ckv.astype(jnp.bfloat16), wukv_ref[...], preferred_element_type=jnp.float32)
    kr = pltpu.roll(p[:, Q_LORA + KV_LORA:], HEAD_DIM, 1)
    lo = _lane_lo(kr.shape)
    c, sa, sb = c_ref[...], sa_ref[...], sb_ref[...]
    scale = A_QK ** -0.5 * LOG2E
    ones = jnp.ones((ONES_ROWS, p.shape[0]), vt_out.dtype)
    for h in range(A_HEADS):
        qh = q[:, h * LANES:(h + 1) * LANES]
        ss = jnp.sum(qh * qh, axis=-1, keepdims=True) * (1.0 / A_QK)
        qh = qh * lax.rsqrt(ss + EPS) * gq_ref[...]
        qt_out[h] = (_rope(qh, c, sa, sb) * scale).T.astype(qt_out.dtype)
        kvh = kvu[:, h * LANES:(h + 1) * LANES]
        kh = jnp.where(lo, kvh, kr)
        ss = jnp.sum(kh * kh, axis=-1, keepdims=True) * (1.0 / A_QK)
        kh = kh * lax.rsqrt(ss + EPS) * gk_ref[...]
        k_out[h] = _rope(kh, c, sa, sb).astype(k_out.dtype)
        vt_out[h, :A_V, :] = kvh.T[A_NOPE:].astype(vt_out.dtype)
        vt_out[h, A_V:, :] = ones


def _prep_a(proj, qn, kvn, wuq, wukv, gq, gk, tabs, tm=512):
    s = proj.shape[0]
    row = lambda w: pl.BlockSpec((1, w), lambda i: (0, 0))
    tab = pl.BlockSpec((tm, LANES), lambda i: (i, 0))
    return pl.pallas_call(
        _prep_a_kernel,
        grid=(s // tm,),
        in_specs=[pl.BlockSpec((tm, A_BLOCK), lambda i: (i, 0)),
                  row(Q_LORA), row(KV_LORA),
                  pl.BlockSpec(wuq.shape, lambda i: (0, 0)),
                  pl.BlockSpec(wukv.shape, lambda i: (0, 0)),
                  row(LANES), row(LANES), tab, tab, tab],
        out_specs=[pl.BlockSpec((A_HEADS, LANES, tm), lambda i: (0, 0, i)),
                   pl.BlockSpec((A_HEADS, tm, LANES), lambda i: (0, i, 0)),
                   pl.BlockSpec((A_HEADS, A_V + ONES_ROWS, tm), lambda i: (0, 0, i))],
        out_shape=[jax.ShapeDtypeStruct((A_HEADS, LANES, s), jnp.bfloat16),
                   jax.ShapeDtypeStruct((A_HEADS, s, LANES), jnp.bfloat16),
                   jax.ShapeDtypeStruct((A_HEADS, A_V + ONES_ROWS, s), jnp.bfloat16)],
        compiler_params=_cparams(("parallel",)),
        name="prep_a",
    )(proj, qn, kvn, wuq, wukv, gq, gk, *tabs)


B_TQ = 256


def _prep_b_kernel(q_ref, kv_ref, gq_ref, gk_ref, c_ref, sa_ref, sb_ref,
                   qt_out, k_out, vt_out):
    c, sa, sb = c_ref[...], sa_ref[...], sb_ref[...]
    lo = _lane_lo(c.shape)
    tm = c.shape[0]
    scale = HEAD_DIM ** -0.5 * LOG2E
    q = q_ref[...]
    for b in range(B_HEADS // 2):
        x = _rope(_halfnorm(q[:, b * LANES:(b + 1) * LANES], gq_ref[...], lo), c, sa, sb) * scale
        xr = pltpu.roll(x, HEAD_DIM, 1)
        g = b // 2
        even, odd = (x, xr) if g == 0 else (xr, x)
        keep = lo if g == 0 else jnp.logical_not(lo)
        for j, xh in enumerate((even, odd)):
            hh = 2 * (b % 2) + j
            qt_out[g, :, hh * tm:(hh + 1) * tm] = jnp.where(keep, xh, 0.0).T.astype(qt_out.dtype)
    kv = kv_ref[...]
    k_out[...] = _rope(_halfnorm(kv[:, :LANES], gk_ref[...], lo), c, sa, sb).astype(k_out.dtype)
    vt = kv[:, LANES:].T.astype(vt_out.dtype)
    ones = jnp.ones((ONES_ROWS, tm), vt_out.dtype)
    for g in range(B_KV_HEADS):
        vt_out[g, :HEAD_DIM, :] = vt[g * HEAD_DIM:(g + 1) * HEAD_DIM]
        vt_out[g, HEAD_DIM:, :] = ones


def _prep_b(proj, gq, gk, tabs, tm=B_TQ):
    s = proj.shape[0]
    row = pl.BlockSpec((1, LANES), lambda i: (0, 0))
    tab = pl.BlockSpec((tm, LANES), lambda i: (i, 0))
    rv = HEAD_DIM + ONES_ROWS
    return pl.pallas_call(
        _prep_b_kernel,
        grid=(s // tm,),
        in_specs=[pl.BlockSpec((tm, 512), lambda i: (i, OFF_BQ // 512)),
                  pl.BlockSpec((tm, 256), lambda i: (i, OFF_BKV // 256)),
                  row, row, tab, tab, tab],
        out_specs=[pl.BlockSpec((2, None, LANES, 4 * tm), lambda i: (0, i, 0, 0)),
                   pl.BlockSpec((tm, LANES), lambda i: (i, 0)),
                   pl.BlockSpec((2, rv, tm), lambda i: (0, 0, i))],
        out_shape=[jax.ShapeDtypeStruct((2, s // tm, LANES, 4 * tm), jnp.bfloat16),
                   jax.ShapeDtypeStruct((s, LANES), jnp.bfloat16),
                   jax.ShapeDtypeStruct((2, rv, s), jnp.bfloat16)],
        compiler_params=_cparams(("parallel",)),
        name="prep_b",
    )(proj, proj, gq, gk, *tabs)


def _prep_d_kernel(q_ref, k_ref, v_ref, gq_ref, gk_ref, qt_out, k_out, vt_out):
    q, k, v = q_ref[...], k_ref[...], v_ref[...]
    tm = q.shape[0]
    lo = _lane_lo((tm, LANES))
    scale = HEAD_DIM ** -0.5 * LOG2E
    ones = jnp.ones((ONES_ROWS, tm), vt_out.dtype)
    for h in range(D_HEADS):
        cols = slice(h * LANES, (h + 1) * LANES)
        x = _halfnorm(q[:, cols], gq_ref[...], lo) * scale
        qt_out[h, :, :tm] = jnp.where(lo, x, 0.0).T.astype(qt_out.dtype)
        qt_out[h, :, tm:] = jnp.where(lo, 0.0, x).T.astype(qt_out.dtype)
        k_out[:, cols] = _halfnorm(k[:, cols], gk_ref[...], lo).astype(k_out.dtype)
        vt_out[h, :D_V, :] = v[:, cols].T.astype(vt_out.dtype)
        vt_out[h, D_V:, :] = ones


def _prep_d(proj, gq, gk, tm=D_TQ):
    s = proj.shape[0]
    row = pl.BlockSpec((1, LANES), lambda i: (0, 0))
    blk = lambda off: pl.BlockSpec((tm, 512), lambda i: (i, off // 512))
    rv = D_V + ONES_ROWS
    return pl.pallas_call(
        _prep_d_kernel,
        grid=(s // tm,),
        in_specs=[blk(OFF_DQ), blk(OFF_DK), blk(OFF_DV), row, row],
        out_specs=[pl.BlockSpec((D_HEADS, None, LANES, 2 * tm), lambda i: (0, i, 0, 0)),
                   pl.BlockSpec((tm, 512), lambda i: (i, 0)),
                   pl.BlockSpec((D_HEADS, rv, tm), lambda i: (0, 0, i))],
        out_shape=[jax.ShapeDtypeStruct((D_HEADS, s // tm, LANES, 2 * tm), jnp.bfloat16),
                   jax.ShapeDtypeStruct((s, 512), jnp.bfloat16),
                   jax.ShapeDtypeStruct((D_HEADS, rv, s), jnp.bfloat16)],
        compiler_params=_cparams(("parallel",)),
        name="prep_d",
    )(proj, proj, proj, gq, gk)


def _prep_c_kernel(q_ref, k_ref, v_ref, gq_ref, gk_ref, q_out, kt_out, v_out, *, dil, ut):
    lo = _lane_lo((ut, LANES))
    scale = HEAD_DIM ** -0.5 * LOG2E
    for r in range(dil):
        rows = pl.ds(r, ut, stride=dil) if dil > 1 else pl.ds(0, ut)
        q_out[r] = (_halfnorm(q_ref[rows, :], gq_ref[...], lo) * scale).astype(q_out.dtype)
        kt_out[r] = _halfnorm(k_ref[rows, :], gk_ref[...], lo).T.astype(kt_out.dtype)
        v_out[r] = v_ref[rows, :].astype(v_out.dtype)


def _prep_c(proj, gq, gk, g, ut):
    s = proj.shape[0]
    dil = C_CONFIGS[g][1]
    seg = s // dil
    nb = C_HEADS // 2
    tn = ut * dil
    blk = lambda off: pl.BlockSpec((tn, LANES), lambda i, b: (i, off // LANES + g * nb + b))
    row = pl.BlockSpec((1, LANES), lambda i, b: (0, 0))
    return pl.pallas_call(
        functools.partial(_prep_c_kernel, dil=dil, ut=ut),
        grid=(s // tn, nb),
        in_specs=[blk(OFF_CQ), blk(OFF_CK), blk(OFF_CV), row, row],
        out_specs=[pl.BlockSpec((dil, ut, LANES), lambda i, b: (0, i, b)),
                   pl.BlockSpec((dil, LANES, ut), lambda i, b: (0, b, i)),
                   pl.BlockSpec((dil, ut, LANES), lambda i, b: (0, i, b))],
        out_shape=[jax.ShapeDtypeStruct((dil, seg, nb * LANES), jnp.bfloat16),
                   jax.ShapeDtypeStruct((dil, nb * LANES, seg), jnp.bfloat16),
                   jax.ShapeDtypeStruct((dil, seg, nb * LANES), jnp.bfloat16)],
        compiler_params=_cparams(("parallel", "parallel")),
        name=f"prep_c{g}",
    )(proj, proj, proj, gq, gk)


def _scores(k, qt, s_ref, mc_ref, slot, bias=None):
    st = jnp.dot(k, qt, preferred_element_type=jnp.float32)
    if bias is not None:
        st = st + bias
    s_ref[slot] = st
    mc_ref[slot] = jnp.max(st, axis=0, keepdims=True)


def _consume(s_ref, mc_ref, slot, vt, m_ref, acc_ref):
    m_prev = m_ref[...]
    m_next = jnp.maximum(m_prev, mc_ref[slot])
    p = jnp.exp2(s_ref[slot] - m_next)
    alpha = jnp.exp2(m_prev - m_next)
    m_ref[...] = m_next
    pv = jnp.dot(vt, p.astype(jnp.bfloat16), preferred_element_type=jnp.float32)
    acc_ref[...] = alpha * acc_ref[...] + pv


FLASH_UNROLL = 4


def _flash_loop(nk, score, consume, slots=2):
    u = min(FLASH_UNROLL, nk)
    ahead = slots - 1
    assert nk % u == 0 and u % slots == 0
    for c in range(ahead):
        score(c, c % slots)

    def body(j, carry):
        for i in range(u):
            score(j * u + i + ahead, (i + ahead) % slots)
            consume(j * u + i, i % slots)
        return carry

    lax.fori_loop(0, nk // u - 1, body, 0)
    for c in range(nk - u, nk):
        if c + ahead < nk:
            score(c + ahead, (c + ahead) % slots)
        consume(c, c % slots)


def _init_stats(m_ref, acc_ref):
    m_ref[...] = jnp.full_like(m_ref, -jnp.inf)
    acc_ref[...] = jnp.zeros_like(acc_ref)


def _normalised(acc_ref, rv):
    acc = acc_ref[...]
    return acc[:rv] / acc[rv:rv + 1]


def _flash_scratch(streams, tk, n, rv, slots=2):
    return ([pltpu.VMEM((slots, tk, n), jnp.float32)] * streams
            + [pltpu.VMEM((slots, 1, n), jnp.float32)] * streams
            + [pltpu.VMEM((1, n), jnp.float32)] * streams
            + [pltpu.VMEM((rv, n), jnp.float32)] * streams)


def _attn_a_kernel(qt_ref, k_ref, vt_ref, o_ref, s0_ref, s1_ref, mc0_ref, mc1_ref,
                   m0_ref, m1_ref, acc0_ref, acc1_ref, *, tk, nk):
    streams = ((s0_ref, mc0_ref, m0_ref, acc0_ref), (s1_ref, mc1_ref, m1_ref, acc1_ref))
    for _, _, m_ref, acc_ref in streams:
        _init_stats(m_ref, acc_ref)

    def score(c, slot):
        k0 = pl.multiple_of(c * tk, tk)
        for hh, (s_ref, mc_ref, _, _) in enumerate(streams):
            _scores(k_ref[hh, pl.ds(k0, tk), :], qt_ref[hh], s_ref, mc_ref, slot)

    def consume(c, slot):
        k0 = pl.multiple_of(c * tk, tk)
        for hh, (s_ref, mc_ref, m_ref, acc_ref) in enumerate(streams):
            _consume(s_ref, mc_ref, slot, vt_ref[hh, :, pl.ds(k0, tk)], m_ref, acc_ref)

    _flash_loop(nk, score, consume)
    o_ref[...] = jnp.concatenate([_normalised(acc0_ref, A_V), _normalised(acc1_ref, A_V)],
                                 axis=0).T


def _attn_a(qt, k, vt, tq=512, tk=512):
    s = k.shape[1]
    tk = min(tk, s // 2)
    rv = vt.shape[1]
    return pl.pallas_call(
        functools.partial(_attn_a_kernel, tk=tk, nk=s // tk),
        grid=(A_HEADS // 2, s // tq),
        in_specs=[pl.BlockSpec((2, LANES, tq), lambda p, i: (p, 0, i)),
                  pl.BlockSpec((2, s, LANES), lambda p, i: (p, 0, 0)),
                  pl.BlockSpec((2, rv, s), lambda p, i: (p, 0, 0))],
        out_specs=pl.BlockSpec((tq, LANES), lambda p, i: (i, p)),
        out_shape=jax.ShapeDtypeStruct((s, BRANCH_W), jnp.float32),
        scratch_shapes=_flash_scratch(2, tk, tq, rv),
        compiler_params=_cparams(("parallel", "arbitrary")),
        name="attn_a",
    )(qt, k, vt)


def _attn_b_kernel(qt_ref, k_ref, vt_ref, o_ref, s_ref, mc_ref, m_ref, acc_ref, *, tq, tk, nk):
    _init_stats(m_ref, acc_ref)

    def score(c, slot):
        _scores(k_ref[pl.ds(pl.multiple_of(c * tk, tk), tk), :], qt_ref[...], s_ref, mc_ref, slot)

    def consume(c, slot):
        _consume(s_ref, mc_ref, slot, vt_ref[:, pl.ds(pl.multiple_of(c * tk, tk), tk)],
                 m_ref, acc_ref)

    _flash_loop(nk, score, consume)
    ot = _normalised(acc_ref, HEAD_DIM)
    o_ref[...] = jnp.concatenate([ot[:, h * tq:(h + 1) * tq] for h in range(4)], axis=0).T


def _attn_b(qt, k, vt, tq=B_TQ, tk=512):
    s = k.shape[0]
    tk = min(tk, s // 2)
    rv = vt.shape[1]
    return pl.pallas_call(
        functools.partial(_attn_b_kernel, tq=tq, tk=tk, nk=s // tk),
        grid=(2, s // tq),
        in_specs=[pl.BlockSpec((None, None, LANES, 4 * tq), lambda g, i: (g, i, 0, 0)),
                  pl.BlockSpec((s, LANES), lambda g, i: (0, 0)),
                  pl.BlockSpec((None, rv, s), lambda g, i: (g, 0, 0))],
        out_specs=pl.BlockSpec((tq, 2 * LANES), lambda g, i: (i, g)),
        out_shape=jax.ShapeDtypeStruct((s, BRANCH_W), jnp.float32),
        scratch_shapes=_flash_scratch(1, tk, 4 * tq, rv),
        compiler_params=_cparams(("parallel", "arbitrary")),
        name="attn_b",
    )(qt, k, vt)


def _attn_d_kernel(qt_ref, k_ref, vt_ref, band_ref, lam_ref, sub_ref, o_ref,
                   s_ref, mc_ref, m_ref, acc_ref, *, tq, tk, nk, lambda_init):
    _init_stats(m_ref, acc_ref)
    q0 = pl.program_id(1) * tq
    hi = tq + 2 * D_BAND - tk

    def score(c, slot):
        k0 = pl.multiple_of(c * tk, tk)
        b0 = pl.multiple_of(jnp.clip(k0 - q0 + D_BAND, 0, hi), LANES)
        bias = band_ref[pl.ds(b0, tk), :]
        _scores(k_ref[pl.ds(k0, tk), :], qt_ref[...], s_ref, mc_ref, slot,
                jnp.concatenate([bias, bias], axis=1))

    def consume(c, slot):
        _consume(s_ref, mc_ref, slot, vt_ref[:, pl.ds(pl.multiple_of(c * tk, tk), tk)],
                 m_ref, acc_ref)

    _flash_loop(nk, score, consume, D_SLOTS)
    lv = lam_ref[...]
    lam = (jnp.exp(jnp.sum(lv[0:1] * lv[1:2], axis=-1, keepdims=True))
           - jnp.exp(jnp.sum(lv[2:3] * lv[3:4], axis=-1, keepdims=True)) + lambda_init)
    ot = _normalised(acc_ref, D_V)
    o = (ot[:, :tq] - lam * ot[:, tq:]).T
    o = o * lax.rsqrt(jnp.mean(o * o, axis=-1, keepdims=True) + EPS) * sub_ref[...]
    o_ref[...] = o * (1.0 - lambda_init)


def _attn_d(qt, k, vt, band, lam_vecs, subw, lambda_init, tq=D_TQ, tk=D_TK):
    s = k.shape[0]
    rv = vt.shape[1]
    return pl.pallas_call(
        functools.partial(_attn_d_kernel, tq=tq, tk=tk, nk=s // tk, lambda_init=lambda_init),
        grid=(D_HEADS, s // tq),
        in_specs=[pl.BlockSpec((None, None, LANES, 2 * tq), lambda h, i: (h, i, 0, 0)),
                  pl.BlockSpec((s, LANES), lambda h, i: (0, h)),
                  pl.BlockSpec((None, rv, s), lambda h, i: (h, 0, 0)),
                  pl.BlockSpec((None, tq + 2 * D_BAND, tq), lambda h, i: (h, 0, 0)),
                  pl.BlockSpec((4, HEAD_DIM), lambda h, i: (0, 0)),
                  pl.BlockSpec((1, LANES), lambda h, i: (0, 0))],
        out_specs=pl.BlockSpec((tq, LANES), lambda h, i: (i, h)),
        out_shape=jax.ShapeDtypeStruct((s, BRANCH_W), jnp.float32),
        scratch_shapes=_flash_scratch(1, tk, 2 * tq, rv, D_SLOTS),
        compiler_params=_cparams(("parallel", "arbitrary")),
        name="attn_d",
    )(qt, k, vt, band, lam_vecs, subw)


def _attn_c_kernel(q_ref, kt_ref, v_ref, band_ref, o_ref, lse_ref, *, nt, seg, win, unroll):
    j = pl.program_id(2)
    lo = _lane_lo((C_QT, LANES))

    def scores(t):
        r0 = pl.multiple_of(t * C_QT, C_QT)
        u0 = j * (nt * C_QT) + r0
        ws = pl.multiple_of(jnp.clip(u0 - C_QT, 0, seg - win), C_QT)
        x = pl.multiple_of(ws - u0 + 2 * C_QT, C_QT)
        q = q_ref[pl.ds(r0, C_QT), :]
        zero = jnp.zeros_like(q)
        q2 = jnp.concatenate([jnp.where(lo, q, zero), jnp.where(lo, zero, q)], axis=0)
        s = jnp.dot(q2, kt_ref[:, pl.ds(ws, win)], preferred_element_type=jnp.float32)
        s = s + jnp.concatenate([band_ref[0, :, pl.ds(x, win)], band_ref[1, :, pl.ds(x, win)]], axis=0)
        return r0, ws, s

    def finish(r0, ws, s):
        m = jnp.max(s, axis=1, keepdims=True)
        p = jnp.exp2(s - m)
        l = jnp.sum(p, axis=1, keepdims=True)
        pv = jnp.dot(p.astype(jnp.bfloat16), v_ref[pl.ds(ws, win), :],
                     preferred_element_type=jnp.float32)
        o = pv / l
        lse = m + jnp.log2(l)
        o_ref[pl.ds(r0, C_QT), :] = jnp.where(lo, o[:C_QT], o[C_QT:])
        lse_ref[pl.ds(r0, C_QT), :] = jnp.where(lo, lse[:C_QT], lse[C_QT:])

    def body(tt, carry):
        tiles = [scores(tt * unroll + i) for i in range(unroll)]
        for tile in tiles:
            finish(*tile)
        return carry

    lax.fori_loop(0, nt // unroll, body, 0)


def _attn_c(q, kt, v, band, nt_max=8):
    dil, seg, w = q.shape
    nt = min(nt_max, seg // C_QT)
    win = min(3 * C_QT, seg)
    tqb = nt * C_QT
    out = jax.ShapeDtypeStruct((dil, seg, w), jnp.float32)
    ospec = pl.BlockSpec((None, tqb, LANES), lambda r, b, j: (r, j, b))
    return pl.pallas_call(
        functools.partial(_attn_c_kernel, nt=nt, seg=seg, win=win, unroll=min(4, nt)),
        grid=(dil, C_HEADS // 2, seg // tqb),
        in_specs=[pl.BlockSpec((None, tqb, LANES), lambda r, b, j: (r, j, b)),
                  pl.BlockSpec((None, LANES, seg), lambda r, b, j: (r, b, 0)),
                  pl.BlockSpec((None, seg, LANES), lambda r, b, j: (r, 0, b)),
                  pl.BlockSpec((2, C_QT, C_BAND_W), lambda r, b, j: (b, 0, 0))],
        out_specs=[ospec, ospec],
        out_shape=[out, out],
        compiler_params=_cparams(("parallel", "parallel", "arbitrary")),
        name=f"attn_c_d{dil}",
    )(q, kt, v, band)


def _combine_c_kernel(*refs):
    o_refs, l_refs, out_ref = refs[0:2 * C_GROUPS:2], refs[1:2 * C_GROUPS:2], refs[-1]
    n = out_ref.shape[0] // C_DIL_MAX
    for r in range(C_DIL_MAX):
        outs, lses = [], []
        for (_, dil), o_ref, l_ref in zip(C_CONFIGS, o_refs, l_refs):
            rows = pl.ds(r // dil, n, stride=C_DIL_MAX // dil)
            cls = pl.ds(r % dil, 1)
            outs.append(o_ref[cls, rows, :][0])
            lses.append(l_ref[cls, rows, :][0])
        m = functools.reduce(jnp.maximum, lses)
        es = [jnp.exp2(l - m) for l in lses]
        num = sum(e * o for e, o in zip(es, outs))
        out_ref[pl.ds(r, n, stride=C_DIL_MAX), :] = num / sum(es)


def _combine_c(os, lses, s, tn=2048):
    nb = C_HEADS // 2
    args, in_specs = [], []
    for (_, dil), o, l in zip(C_CONFIGS, os, lses):
        spec = pl.BlockSpec((dil, tn // dil, LANES), lambda i, b: (0, i, b))
        args += [o, l]
        in_specs += [spec, spec]
    return pl.pallas_call(
        _combine_c_kernel,
        grid=(s // tn, nb),
        in_specs=in_specs,
        out_specs=pl.BlockSpec((tn, LANES), lambda i, b: (i, b)),
        out_shape=jax.ShapeDtypeStruct((s, nb * LANES), jnp.float32),
        compiler_params=_cparams(("parallel", "parallel")),
        name="combine_c",
    )(*args)


def _merge_kernel(x_ref, h_ref, sg_ref, ya_ref, yb_ref, yc_ref, yd_ref,
                  wmt_ref, wb_ref, wo_ref, o_ref, yg_ref):
    c = pl.program_id(1)

    @pl.when(c == 0)
    def _():
        sg = sg_ref[...]
        for n, y_ref in enumerate((ya_ref, yb_ref, yc_ref, yd_ref)):
            g = sg[:, n * BRANCH_W:(n + 1) * BRANCH_W]
            yg_ref[n] = (y_ref[...] * (g * jax.nn.sigmoid(g))).astype(yg_ref.dtype)
        o_ref[...] = x_ref[...]

    h = h_ref[...]
    mixed = 0.0
    for n in range(N_BRANCH):
        gate = jax.nn.sigmoid(_dot_nt(h, wmt_ref[n]))
        z = _dot_nt(yg_ref[n], wb_ref[n])
        mixed = mixed + gate * z
    o_ref[...] += jnp.dot(mixed.astype(jnp.bfloat16), wo_ref[...],
                          preferred_element_type=jnp.float32)


def _merge(x, h, proj, ys, wmerge_t, wbranch_all, wout_all, layer, tm=512, tn=256):
    s, d = x.shape
    nc = d // tn
    ni = s // tm
    row = lambda w: pl.BlockSpec((tm, w), lambda i, c: (i, 0))

    def early(w, c_from):
        return pl.BlockSpec((tm, w), lambda i, c: (jnp.minimum(i + (c >= c_from), ni - 1), 0))

    return pl.pallas_call(
        _merge_kernel,
        grid=(ni, nc),
        in_specs=[early(d, 3), row(d), early(N_BRANCH * BRANCH_W, 1),
                  early(BRANCH_W, 5), early(BRANCH_W, 5), early(BRANCH_W, 6), early(BRANCH_W, 6),
                  pl.BlockSpec((None, N_BRANCH, tn, d), lambda i, c: (layer, 0, c, 0)),
                  pl.BlockSpec((None, N_BRANCH, tn, BRANCH_W), lambda i, c: (layer, 0, c, 0)),
                  pl.BlockSpec((None, tn, d), lambda i, c: (layer, c, 0))],
        out_specs=row(d),
        out_shape=jax.ShapeDtypeStruct((s, d), jnp.float32),
        scratch_shapes=[pltpu.VMEM((N_BRANCH, tm, BRANCH_W), jnp.bfloat16)],
        compiler_params=_cparams(("parallel", "arbitrary")),
        name="merge",
    )(x, h, proj, *ys, wmerge_t, wbranch_all, wout_all)


def _rope_cos_sin(pos, dim):
    inv = ROPE_THETA ** (-jnp.arange(0, dim, 2, dtype=jnp.float32) / dim)
    ang = pos.astype(jnp.float32)[:, None] * inv[None, :]
    return jnp.cos(ang), jnp.sin(ang)


def _tables_a(s):
    cos, sin = _rope_cos_sin(jnp.arange(s, dtype=jnp.int32), A_ROPE)
    z16, z32 = jnp.zeros((s, 16), jnp.float32), jnp.zeros((s, 32), jnp.float32)
    one = jnp.ones((s, A_NOPE), jnp.float32)
    zero = jnp.zeros((s, A_NOPE), jnp.float32)
    c = jnp.concatenate([one, cos, cos, z32], axis=1)
    sa = jnp.concatenate([zero, -sin, z16, z32], axis=1)
    sb = jnp.concatenate([zero, z16, sin, z32], axis=1)
    return c, sa, sb


def _tables_b(s):
    rows = s // GRID_W
    row_pos = jnp.repeat(jnp.arange(rows, dtype=jnp.int32), GRID_W)
    col_pos = jnp.tile(jnp.arange(GRID_W, dtype=jnp.int32), rows)
    cr, sr = _rope_cos_sin(row_pos, HEAD_DIM // 2)
    cc, sc = _rope_cos_sin(col_pos, HEAD_DIM // 2)
    z = jnp.zeros_like(sr)
    c = jnp.concatenate([cr, cr, cc, cc] * 2, axis=1)
    sa = jnp.concatenate([-sr, z, -sc, z] * 2, axis=1)
    sb = jnp.concatenate([z, sr, z, sc] * 2, axis=1)
    return c, sa, sb


def _pad_lanes(v, width=LANES):
    return jnp.pad(v, ((0, width - v.shape[0]),))[None, :]


def kernel(x, norm_w, w_in, mla_q_norm, mla_kv_norm, mla_w_uq, mla_w_ukv, mla_qk_norm,
           gqa_qk_norm, dil_qk_norm, diff_qk_norm, diff_lambda, diff_subnorm, rel_bias,
           w_branch, w_out):
    b, s, d = x.shape
    assert b == 1 and d == D_MODEL
    depth = norm_w.shape[0]
    bf = jnp.bfloat16

    wt = jnp.swapaxes(w_in, 1, 2)
    seg = lambda a, b: wt[:, a:b].astype(bf)
    w_main_t = jnp.concatenate([seg(SRC_SILU, SRC_MERGE), seg(SRC_C, SRC_D),
                                seg(SRC_D, SRC_SILU), seg(SRC_B, SRC_C)], axis=1)
    w_a_t = jnp.pad(seg(SRC_A, SRC_B), ((0, 0), (0, A_BLOCK - SRC_B), (0, 0)))
    w_merge_t = seg(SRC_MERGE, SRC_END).reshape(depth, N_BRANCH, d, d)
    w_branch_b = jnp.swapaxes(w_branch, 2, 3).astype(bf)
    w_out_b = w_out.astype(bf)
    wuq = mla_w_uq.reshape(depth, Q_LORA, A_HEADS, A_QK)
    wuq = jnp.pad(wuq, ((0, 0), (0, 0), (0, 0), (0, LANES - A_QK)))
    wuq = wuq.reshape(depth, Q_LORA, A_HEADS * LANES).astype(bf)
    wukv = mla_w_ukv.astype(bf)

    tabs_a = _tables_a(s)
    tabs_b = _tables_b(s)
    bands_c = [_band(rel_bias, C_HEADS, C_QT, C_BAND_W, C_QT, pad=2 * C_QT, dil=dil,
                     half=C_HALF, col0=g * C_HEADS, key_axis=1, name=f"band_c{g}")
               for g, (_, dil) in enumerate(C_CONFIGS)]
    band_r = D_TQ + 2 * D_BAND
    band_d = _band(rel_bias, D_HEADS, band_r, D_TQ, band_r // 2, pad=D_BAND, dil=1, half=None,
                   col0=C_GROUPS * C_HEADS, key_axis=0, name="band_d")
    c_ut = [max(C_QT, min(s, 2048) // dil) for _, dil in C_CONFIGS]

    xs = x[0]
    for l in range(depth):
        h = _rmsnorm(xs, norm_w[l][None, :])
        proj = _in_proj(h, w_main_t, l)
        proj_a = _in_proj(h, w_a_t, l)

        qa, kta, va = _prep_a(proj_a, mla_q_norm[l][None, :], mla_kv_norm[l][None, :],
                              wuq[l], wukv[l], _pad_lanes(mla_qk_norm[l, 0]),
                              _pad_lanes(mla_qk_norm[l, 1]), tabs_a)
        y_a = _attn_a(qa, kta, va)

        g2 = lambda v: jnp.tile(v, 2)[None, :]
        qb, ktb, vb = _prep_b(proj, g2(gqa_qk_norm[l, 0]), g2(gqa_qk_norm[l, 1]), tabs_b)
        y_b = _attn_b(qb, ktb, vb)

        os, lses = [], []
        for g in range(C_GROUPS):
            qg, ktg, vg = _prep_c(proj, g2(dil_qk_norm[l, 0, g]), g2(dil_qk_norm[l, 1, g]),
                                  g, c_ut[g])
            og, lg = _attn_c(qg, ktg, vg, bands_c[g])
            os.append(og)
            lses.append(lg)
        y_c = _combine_c(os, lses, s)

        qd, ktd, vd = _prep_d(proj, g2(diff_qk_norm[l, 0]), g2(diff_qk_norm[l, 1]))
        lambda_init = 0.8 - 0.6 * math.exp(-0.3 * l)
        y_d = _attn_d(qd, ktd, vd, band_d, diff_lambda[l], diff_subnorm[l][None, :], lambda_init)

        xs = _merge(xs, h, proj, (y_a, y_b, y_c, y_d), w_merge_t, w_branch_b, w_out_b, l)
    return xs[None]
```
